```python
import math
import jax, jax.numpy as jnp
from jax import lax
import numpy as np

D_MODEL = 1024
BATCH = 4
SEQ = 4096
DEPTH = 1

CHUNK = 64
MIX_WIDTH = D_MODEL
POOL_WIDTH = MIX_WIDTH // 2
POOL_WINDOWS = (2, 4, 8, 16)
POOL_GROUP = POOL_WIDTH // len(POOL_WINDOWS)
ATTN_HEADS = 8
HEAD_DIM = (MIX_WIDTH - POOL_WIDTH) // ATTN_HEADS
ATTN_WIDTH = ATTN_HEADS * HEAD_DIM
IDX_HEADS = 8
IDX_DIM = 64
TOPK_MAX = 256
Q_BLOCK = 128
ROPE_THETA = 10000.0
MEM_TOKENS = 256
MEM_HEADS = 4
MEM_HEAD_DIM = 128
MEM_WIDTH = MEM_HEADS * MEM_HEAD_DIM
N_GROUPS = 4
EXPERTS_PER_GROUP = 8
EXPERT_FF = 256
TOP_K_EXPERTS = 2
EPS = 1e-6
OFF_POOL = 0
OFF_Q = OFF_POOL + POOL_WIDTH
OFF_K = OFF_Q + ATTN_WIDTH
OFF_V = OFF_K + ATTN_WIDTH
OFF_QI = OFF_V + ATTN_WIDTH
OFF_KI = OFF_QI + IDX_HEADS * IDX_DIM
OFF_WI = OFF_KI + IDX_DIM
IN_COLS = OFF_WI + IDX_HEADS

kernel_name = "hybrid_pool_dsa_hmoe_block"


def rmsnorm(x, g):
    xf = x.astype(jnp.float32)
    y = xf * lax.rsqrt(jnp.mean(xf * xf, axis=-1, keepdims=True) + EPS)
    return (y * g.astype(jnp.float32)).astype(x.dtype)


def rope_tables(seq_len, dim):
    inv = ROPE_THETA ** (-jnp.arange(0, dim, 2, dtype=jnp.float32) / dim)
    ang = jnp.arange(seq_len, dtype=jnp.float32)[:, None] * inv[None, :]
    ang = jnp.concatenate([ang, ang], axis=-1)
    return jnp.cos(ang), jnp.sin(ang)


def apply_rope(x, cos, sin):
    xf = x.astype(jnp.float32)
    half = xf.shape[-1] // 2
    rot = jnp.concatenate([-xf[..., half:], xf[..., :half]], axis=-1)
    return (xf * cos[None, :, None, :] + rot * sin[None, :, None, :]).astype(x.dtype)


def pool_mixer(u, pool_w, pool_scale):
    B, S, _ = u.shape
    uf = u.astype(jnp.float32)
    pos1 = jnp.arange(1, S + 1)
    outs = []
    for g, w in enumerate(POOL_WINDOWS):
        ug = uf[..., g * POOL_GROUP:(g + 1) * POOL_GROUP]
        cs = jnp.cumsum(ug, axis=1)
        lag = jnp.pad(cs, ((0, 0), (w, 0), (0, 0)))[:, :S]
        cnt = jnp.minimum(pos1, w).astype(jnp.float32)[None, :, None]
        mixed = ((cs - lag) / cnt - ug).astype(u.dtype)
        outs.append(jnp.einsum('bsc,cd->bsd', mixed, pool_w[g]))
    return jnp.concatenate(outs, axis=-1) * pool_scale


def dsa_attention(q, k, v, qi, ki, wi):
    B, S, H, Dh = q.shape
    topk = min(TOPK_MAX, S // 4)
    n_blocks = S // Q_BLOCK
    key_chunk = jnp.arange(S) // CHUNK
    idx_scale = (IDX_DIM ** -0.5) * (IDX_HEADS ** -0.5)
    gather = jax.vmap(lambda a, ix: a[ix])

    def block(i):
        start = i * Q_BLOCK
        qb = lax.dynamic_slice_in_dim(q, start, Q_BLOCK, axis=1)
        qib = lax.dynamic_slice_in_dim(qi, start, Q_BLOCK, axis=1)
        wib = lax.dynamic_slice_in_dim(wi, start, Q_BLOCK, axis=1)
        q_chunk = (start + jnp.arange(Q_BLOCK)) // CHUNK
        adm = key_chunk[None, :] <= q_chunk[:, None]
        rel = jax.nn.relu(jnp.einsum('bthd,bsd->bths', qib, ki).astype(jnp.float32))
        score = jnp.einsum('bths,bth->bts', rel, wib.astype(jnp.float32)) * idx_scale
        score = jnp.where(adm[None], score, -jnp.inf)
        _, sel = lax.top_k(score, topk)
        valid = key_chunk[sel] <= q_chunk[None, :, None]
        ks = gather(k, sel)
        vs = gather(v, sel)
        s = jnp.einsum('bthd,btkhd->bthk', qb, ks).astype(jnp.float32) * (Dh ** -0.5)
        s = jnp.where(valid[:, :, None, :], s, -jnp.inf)
        p = jax.nn.softmax(s, axis=-1).astype(v.dtype)
        return jnp.einsum('bthk,btkhd->bthd', p, vs)

    out = lax.map(block, jnp.arange(n_blocks))
    return out.transpose(1, 0, 2, 3, 4).reshape(B, S, H * Dh)


def memory_cross_attention(h, mem, g_mem, w_q, w_kv, q_norm, k_norm, w_o):
    B, S, _ = h.shape
    M = mem.shape[1]
    m = rmsnorm(mem, g_mem)
    q = (h @ w_q).reshape(B, S, MEM_HEADS, MEM_HEAD_DIM)
    kv = (m @ w_kv).reshape(B, M, 2, MEM_HEADS, MEM_HEAD_DIM)
    q = rmsnorm(q, q_norm)
    k = rmsnorm(kv[:, :, 0], k_norm)
    v = kv[:, :, 1]
    s = jnp.einsum('bshd,bmhd->bhsm', q, k).astype(jnp.float32) * (MEM_HEAD_DIM ** -0.5)
    p = jax.nn.softmax(s, axis=-1).astype(v.dtype)
    o = jnp.einsum('bhsm,bmhd->bshd', p, v).reshape(B, S, MEM_WIDTH)
    return o @ w_o


def hierarchical_moe(h, wg_router, bg_router, we_router, be_router, w_gate, w_up, w_down):
    B, S, D = h.shape
    hf = h.reshape(B * S, D)
    g_logits = (hf @ wg_router + bg_router).astype(jnp.float32)
    g_prob = jax.nn.softmax(g_logits, axis=-1)
    g_sel = jnp.argmax(g_logits, axis=-1)
    g_w = jnp.take_along_axis(g_prob, g_sel[:, None], axis=1)
    e_logits = (hf @ we_router + be_router).astype(jnp.float32)
    e_logits = e_logits.reshape(-1, N_GROUPS, EXPERTS_PER_GROUP)
    e_in = jnp.take_along_axis(e_logits, g_sel[:, None, None], axis=1)[:, 0]
    top_v, top_i = lax.top_k(e_in, TOP_K_EXPERTS)
    top_w = jax.nn.softmax(top_v, axis=-1) * g_w
    exp_w = jnp.sum(jax.nn.one_hot(top_i, EXPERTS_PER_GROUP, dtype=jnp.float32)
                    * top_w[..., None], axis=1)
    gates = jax.nn.one_hot(g_sel, N_GROUPS, dtype=jnp.float32)[:, :, None] * exp_w[:, None, :]
    out = jnp.zeros_like(hf)
    for g in range(N_GROUPS):
        a = jnp.einsum('nd,edf->nef', hf, w_gate[g])
        b = jnp.einsum('nd,edf->nef', hf, w_up[g])
        hid = jax.nn.silu(a) * b * gates[:, g, :, None].astype(hf.dtype)
        out = out + jnp.einsum('nef,efd->nd', hid, w_down[g])
    return out.reshape(B, S, D)


def setup_inputs(seed: int = 0) -> dict:
    key = jax.random.key(seed)
    ks = jax.random.split(key, 32)
    f32 = jnp.float32
    L = DEPTH

    def nrm(k, shape, scale):
        return jax.random.normal(k, shape, f32) * scale

    def gain(k, shape):
        return 1.0 + 0.02 * jax.random.normal(k, shape, f32)

    return {
        "x": nrm(ks[0], (BATCH, SEQ, D_MODEL), 1.0),
        "mem": nrm(ks[1], (BATCH, MEM_TOKENS, D_MODEL), 1.0),
        "mix_norm": gain(ks[2], (L, D_MODEL)),
        "w_in": nrm(ks[3], (L, D_MODEL, IN_COLS), D_MODEL ** -0.5),
        "pool_w": nrm(ks[4], (L, len(POOL_WINDOWS), POOL_GROUP, POOL_GROUP), POOL_GROUP ** -0.5),
        "pool_scale": gain(ks[5], (L, POOL_WIDTH)),
        "q_norm": gain(ks[6], (L, HEAD_DIM)),
        "k_norm": gain(ks[7], (L, HEAD_DIM)),
        "idx_k_norm": gain(ks[8], (L, IDX_DIM)),
        "w_out": nrm(ks[9], (L, MIX_WIDTH, D_MODEL), MIX_WIDTH ** -0.5),
        "xattn_norm": gain(ks[10], (L, D_MODEL)),
        "mem_norm": gain(ks[11], (L, D_MODEL)),
        "xattn_wq": nrm(ks[12], (L, D_MODEL, MEM_WIDTH), D_MODEL ** -0.5),
        "xattn_wkv": nrm(ks[13], (L, D_MODEL, 2 * MEM_WIDTH), D_MODEL ** -0.5),
        "xattn_q_norm": gain(ks[14], (L, MEM_HEAD_DIM)),
        "xattn_k_norm": gain(ks[15], (L, MEM_HEAD_DIM)),
        "xattn_wo": nrm(ks[16], (L, MEM_WIDTH, D_MODEL), MEM_WIDTH ** -0.5),
        "ffn_norm": gain(ks[17], (L, D_MODEL)),
        "router_group_w": nrm(ks[18], (L, D_MODEL, N_GROUPS), D_MODEL ** -0.5),
        "router_group_b": nrm(ks[19], (L, N_GROUPS), 0.01),
        "router_expert_w": nrm(ks[20], (L, D_MODEL, N_GROUPS * EXPERTS_PER_GROUP), D_MODEL ** -0.5),
        "router_expert_b": nrm(ks[21], (L, N_GROUPS * EXPERTS_PER_GROUP), 0.01),
        "expert_w_gate": nrm(ks[22], (L, N_GROUPS, EXPERTS_PER_GROUP, D_MODEL, EXPERT_FF), D_MODEL ** -0.5),
        "expert_w_up": nrm(ks[23], (L, N_GROUPS, EXPERTS_PER_GROUP, D_MODEL, EXPERT_FF), D_MODEL ** -0.5),
        "expert_w_down": nrm(ks[24], (L, N_GROUPS, EXPERTS_PER_GROUP, EXPERT_FF, D_MODEL), EXPERT_FF ** -0.5),
    }


def reference(x, mem, mix_norm, w_in, pool_w, pool_scale, q_norm, k_norm, idx_k_norm, w_out,
              xattn_norm, mem_norm, xattn_wq, xattn_wkv, xattn_q_norm, xattn_k_norm, xattn_wo,
              ffn_norm, router_group_w, router_group_b, router_expert_w, router_expert_b,
              expert_w_gate, expert_w_up, expert_w_down):
    B, S, _ = x.shape
    cos_a, sin_a = rope_tables(S, HEAD_DIM)
    cos_i, sin_i = rope_tables(S, IDX_DIM)
    for l in range(DEPTH):
        h = rmsnorm(x, mix_norm[l])
        z = h @ w_in[l]
        u = z[..., OFF_POOL:OFF_Q]
        q = z[..., OFF_Q:OFF_K].reshape(B, S, ATTN_HEADS, HEAD_DIM)
        k = z[..., OFF_K:OFF_V].reshape(B, S, ATTN_HEADS, HEAD_DIM)
        v = z[..., OFF_V:OFF_QI].reshape(B, S, ATTN_HEADS, HEAD_DIM)
        qi = z[..., OFF_QI:OFF_KI].reshape(B, S, IDX_HEADS, IDX_DIM)
        ki = z[..., OFF_KI:OFF_WI]
        wi = z[..., OFF_WI:IN_COLS]
        q = apply_rope(rmsnorm(q, q_norm[l]), cos_a, sin_a)
        k = apply_rope(rmsnorm(k, k_norm[l]), cos_a, sin_a)
        qi = apply_rope(qi, cos_i, sin_i)
        ki = apply_rope(rmsnorm(ki, idx_k_norm[l])[:, :, None, :], cos_i, sin_i)[:, :, 0]
        y_pool = pool_mixer(u, pool_w[l], pool_scale[l])
        y_attn = dsa_attention(q, k, v, qi, ki, wi)
        x = x + jnp.concatenate([y_pool, y_attn], axis=-1) @ w_out[l]
        h = rmsnorm(x, xattn_norm[l])
        x = x + memory_cross_attention(h, mem, mem_norm[l], xattn_wq[l], xattn_wkv[l],
                                       xattn_q_norm[l], xattn_k_norm[l], xattn_wo[l])
        h = rmsnorm(x, ffn_norm[l])
        x = x + hierarchical_moe(h, router_group_w[l], router_group_b[l], router_expert_w[l],
                                 router_expert_b[l], expert_w_gate[l], expert_w_up[l],
                                 expert_w_down[l])
    return x
```

```python
import functools
import math

import jax
import jax.numpy as jnp
from jax import lax
from jax.experimental import pallas as pl
from jax.experimental.pallas import tpu as pltpu

CHUNK = 64
POOL_WINDOWS = (2, 4, 8, 16)
POOL_GROUP = 128
POOL_WIDTH = POOL_GROUP * len(POOL_WINDOWS)
ATTN_HEADS = 8
HEAD_DIM = 64
ATTN_WIDTH = ATTN_HEADS * HEAD_DIM
IDX_HEADS = 8
IDX_DIM = 64
TOPK_MAX = 256
ROPE_THETA = 10000.0
MEM_HEADS = 4
MEM_HEAD_DIM = 128
MEM_WIDTH = MEM_HEADS * MEM_HEAD_DIM
N_GROUPS = 4
EXPERTS_PER_GROUP = 8
N_EXPERTS = N_GROUPS * EXPERTS_PER_GROUP
EPS = 1e-6
OFF_Q = POOL_WIDTH
OFF_K = OFF_Q + ATTN_WIDTH
OFF_V = OFF_K + ATTN_WIDTH
OFF_QI = OFF_V + ATTN_WIDTH
OFF_KI = OFF_QI + IDX_HEADS * IDX_DIM
OFF_WI = OFF_KI + IDX_DIM
IN_COLS = OFF_WI + IDX_HEADS

LANES = 128
SUBLANES = 8
VMEM_LIMIT_BYTES = 56 * 1024 * 1024

HALO = 16
INT_MIN = -(2 ** 31)
NEG_BIG = -1e30

F32 = jnp.float32
BF16 = jnp.bfloat16
I32 = jnp.int32


def _dot(a, b):
    return jnp.dot(a, b, preferred_element_type=F32)


def _dot_nt(a, b):
    return lax.dot_general(a, b, (((1,), (1,)), ((), ())), preferred_element_type=F32)


def _rms_rows(x, g):
    ms = jnp.mean(x * x, axis=-1, keepdims=True)
    return x * lax.rsqrt(ms + EPS) * g


ROW_Q = 0
ROW_K = ROW_Q + ATTN_WIDTH
ROW_V = ROW_K + ATTN_WIDTH
ROW_QI = ROW_V + ATTN_WIDTH
ROW_KI = ROW_QI + IDX_HEADS * IDX_DIM
ROW_WI = ROW_KI + LANES
ROWS_T = ROW_WI + 2 * SUBLANES


def _norm_rope_t(z, gain, cos, sin_signed):
    if gain is not None:
        ms = jnp.mean(z * z, axis=0, keepdims=True)
        z = z * lax.rsqrt(ms + EPS) * gain
    half = z.shape[0] // 2
    swapped = jnp.concatenate([z[half:], z[:half]], axis=0)
    return z * cos + swapped * sin_signed


def _proj_kernel(x_ref, g_ref, wu_ref, wt_ref, poolw_ref, pscale_ref, qg_ref, kg_ref, ig_ref,
                 cos_ref, sin_ref,
                 ypool_ref, qt_ref, k_ref, vt_ref, qit_ref, ki_ref, wit_ref,
                 halo_ref, ext_ref, kt_ref, *, idx_scale):
    tm = x_ref.shape[0]
    h = _rms_rows(x_ref[...], g_ref[...]).astype(BF16)
    cos = cos_ref[...]
    sin = sin_ref[...]

    u = _dot(h, wu_ref[...])
    @pl.when(pl.program_id(1) == 0)
    def _():
        halo_ref[...] = jnp.zeros_like(halo_ref)

    ext_ref[0:HALO, :] = halo_ref[...]
    ext_ref[HALO:HALO + tm, :] = u
    halo_ref[...] = u[tm - HALO:, :]
    pos1 = pl.program_id(1) * tm + lax.broadcasted_iota(I32, (tm, 1), 0) + 1
    for g, w in enumerate(POOL_WINDOWS):
        cols = slice(g * POOL_GROUP, (g + 1) * POOL_GROUP)
        win = u[:, cols]
        for j in range(1, w):
            win = win + ext_ref[HALO - j:HALO - j + tm, cols]
        cnt = jnp.minimum(pos1, w).astype(F32)
        mixed = (win / cnt - u[:, cols]).astype(BF16)
        y = _dot(mixed, poolw_ref[g]) * pscale_ref[:, cols]
        ypool_ref[:, cols] = y.astype(ypool_ref.dtype)

    qg = qg_ref[...]
    kg = kg_ref[...]
    zq = _dot_nt(wt_ref[ROW_Q:ROW_K, :], h)
    for hd in range(ATTN_HEADS):
        rows = slice(hd * HEAD_DIM, (hd + 1) * HEAD_DIM)
        qt_ref[rows, :] = (_norm_rope_t(zq[rows], qg, cos, sin) * (HEAD_DIM ** -0.5)).astype(qt_ref.dtype)
    zk = _dot_nt(wt_ref[ROW_K:ROW_V, :], h)
    for hd in range(ATTN_HEADS):
        rows = slice(hd * HEAD_DIM, (hd + 1) * HEAD_DIM)
        kt_ref[rows, :] = _norm_rope_t(zk[rows], kg, cos, sin)
    k_ref[...] = kt_ref[...].T.astype(k_ref.dtype)
    vt_ref[...] = _dot_nt(wt_ref[ROW_V:ROW_QI, :], h).astype(vt_ref.dtype)
    zqi = _dot_nt(wt_ref[ROW_QI:ROW_KI, :], h)
    for hd in range(IDX_HEADS):
        rows = slice(hd * IDX_DIM, (hd + 1) * IDX_DIM)
        qit_ref[rows, :] = _norm_rope_t(zqi[rows], None, cos, sin).astype(qit_ref.dtype)
    zi = _dot_nt(wt_ref[ROW_KI:ROWS_T, :], h)
    kit = _norm_rope_t(zi[0:IDX_DIM], ig_ref[...], cos, sin)
    kit = jnp.concatenate([kit, jnp.zeros((LANES - IDX_DIM, tm), F32)], axis=0)
    ki_ref[...] = kit.T.astype(ki_ref.dtype)
    wit_ref[...] = zi[LANES:LANES + IDX_HEADS] * idx_scale


def _proj_call(x, g, wu, wt, poolw, pscale, qg, kg, ig, cos_t, sin_t, *, tm):
    B, S, D = x.shape
    nt = S // tm
    kernel = functools.partial(_proj_kernel, idx_scale=(IDX_DIM ** -0.5) * (IDX_HEADS ** -0.5))
    full = lambda shape: pl.BlockSpec(shape, lambda b, i: (0,) * len(shape))
    out_shape = (
        jax.ShapeDtypeStruct((B, S, POOL_WIDTH), BF16),
        jax.ShapeDtypeStruct((B, ATTN_WIDTH, S), BF16),
        jax.ShapeDtypeStruct((B, S, ATTN_WIDTH), BF16),
        jax.ShapeDtypeStruct((B, nt, ATTN_WIDTH, tm), BF16),
        jax.ShapeDtypeStruct((B, IDX_HEADS * IDX_DIM, S), BF16),
        jax.ShapeDtypeStruct((B, S, LANES), BF16),
        jax.ShapeDtypeStruct((B, IDX_HEADS, S), F32),
    )
    return pl.pallas_call(
        kernel,
        grid=(B, nt),
        in_specs=[
            pl.BlockSpec((None, tm, D), lambda b, i: (b, i, 0)),
            full(g.shape), full(wu.shape), full(wt.shape), full(poolw.shape), full(pscale.shape),
            full(qg.shape), full(kg.shape), full(ig.shape),
            pl.BlockSpec((HEAD_DIM, tm), lambda b, i: (0, i)),
            pl.BlockSpec((HEAD_DIM, tm), lambda b, i: (0, i)),
        ],
        out_specs=(
            pl.BlockSpec((None, tm, POOL_WIDTH), lambda b, i: (b, i, 0)),
            pl.BlockSpec((None, ATTN_WIDTH, tm), lambda b, i: (b, 0, i)),
            pl.BlockSpec((None, tm, ATTN_WIDTH), lambda b, i: (b, i, 0)),
            pl.BlockSpec((None, None, ATTN_WIDTH, tm), lambda b, i: (b, i, 0, 0)),
            pl.BlockSpec((None, IDX_HEADS * IDX_DIM, tm), lambda b, i: (b, 0, i)),
            pl.BlockSpec((None, tm, LANES), lambda b, i: (b, i, 0)),
            pl.BlockSpec((None, IDX_HEADS, tm), lambda b, i: (b, 0, i)),
        ),
        out_shape=out_shape,
        scratch_shapes=[
            pltpu.VMEM((HALO, POOL_WIDTH), F32),
            pltpu.VMEM((HALO + tm, POOL_WIDTH), F32),
            pltpu.VMEM((ATTN_WIDTH, tm), F32),
        ],
        compiler_params=pltpu.CompilerParams(
            dimension_semantics=("arbitrary", "arbitrary"), vmem_limit_bytes=VMEM_LIMIT_BYTES),
    )(x, g, wu, wt, poolw, pscale, qg, kg, ig, cos_t, sin_t)


def _sortable_key(score):
    bits = lax.bitcast_convert_type(score, I32)
    key = jnp.where(bits < 0, bits ^ jnp.int32(0x7FFFFFFF), bits)
    return jnp.where(key == -1, 0, key)


def _colsum8(x):
    rows, t = x.shape
    return jnp.sum(x.reshape(rows // SUBLANES, SUBLANES, t), axis=0)


def _dsa_kernel(qit_ref, wit_ref, qt_ref, ki_ref, k_ref, vt_ref, out_ref, sc_ref, ot_ref, *, topk):
    tq = qit_ref.shape[1]
    nkt_max, tk, _ = sc_ref.shape
    q0 = pl.program_id(1) * tq
    nkt = (q0 + tq + tk - 1) // tk
    qpos = q0 + lax.broadcasted_iota(I32, (1, tq), 1)
    qend = (qpos // CHUNK + 1) * CHUNK
    krow = lax.broadcasted_iota(I32, (tk, 1), 0)
    zeros_half = jnp.zeros((LANES - IDX_DIM, tq), BF16)

    def score_tile(kt, carry):
        ki_t = ki_ref[pl.ds(pl.multiple_of(kt * tk, tk), tk), :]
        acc = jnp.zeros((tk, tq), F32)
        for hd in range(IDX_HEADS):
            qh = jnp.concatenate([qit_ref[hd * IDX_DIM:(hd + 1) * IDX_DIM, :], zeros_half], axis=0)
            rel = jnp.maximum(_dot(ki_t, qh), 0.0)
            acc = acc + rel * wit_ref[hd:hd + 1, :]
        key = _sortable_key(acc)
        sc_ref[kt] = jnp.where(kt * tk + krow < qend, key, INT_MIN)
        return carry

    lax.fori_loop(0, nkt, score_tile, 0)

    def count(pred_fn):
        def body(kt, c8):
            return c8 + _colsum8(jnp.where(pred_fn(sc_ref[kt], kt), 1, 0).astype(I32))
        c8 = lax.fori_loop(0, nkt, body, jnp.zeros((SUBLANES, tq), I32))
        return jnp.sum(c8, axis=0, keepdims=True)

    def bit_step(step, thr):
        cand = thr + lax.shift_left(jnp.int32(1), 31 - step)
        c = count(lambda key, kt: key >= cand)
        return jnp.where(c >= topk, cand, thr)

    thr = lax.fori_loop(0, 32, bit_step, jnp.full((1, tq), INT_MIN, I32))

    n_above = count(lambda key, kt: key > thr)
    n_ties = count(lambda key, kt: key == thr)
    need = topk - n_above
    nbits = max(1, (nkt_max * tk - 1).bit_length())

    def tie_step(step, cut):
        cand = cut + lax.shift_left(jnp.int32(1), nbits - 1 - step)
        c = count(lambda key, kt: (key == thr) & (kt * tk + krow < cand))
        return jnp.where(c < need, cand, cut)

    def resolve_ties():
        return lax.fori_loop(0, nbits, tie_step, jnp.zeros((1, tq), I32))

    cut = lax.cond(jnp.max(n_ties - need) > 0, resolve_ties,
                   lambda: jnp.full((1, tq), nkt_max * tk, I32))

    def mask_tile(kt, carry):
        key = sc_ref[kt]
        sel = (key > thr) | ((key == thr) & (kt * tk + krow <= cut))
        sel = sel & (key > INT_MIN)
        sc_ref[kt] = lax.bitcast_convert_type(jnp.where(sel, 0.0, NEG_BIG).astype(F32), I32)
        return carry

    lax.fori_loop(0, nkt, mask_tile, 0)

    zeros_head = jnp.zeros((HEAD_DIM, tq), BF16)
    for hd in range(ATTN_HEADS):
        pair = hd // 2
        qh = qt_ref[hd * HEAD_DIM:(hd + 1) * HEAD_DIM, :]
        qh = jnp.concatenate([qh, zeros_head] if hd % 2 == 0 else [zeros_head, qh], axis=0)

        def attn_tile(kt, carry, qh=qh, pair=pair, hd=hd):
            m, l, acc = carry
            k2 = k_ref[pl.ds(pl.multiple_of(kt * tk, tk), tk), pair * LANES:(pair + 1) * LANES]
            s = _dot(k2, qh) + lax.bitcast_convert_type(sc_ref[kt], F32)
            m_new = jnp.maximum(m, jnp.max(s, axis=0, keepdims=True))
            alpha = jnp.exp(m - m_new)
            p = jnp.exp(s - m_new)
            l = alpha * l + jnp.sum(p, axis=0, keepdims=True)
            vt = vt_ref[kt, hd * HEAD_DIM:(hd + 1) * HEAD_DIM, :]
            acc = alpha * acc + _dot(vt, p.astype(BF16))
            return m_new, l, acc

        init = (jnp.full((1, tq), NEG_BIG, F32), jnp.zeros((1, tq), F32), jnp.zeros((HEAD_DIM, tq), F32))
        m, l, acc = lax.fori_loop(0, nkt, attn_tile, init)
        ot_ref[hd * HEAD_DIM:(hd + 1) * HEAD_DIM, :] = acc / l
    out_ref[...] = ot_ref[...].T.astype(out_ref.dtype)


def _dsa_call(qit, wit, qt, ki, k, vt, *, tq, topk):
    B, S, _ = k.shape
    _, nkt, _, tk = vt.shape
    kernel = functools.partial(_dsa_kernel, topk=topk)
    return pl.pallas_call(
        kernel,
        grid=(B, S // tq),
        in_specs=[
            pl.BlockSpec((None, IDX_HEADS * IDX_DIM, tq), lambda b, i: (b, 0, i)),
            pl.BlockSpec((None, IDX_HEADS, tq), lambda b, i: (b, 0, i)),
            pl.BlockSpec((None, ATTN_WIDTH, tq), lambda b, i: (b, 0, i)),
            pl.BlockSpec((None, S, LANES), lambda b, i: (b, 0, 0)),
            pl.BlockSpec((None, S, ATTN_WIDTH), lambda b, i: (b, 0, 0)),
            pl.BlockSpec((None, nkt, ATTN_WIDTH, tk), lambda b, i: (b, 0, 0, 0)),
        ],
        out_specs=pl.BlockSpec((None, tq, ATTN_WIDTH), lambda b, i: (b, i, 0)),
        out_shape=jax.ShapeDtypeStruct((B, S, ATTN_WIDTH), BF16),
        scratch_shapes=[
            pltpu.VMEM((nkt, tk, tq), I32),
            pltpu.VMEM((ATTN_WIDTH, tq), F32),
        ],
        compiler_params=pltpu.CompilerParams(
            dimension_semantics=("arbitrary", "arbitrary"), vmem_limit_bytes=VMEM_LIMIT_BYTES),
    )(qit, wit, qt, ki, k, vt)


def _memkv_kernel(mem_ref, g_ref, wkv_ref, kn_ref, k_ref, v_ref):
    m = _rms_rows(mem_ref[...], g_ref[...]).astype(BF16)
    kv = _dot(m, wkv_ref[...])
    for hd in range(MEM_HEADS):
        cols = slice(hd * MEM_HEAD_DIM, (hd + 1) * MEM_HEAD_DIM)
        k_ref[:, cols] = _rms_rows(kv[:, cols], kn_ref[...]).astype(k_ref.dtype)
    v_ref[...] = kv[:, MEM_WIDTH:].astype(v_ref.dtype)


def _memkv_call(mem, g, wkv, kn):
    B, M, D = mem.shape
    full = lambda shape: pl.BlockSpec(shape, lambda b: (0,) * len(shape))
    return pl.pallas_call(
        _memkv_kernel,
        grid=(B,),
        in_specs=[pl.BlockSpec((None, M, D), lambda b: (b, 0, 0)), full(g.shape), full(wkv.shape), full(kn.shape)],
        out_specs=(pl.BlockSpec((None, M, MEM_WIDTH), lambda b: (b, 0, 0)),
                   pl.BlockSpec((None, M, MEM_WIDTH), lambda b: (b, 0, 0))),
        out_shape=(jax.ShapeDtypeStruct((B, M, MEM_WIDTH), BF16), jax.ShapeDtypeStruct((B, M, MEM_WIDTH), BF16)),
        compiler_params=pltpu.CompilerParams(dimension_semantics=("arbitrary",), vmem_limit_bytes=VMEM_LIMIT_BYTES),
    )(mem, g, wkv, kn)


def _split_bf16(a):
    hi = a.astype(BF16)
    lo = (a - hi.astype(F32)).astype(BF16)
    return hi, lo


def _mid_kernel(x_ref, yp_ref, ya_ref, wo1_ref, wo2_ref, gx_ref, wq_ref, qn_ref, km_ref, vm_ref, wo_ref,
                gf_ref, wr_ref, br_ref, x2_ref, h3_ref, gates_ref):
    tm = x_ref.shape[0]
    x1 = x_ref[...] + _dot(yp_ref[...], wo1_ref[...]) + _dot(ya_ref[...], wo2_ref[...])

    h = _rms_rows(x1, gx_ref[...]).astype(BF16)
    q = _dot(h, wq_ref[...])
    heads = []
    for hd in range(MEM_HEADS):
        cols = slice(hd * MEM_HEAD_DIM, (hd + 1) * MEM_HEAD_DIM)
        qh = (_rms_rows(q[:, cols], qn_ref[...]) * (MEM_HEAD_DIM ** -0.5)).astype(BF16)
        s = _dot_nt(qh, km_ref[:, cols])
        p = jnp.exp(s - jnp.max(s, axis=-1, keepdims=True))
        p = p / jnp.sum(p, axis=-1, keepdims=True)
        heads.append(_dot(p.astype(BF16), vm_ref[:, cols]))
    o = jnp.concatenate(heads, axis=-1).astype(BF16)
    x2 = x1 + _dot(o, wo_ref[...])
    x2_ref[...] = x2

    h3 = _rms_rows(x2, gf_ref[...])
    h3_ref[...] = h3.astype(h3_ref.dtype)
    h_hi, h_lo = _split_bf16(h3)
    w_hi, w_lo = _split_bf16(wr_ref[...])
    logits = _dot(h_hi, w_hi) + (_dot(h_hi, w_lo) + _dot(h_lo, w_hi)) + br_ref[...]
    lane = lax.broadcasted_iota(I32, (tm, LANES), 1).astype(F32)
    neg_inf = -jnp.inf
    g_logit = jnp.where(lane < N_GROUPS, logits, neg_inf)
    g_max = jnp.max(g_logit, axis=-1, keepdims=True)
    g_sel = jnp.min(jnp.where(g_logit == g_max, lane, LANES), axis=-1, keepdims=True)
    g_w = 1.0 / jnp.sum(jnp.exp(g_logit - g_max), axis=-1, keepdims=True)
    e_lo = N_GROUPS + g_sel * EXPERTS_PER_GROUP
    in_group = (lane >= e_lo) & (lane < e_lo + EXPERTS_PER_GROUP)
    e_logit = jnp.where(in_group, logits, neg_inf)
    v1 = jnp.max(e_logit, axis=-1, keepdims=True)
    i1 = jnp.min(jnp.where(e_logit == v1, lane, LANES), axis=-1, keepdims=True)
    rest = jnp.where(lane == i1, neg_inf, e_logit)
    v2 = jnp.max(rest, axis=-1, keepdims=True)
    i2 = jnp.min(jnp.where(rest == v2, lane, LANES), axis=-1, keepdims=True)
    e2 = jnp.exp(v2 - v1)
    w1 = g_w / (1.0 + e2)
    w2 = g_w * e2 / (1.0 + e2)
    gates = jnp.where(lane == i1 - N_GROUPS, w1, 0.0) + jnp.where(lane == i2 - N_GROUPS, w2, 0.0)
    gates_ref[...] = gates


def _mid_call(x, ypool, yattn, wo1, wo2, gx, wq, qn, kmem, vmem, wo, gf, wr, br, *, tm):
    B, S, D = x.shape
    M = kmem.shape[1]
    full = lambda shape: pl.BlockSpec(shape, lambda b, i: (0,) * len(shape))
    tile = lambda width: pl.BlockSpec((None, tm, width), lambda b, i: (b, i, 0))
    return pl.pallas_call(
        _mid_kernel,
        grid=(B, S // tm),
        in_specs=[tile(D), tile(POOL_WIDTH), tile(ATTN_WIDTH), full(wo1.shape), full(wo2.shape), full(gx.shape),
                  full(wq.shape), full(qn.shape),
                  pl.BlockSpec((None, M, MEM_WIDTH), lambda b, i: (b, 0, 0)),
                  pl.BlockSpec((None, M, MEM_WIDTH), lambda b, i: (b, 0, 0)),
                  full(wo.shape), full(gf.shape), full(wr.shape), full(br.shape)],
        out_specs=(tile(D), tile(D), tile(LANES)),
        out_shape=(jax.ShapeDtypeStruct((B, S, D), F32), jax.ShapeDtypeStruct((B, S, D), BF16),
                   jax.ShapeDtypeStruct((B, S, LANES), F32)),
        compiler_params=pltpu.CompilerParams(
            dimension_semantics=("arbitrary", "arbitrary"), vmem_limit_bytes=VMEM_LIMIT_BYTES),
    )(x, ypool, yattn, wo1, wo2, gx, wq, qn, kmem, vmem, wo, gf, wr, br)


def _moe_kernel(x2_ref, h3_ref, gates_ref, wg_ref, wu_ref, wd_ref, out_ref):
    e = pl.program_id(1)

    @pl.when(e == 0)
    def _():
        out_ref[...] = x2_ref[...]

    h = h3_ref[...]
    a = _dot(h, wg_ref[...])
    b = _dot(h, wu_ref[...])
    lane = lax.broadcasted_iota(I32, gates_ref.shape, 1)
    gate = jnp.sum(jnp.where(lane == e, gates_ref[...], 0.0), axis=-1, keepdims=True)
    hid = (a * jax.nn.sigmoid(a) * b * gate).astype(BF16)
    out_ref[...] += _dot(hid, wd_ref[...])


def _moe_call(x2, h3, gates, wg, wu, wd, *, tm):
    N, D = x2.shape
    E, _, F = wg.shape
    return pl.pallas_call(
        _moe_kernel,
        grid=(N // tm, E),
        in_specs=[pl.BlockSpec((tm, D), lambda i, e: (i, 0)),
                  pl.BlockSpec((tm, D), lambda i, e: (i, 0)),
                  pl.BlockSpec((tm, LANES), lambda i, e: (i, 0)),
                  pl.BlockSpec((None, D, F), lambda i, e: (e, 0, 0)),
                  pl.BlockSpec((None, D, F), lambda i, e: (e, 0, 0)),
                  pl.BlockSpec((None, F, D), lambda i, e: (e, 0, 0))],
        out_specs=pl.BlockSpec((tm, D), lambda i, e: (i, 0)),
        out_shape=jax.ShapeDtypeStruct((N, D), F32),
        compiler_params=pltpu.CompilerParams(
            dimension_semantics=("arbitrary", "arbitrary"), vmem_limit_bytes=VMEM_LIMIT_BYTES),
    )(x2, h3, gates, wg, wu, wd)


def _rope_tables_t(seq_len, dim):
    inv = ROPE_THETA ** (-jnp.arange(0, dim, 2, dtype=F32) / dim)
    ang = jnp.arange(seq_len, dtype=F32)[:, None] * inv[None, :]
    ang = jnp.concatenate([ang, ang], axis=-1)
    sign = jnp.concatenate([-jnp.ones((dim // 2,), F32), jnp.ones((dim // 2,), F32)])
    return jnp.cos(ang).T, (jnp.sin(ang) * sign[None, :]).T


def _layer(x, mem, mix_norm, w_in, pool_w, pool_scale, q_norm, k_norm, idx_k_norm, w_out,
           xattn_norm, mem_norm, xattn_wq, xattn_wkv, xattn_q_norm, xattn_k_norm, xattn_wo,
           ffn_norm, router_group_w, router_group_b, router_expert_w, router_expert_b,
           expert_w_gate, expert_w_up, expert_w_down, *, tm, tq):
    B, S, D = x.shape
    topk = min(TOPK_MAX, S // 4)
    row = lambda v: v.reshape(1, -1).astype(F32)
    col = lambda v: jnp.broadcast_to(v.astype(F32)[:, None], (v.shape[0], tm))

    wu = w_in[:, :POOL_WIDTH].astype(BF16)
    pad = lambda n: jnp.zeros((n, D), F32)
    wt = jnp.concatenate([
        w_in[:, OFF_Q:OFF_KI].T, w_in[:, OFF_KI:OFF_WI].T, pad(LANES - IDX_DIM),
        w_in[:, OFF_WI:IN_COLS].T, pad(2 * SUBLANES - IDX_HEADS)], axis=0).astype(BF16)
    cos_t, sin_t = _rope_tables_t(S, HEAD_DIM)

    ypool, qt, k, vt, qit, ki, wit = _proj_call(
        x, row(mix_norm), wu, wt, pool_w.astype(BF16), row(pool_scale), col(q_norm), col(k_norm),
        col(idx_k_norm), cos_t, sin_t, tm=tm)
    yattn = _dsa_call(qit, wit, qt, ki, k, vt, tq=tq, topk=topk)
    kmem, vmem = _memkv_call(mem, row(mem_norm), xattn_wkv.astype(BF16), row(xattn_k_norm))

    n_logits = N_GROUPS + N_EXPERTS
    wr = jnp.concatenate([router_group_w, router_expert_w, jnp.zeros((D, LANES - n_logits), F32)], axis=1)
    br = jnp.concatenate([router_group_b, router_expert_b, jnp.zeros((LANES - n_logits,), F32)]).reshape(1, LANES)
    x2, h3, gates = _mid_call(
        x, ypool, yattn, w_out[:POOL_WIDTH].astype(BF16), w_out[POOL_WIDTH:].astype(BF16), row(xattn_norm),
        xattn_wq.astype(BF16), row(xattn_q_norm), kmem, vmem, xattn_wo.astype(BF16), row(ffn_norm), wr, br,
        tm=tm)

    F = expert_w_gate.shape[-1]
    out = _moe_call(
        x2.reshape(B * S, D), h3.reshape(B * S, D), gates.reshape(B * S, LANES),
        expert_w_gate.reshape(N_EXPERTS, D, F).astype(BF16), expert_w_up.reshape(N_EXPERTS, D, F).astype(BF16),
        expert_w_down.reshape(N_EXPERTS, F, D).astype(BF16), tm=tm)
    return out.reshape(B, S, D)


def kernel(x, mem, mix_norm, w_in, pool_w, pool_scale, q_norm, k_norm, idx_k_norm, w_out, xattn_norm, mem_norm,
           xattn_wq, xattn_wkv, xattn_q_norm, xattn_k_norm, xattn_wo, ffn_norm, router_group_w, router_group_b,
           router_expert_w, router_expert_b, expert_w_gate, expert_w_up, expert_w_down):
    depth = mix_norm.shape[0]
    for l in range(depth):
        x = _layer(
            x, mem, mix_norm[l], w_in[l], pool_w[l], pool_scale[l], q_norm[l], k_norm[l], idx_k_norm[l], w_out[l],
            xattn_norm[l], mem_norm[l], xattn_wq[l], xattn_wkv[l], xattn_q_norm[l], xattn_k_norm[l], xattn_wo[l],
            ffn_norm[l], router_group_w[l], router_group_b[l], router_expert_w[l], router_expert_b[l],
            expert_w_gate[l], expert_w_up[l], expert_w_down[l], tm=512, tq=256)
    return x
```

```python
import functools
import math

import jax
import jax.numpy as jnp
from jax import lax
from jax.experimental import pallas as pl
from jax.experimental.pallas import tpu as pltpu

CHUNK = 64
POOL_WINDOWS = (2, 4, 8, 16)
POOL_GROUP = 128
POOL_WIDTH = POOL_GROUP * len(POOL_WINDOWS)
ATTN_HEADS = 8
HEAD_DIM = 64
ATTN_WIDTH = ATTN_HEADS * HEAD_DIM
IDX_HEADS = 8
IDX_DIM = 64
TOPK_MAX = 256
ROPE_THETA = 10000.0
MEM_HEADS = 4
MEM_HEAD_DIM = 128
MEM_WIDTH = MEM_HEADS * MEM_HEAD_DIM
N_GROUPS = 4
EXPERTS_PER_GROUP = 8
N_EXPERTS = N_GROUPS * EXPERTS_PER_GROUP
EPS = 1e-6
OFF_Q = POOL_WIDTH
OFF_K = OFF_Q + ATTN_WIDTH
OFF_V = OFF_K + ATTN_WIDTH
OFF_QI = OFF_V + ATTN_WIDTH
OFF_KI = OFF_QI + IDX_HEADS * IDX_DIM
OFF_WI = OFF_KI + IDX_DIM
IN_COLS = OFF_WI + IDX_HEADS

LANES = 128
SUBLANES = 8
VMEM_LIMIT_BYTES = 56 * 1024 * 1024

HALO = 16
INT_MIN = -(2 ** 31)
NEG_BIG = -1e30

F32 = jnp.float32
BF16 = jnp.bfloat16
I32 = jnp.int32


def _dot(a, b):
    return jnp.dot(a, b, preferred_element_type=F32)


def _dot_nt(a, b):
    return lax.dot_general(a, b, (((1,), (1,)), ((), ())), preferred_element_type=F32)


def _rms_rows(x, g):
    ms = jnp.mean(x * x, axis=-1, keepdims=True)
    return x * lax.rsqrt(ms + EPS) * g


ROW_Q = 0
ROW_K = ROW_Q + ATTN_WIDTH
ROW_V = ROW_K + ATTN_WIDTH
ROW_QI = ROW_V + ATTN_WIDTH
ROW_KI = ROW_QI + IDX_HEADS * IDX_DIM
ROW_WI = ROW_KI + LANES
ROWS_T = ROW_WI + 2 * SUBLANES


def _norm_rope_t(z, gain, cos, sin_signed):
    if gain is not None:
        ms = jnp.mean(z * z, axis=0, keepdims=True)
        z = z * lax.rsqrt(ms + EPS) * gain
    half = z.shape[0] // 2
    swapped = jnp.concatenate([z[half:], z[:half]], axis=0)
    return z * cos + swapped * sin_signed


def _proj_kernel(x_ref, g_ref, wu_ref, wt_ref, poolw_ref, pscale_ref, qg_ref, kg_ref, ig_ref,
                 cos_ref, sin_ref,
                 ypool_ref, qt_ref, k_ref, vt_ref, qit_ref, ki_ref, wit_ref,
                 halo_ref, ext_ref, kt_ref, *, idx_scale):
    tm = x_ref.shape[0]
    h = _rms_rows(x_ref[...], g_ref[...]).astype(BF16)
    cos = cos_ref[...]
    sin = sin_ref[...]

    u = _dot(h, wu_ref[...])
    @pl.when(pl.program_id(1) == 0)
    def _():
        halo_ref[...] = jnp.zeros_like(halo_ref)

    ext_ref[0:HALO, :] = halo_ref[...]
    ext_ref[HALO:HALO + tm, :] = u
    halo_ref[...] = u[tm - HALO:, :]
    pos1 = pl.program_id(1) * tm + lax.broadcasted_iota(I32, (tm, 1), 0) + 1
    for g, w in enumerate(POOL_WINDOWS):
        cols = slice(g * POOL_GROUP, (g + 1) * POOL_GROUP)
        win = u[:, cols]
        for j in range(1, w):
            win = win + ext_ref[HALO - j:HALO - j + tm, cols]
        cnt = jnp.minimum(pos1, w).astype(F32)
        mixed = (win / cnt - u[:, cols]).astype(BF16)
        y = _dot(mixed, poolw_ref[g]) * pscale_ref[:, cols]
        ypool_ref[:, cols] = y.astype(ypool_ref.dtype)

    qg = qg_ref[...]
    kg = kg_ref[...]
    zq = _dot_nt(wt_ref[ROW_Q:ROW_K, :], h)
    for hd in range(ATTN_HEADS):
        rows = slice(hd * HEAD_DIM, (hd + 1) * HEAD_DIM)
        qt_ref[rows, :] = (_norm_rope_t(zq[rows], qg, cos, sin) * (HEAD_DIM ** -0.5)).astype(qt_ref.dtype)
    zk = _dot_nt(wt_ref[ROW_K:ROW_V, :], h)
    for hd in range(ATTN_HEADS):
        rows = slice(hd * HEAD_DIM, (hd + 1) * HEAD_DIM)
        kt_ref[rows, :] = _norm_rope_t(zk[rows], kg, cos, sin)
    k_ref[...] = kt_ref[...].T.astype(k_ref.dtype)
    vt_ref[...] = _dot_nt(wt_ref[ROW_V:ROW_QI, :], h).astype(vt_ref.dtype)
    zqi = _dot_nt(wt_ref[ROW_QI:ROW_KI, :], h)
    for hd in range(IDX_HEADS):
        rows = slice(hd * IDX_DIM, (hd + 1) * IDX_DIM)
        qit_ref[rows, :] = _norm_rope_t(zqi[rows], None, cos, sin).astype(qit_ref.dtype)
    zi = _dot_nt(wt_ref[ROW_KI:ROWS_T, :], h)
    kit = _norm_rope_t(zi[0:IDX_DIM], ig_ref[...], cos, sin)
    kit = jnp.concatenate([kit, jnp.zeros((LANES - IDX_DIM, tm), F32)], axis=0)
    ki_ref[...] = kit.T.astype(ki_ref.dtype)
    wit_ref[...] = zi[LANES:LANES + IDX_HEADS] * idx_scale


def _proj_call(x, g, wu, wt, poolw, pscale, qg, kg, ig, cos_t, sin_t, *, tm):
    B, S, D = x.shape
    nt = S // tm
    kernel = functools.partial(_proj_kernel, idx_scale=(IDX_DIM ** -0.5) * (IDX_HEADS ** -0.5))
    full = lambda shape: pl.BlockSpec(shape, lambda b, i: (0,) * len(shape))
    out_shape = (
        jax.ShapeDtypeStruct((B, S, POOL_WIDTH), BF16),
        jax.ShapeDtypeStruct((B, ATTN_WIDTH, S), BF16),
        jax.ShapeDtypeStruct((B, S, ATTN_WIDTH), BF16),
        jax.ShapeDtypeStruct((B, nt, ATTN_WIDTH, tm), BF16),
        jax.ShapeDtypeStruct((B, IDX_HEADS * IDX_DIM, S), BF16),
        jax.ShapeDtypeStruct((B, S, LANES), BF16),
        jax.ShapeDtypeStruct((B, IDX_HEADS, S), F32),
    )
    return pl.pallas_call(
        kernel,
        grid=(B, nt),
        in_specs=[
            pl.BlockSpec((None, tm, D), lambda b, i: (b, i, 0)),
            full(g.shape), full(wu.shape), full(wt.shape), full(poolw.shape), full(pscale.shape),
            full(qg.shape), full(kg.shape), full(ig.shape),
            pl.BlockSpec((HEAD_DIM, tm), lambda b, i: (0, i)),
            pl.BlockSpec((HEAD_DIM, tm), lambda b, i: (0, i)),
        ],
        out_specs=(
            pl.BlockSpec((None, tm, POOL_WIDTH), lambda b, i: (b, i, 0)),
            pl.BlockSpec((None, ATTN_WIDTH, tm), lambda b, i: (b, 0, i)),
            pl.BlockSpec((None, tm, ATTN_WIDTH), lambda b, i: (b, i, 0)),
            pl.BlockSpec((None, None, ATTN_WIDTH, tm), lambda b, i: (b, i, 0, 0)),
            pl.BlockSpec((None, IDX_HEADS * IDX_DIM, tm), lambda b, i: (b, 0, i)),
            pl.BlockSpec((None, tm, LANES), lambda b, i: (b, i, 0)),
            pl.BlockSpec((None, IDX_HEADS, tm), lambda b, i: (b, 0, i)),
        ),
        out_shape=out_shape,
        scratch_shapes=[
            pltpu.VMEM((HALO, POOL_WIDTH), F32),
            pltpu.VMEM((HALO + tm, POOL_WIDTH), F32),
            pltpu.VMEM((ATTN_WIDTH, tm), F32),
        ],
        compiler_params=pltpu.CompilerParams(
            dimension_semantics=("arbitrary", "arbitrary"), vmem_limit_bytes=VMEM_LIMIT_BYTES),
    )(x, g, wu, wt, poolw, pscale, qg, kg, ig, cos_t, sin_t)


def _sortable_key(score):
    bits = lax.bitcast_convert_type(score, I32)
    key = jnp.where(bits < 0, bits ^ jnp.int32(0x7FFFFFFF), bits)
    return jnp.where(key == -1, 0, key)


def _colsum8(x):
    rows, t = x.shape
    return jnp.sum(x.reshape(rows // SUBLANES, SUBLANES, t), axis=0)


def _dsa_kernel(qit_ref, wit_ref, qt_ref, ki_ref, k_ref, vt_ref, out_ref, sc_ref, ot_ref, m_ref, l_ref, *, topk):
    tq = qit_ref.shape[1]
    nkt_max, tk, _ = sc_ref.shape
    q0 = pl.program_id(1) * tq
    nkt = (q0 + tq + tk - 1) // tk
    qpos = q0 + lax.broadcasted_iota(I32, (1, tq), 1)
    qend = (qpos // CHUNK + 1) * CHUNK
    krow = lax.broadcasted_iota(I32, (tk, 1), 0)
    zeros_half = jnp.zeros((LANES - IDX_DIM, tq), BF16)

    def score_tile(kt, carry):
        ki_t = ki_ref[pl.ds(pl.multiple_of(kt * tk, tk), tk), :]
        acc = jnp.zeros((tk, tq), F32)
        for hd in range(IDX_HEADS):
            qh = jnp.concatenate([qit_ref[hd * IDX_DIM:(hd + 1) * IDX_DIM, :], zeros_half], axis=0)
            rel = jnp.maximum(_dot(ki_t, qh), 0.0)
            acc = acc + rel * wit_ref[hd:hd + 1, :]
        key = _sortable_key(acc)
        sc_ref[kt] = jnp.where(kt * tk + krow < qend, key, INT_MIN)
        return carry

    lax.fori_loop(0, nkt, score_tile, 0)

    def count(pred_fn):
        def body(kt, c8):
            return c8 + _colsum8(jnp.where(pred_fn(sc_ref[kt], kt), 1, 0).astype(I32))
        c8 = lax.fori_loop(0, nkt, body, jnp.zeros((SUBLANES, tq), I32))
        return jnp.sum(c8, axis=0, keepdims=True)

    def bit_step(step, thr):
        cand = thr + lax.shift_left(jnp.int32(1), 31 - step)
        c = count(lambda key, kt: key >= cand)
        return jnp.where(c >= topk, cand, thr)

    thr = lax.fori_loop(0, 32, bit_step, jnp.full((1, tq), INT_MIN, I32))

    n_above = count(lambda key, kt: key > thr)
    n_ties = count(lambda key, kt: key == thr)
    need = topk - n_above
    nbits = max(1, (nkt_max * tk - 1).bit_length())

    def tie_step(step, cut):
        cand = cut + lax.shift_left(jnp.int32(1), nbits - 1 - step)
        c = count(lambda key, kt: (key == thr) & (kt * tk + krow < cand))
        return jnp.where(c < need, cand, cut)

    def resolve_ties():
        return lax.fori_loop(0, nbits, tie_step, jnp.zeros((1, tq), I32))

    cut = lax.cond(jnp.max(n_ties - need) > 0, resolve_ties,
                   lambda: jnp.full((1, tq), nkt_max * tk, I32))

    def mask_tile(kt, carry):
        key = sc_ref[kt]
        sel = (key > thr) | ((key == thr) & (kt * tk + krow <= cut))
        sel = sel & (key > INT_MIN)
        sc_ref[kt] = lax.bitcast_convert_type(jnp.where(sel, 0.0, NEG_BIG).astype(F32), I32)
        return carry

    lax.fori_loop(0, nkt, mask_tile, 0)

    zeros_head = jnp.zeros((HEAD_DIM, tq), BF16)
    m_ref[...] = jnp.full(m_ref.shape, NEG_BIG, F32)
    l_ref[...] = jnp.zeros(l_ref.shape, F32)
    ot_ref[...] = jnp.zeros(ot_ref.shape, F32)

    def attn_tile(kt, carry):
        krows = pl.ds(pl.multiple_of(kt * tk, tk), tk)
        for hd in range(ATTN_HEADS):
            pair = hd // 2
            rows = slice(hd * HEAD_DIM, (hd + 1) * HEAD_DIM)
            qh = qt_ref[rows, :]
            qh = jnp.concatenate([qh, zeros_head] if hd % 2 == 0 else [zeros_head, qh], axis=0)
            k2 = k_ref[krows, pair * LANES:(pair + 1) * LANES]
            s = _dot(k2, qh) + lax.bitcast_convert_type(sc_ref[kt], F32)
            m_old = m_ref[hd:hd + 1, :]
            m_new = jnp.maximum(m_old, jnp.max(s, axis=0, keepdims=True))
            alpha = jnp.exp(m_old - m_new)
            p = jnp.exp(s - m_new)
            m_ref[hd:hd + 1, :] = m_new
            l_ref[hd:hd + 1, :] = alpha * l_ref[hd:hd + 1, :] + jnp.sum(p, axis=0, keepdims=True)
            ot_ref[rows, :] = alpha * ot_ref[rows, :] + _dot(vt_ref[kt, rows, :], p.astype(BF16))
        return carry

    lax.fori_loop(0, nkt, attn_tile, 0)
    for hd in range(ATTN_HEADS):
        rows = slice(hd * HEAD_DIM, (hd + 1) * HEAD_DIM)
        ot_ref[rows, :] = ot_ref[rows, :] / l_ref[hd:hd + 1, :]
    out_ref[...] = ot_ref[...].T.astype(out_ref.dtype)


def _dsa_call(qit, wit, qt, ki, k, vt, *, tq, topk):
    B, S, _ = k.shape
    _, nkt, _, tk = vt.shape
    kernel = functools.partial(_dsa_kernel, topk=topk)
    return pl.pallas_call(
        kernel,
        grid=(B, S // tq),
        in_specs=[
            pl.BlockSpec((None, IDX_HEADS * IDX_DIM, tq), lambda b, i: (b, 0, i)),
            pl.BlockSpec((None, IDX_HEADS, tq), lambda b, i: (b, 0, i)),
            pl.BlockSpec((None, ATTN_WIDTH, tq), lambda b, i: (b, 0, i)),
            pl.BlockSpec((None, S, LANES), lambda b, i: (b, 0, 0)),
            pl.BlockSpec((None, S, ATTN_WIDTH), lambda b, i: (b, 0, 0)),
            pl.BlockSpec((None, nkt, ATTN_WIDTH, tk), lambda b, i: (b, 0, 0, 0)),
        ],
        out_specs=pl.BlockSpec((None, tq, ATTN_WIDTH), lambda b, i: (b, i, 0)),
        out_shape=jax.ShapeDtypeStruct((B, S, ATTN_WIDTH), BF16),
        scratch_shapes=[
            pltpu.VMEM((nkt, tk, tq), I32),
            pltpu.VMEM((ATTN_WIDTH, tq), F32),
            pltpu.VMEM((ATTN_HEADS, tq), F32),
            pltpu.VMEM((ATTN_HEADS, tq), F32),
        ],
        compiler_params=pltpu.CompilerParams(
            dimension_semantics=("arbitrary", "arbitrary"), vmem_limit_bytes=VMEM_LIMIT_BYTES),
    )(qit, wit, qt, ki, k, vt)


def _memkv_kernel(mem_ref, g_ref, wkv_ref, kn_ref, k_ref, v_ref):
    m = _rms_rows(mem_ref[...], g_ref[...]).astype(BF16)
    kv = _dot(m, wkv_ref[...])
    for hd in range(MEM_HEADS):
        cols = slice(hd * MEM_HEAD_DIM, (hd + 1) * MEM_HEAD_DIM)
        k_ref[:, cols] = _rms_rows(kv[:, cols], kn_ref[...]).astype(k_ref.dtype)
    v_ref[...] = kv[:, MEM_WIDTH:].astype(v_ref.dtype)


def _memkv_call(mem, g, wkv, kn):
    B, M, D = mem.shape
    full = lambda shape: pl.BlockSpec(shape, lambda b: (0,) * len(shape))
    return pl.pallas_call(
        _memkv_kernel,
        grid=(B,),
        in_specs=[pl.BlockSpec((None, M, D), lambda b: (b, 0, 0)), full(g.shape), full(wkv.shape), full(kn.shape)],
        out_specs=(pl.BlockSpec((None, M, MEM_WIDTH), lambda b: (b, 0, 0)),
                   pl.BlockSpec((None, M, MEM_WIDTH), lambda b: (b, 0, 0))),
        out_shape=(jax.ShapeDtypeStruct((B, M, MEM_WIDTH), BF16), jax.ShapeDtypeStruct((B, M, MEM_WIDTH), BF16)),
        compiler_params=pltpu.CompilerParams(dimension_semantics=("arbitrary",), vmem_limit_bytes=VMEM_LIMIT_BYTES),
    )(mem, g, wkv, kn)


def _split_bf16(a):
    hi = a.astype(BF16)
    lo = (a - hi.astype(F32)).astype(BF16)
    return hi, lo


def _mid_kernel(x_ref, yp_ref, ya_ref, wo1_ref, wo2_ref, gx_ref, wq_ref, qn_ref, km_ref, vm_ref, wo_ref,
                gf_ref, wr_ref, br_ref, x2_ref, h3_ref, gates_ref):
    tm = x_ref.shape[0]
    x1 = x_ref[...] + _dot(yp_ref[...], wo1_ref[...]) + _dot(ya_ref[...], wo2_ref[...])

    h = _rms_rows(x1, gx_ref[...]).astype(BF16)
    q = _dot(h, wq_ref[...])
    heads = []
    for hd in range(MEM_HEADS):
        cols = slice(hd * MEM_HEAD_DIM, (hd + 1) * MEM_HEAD_DIM)
        qh = (_rms_rows(q[:, cols], qn_ref[...]) * (MEM_HEAD_DIM ** -0.5)).astype(BF16)
        s = _dot_nt(qh, km_ref[:, cols])
        p = jnp.exp(s - jnp.max(s, axis=-1, keepdims=True))
        p = p / jnp.sum(p, axis=-1, keepdims=True)
        heads.append(_dot(p.astype(BF16), vm_ref[:, cols]))
    o = jnp.concatenate(heads, axis=-1).astype(BF16)
    x2 = x1 + _dot(o, wo_ref[...])
    x2_ref[...] = x2

    h3 = _rms_rows(x2, gf_ref[...])
    h3_ref[...] = h3.astype(h3_ref.dtype)
    h_hi, h_lo = _split_bf16(h3)
    w_hi, w_lo = _split_bf16(wr_ref[...])
    logits = _dot(h_hi, w_hi) + (_dot(h_hi, w_lo) + _dot(h_lo, w_hi)) + br_ref[...]
    lane = lax.broadcasted_iota(I32, (tm, LANES), 1).astype(F32)
    neg_inf = -jnp.inf
    g_logit = jnp.where(lane < N_GROUPS, logits, neg_inf)
    g_max = jnp.max(g_logit, axis=-1, keepdims=True)
    g_sel = jnp.min(jnp.where(g_logit == g_max, lane, LANES), axis=-1, keepdims=True)
    g_w = 1.0 / jnp.sum(jnp.exp(g_logit - g_max), axis=-1, keepdims=True)
    e_lo = N_GROUPS + g_sel * EXPERTS_PER_GROUP
    in_group = (lane >= e_lo) & (lane < e_lo + EXPERTS_PER_GROUP)
    e_logit = jnp.where(in_group, logits, neg_inf)
    v1 = jnp.max(e_logit, axis=-1, keepdims=True)
    i1 = jnp.min(jnp.where(e_logit == v1, lane, LANES), axis=-1, keepdims=True)
    rest = jnp.where(lane == i1, neg_inf, e_logit)
    v2 = jnp.max(rest, axis=-1, keepdims=True)
    i2 = jnp.min(jnp.where(rest == v2, lane, LANES), axis=-1, keepdims=True)
    e2 = jnp.exp(v2 - v1)
    w1 = g_w / (1.0 + e2)
    w2 = g_w * e2 / (1.0 + e2)
    gates = jnp.where(lane == i1 - N_GROUPS, w1, 0.0) + jnp.where(lane == i2 - N_GROUPS, w2, 0.0)
    gates_ref[...] = gates


def _mid_call(x, ypool, yattn, wo1, wo2, gx, wq, qn, kmem, vmem, wo, gf, wr, br, *, tm):
    B, S, D = x.shape
    M = kmem.shape[1]
    full = lambda shape: pl.BlockSpec(shape, lambda b, i: (0,) * len(shape))
    tile = lambda width: pl.BlockSpec((None, tm, width), lambda b, i: (b, i, 0))
    return pl.pallas_call(
        _mid_kernel,
        grid=(B, S // tm),
        in_specs=[tile(D), tile(POOL_WIDTH), tile(ATTN_WIDTH), full(wo1.shape), full(wo2.shape), full(gx.shape),
                  full(wq.shape), full(qn.shape),
                  pl.BlockSpec((None, M, MEM_WIDTH), lambda b, i: (b, 0, 0)),
                  pl.BlockSpec((None, M, MEM_WIDTH), lambda b, i: (b, 0, 0)),
                  full(wo.shape), full(gf.shape), full(wr.shape), full(br.shape)],
        out_specs=(tile(D), tile(D), tile(LANES)),
        out_shape=(jax.ShapeDtypeStruct((B, S, D), F32), jax.ShapeDtypeStruct((B, S, D), BF16),
                   jax.ShapeDtypeStruct((B, S, LANES), F32)),
        compiler_params=pltpu.CompilerParams(
            dimension_semantics=("arbitrary", "arbitrary"), vmem_limit_bytes=VMEM_LIMIT_BYTES),
    )(x, ypool, yattn, wo1, wo2, gx, wq, qn, kmem, vmem, wo, gf, wr, br)


def _moe_kernel(x2_ref, h3_ref, gates_ref, wg_ref, wu_ref, wd_ref, out_ref):
    e = pl.program_id(1)

    @pl.when(e == 0)
    def _():
        out_ref[...] = x2_ref[...]

    h = h3_ref[...]
    a = _dot(h, wg_ref[...])
    b = _dot(h, wu_ref[...])
    lane = lax.broadcasted_iota(I32, gates_ref.shape, 1)
    gate = jnp.sum(jnp.where(lane == e, gates_ref[...], 0.0), axis=-1, keepdims=True)
    hid = (a * jax.nn.sigmoid(a) * b * gate).astype(BF16)
    out_ref[...] += _dot(hid, wd_ref[...])


def _moe_call(x2, h3, gates, wg, wu, wd, *, tm):
    N, D = x2.shape
    E, _, F = wg.shape
    return pl.pallas_call(
        _moe_kernel,
        grid=(N // tm, E),
        in_specs=[pl.BlockSpec((tm, D), lambda i, e: (i, 0)),
                  pl.BlockSpec((tm, D), lambda i, e: (i, 0)),
                  pl.BlockSpec((tm, LANES), lambda i, e: (i, 0)),
                  pl.BlockSpec((None, D, F), lambda i, e: (e, 0, 0)),
                  pl.BlockSpec((None, D, F), lambda i, e: (e, 0, 0)),
                  pl.BlockSpec((None, F, D), lambda i, e: (e, 0, 0))],
        out_specs=pl.BlockSpec((tm, D), lambda i, e: (i, 0)),
        out_shape=jax.ShapeDtypeStruct((N, D), F32),
        compiler_params=pltpu.CompilerParams(
            dimension_semantics=("arbitrary", "arbitrary"), vmem_limit_bytes=VMEM_LIMIT_BYTES),
    )(x2, h3, gates, wg, wu, wd)


def _rope_tables_t(seq_len, dim):
    inv = ROPE_THETA ** (-jnp.arange(0, dim, 2, dtype=F32) / dim)
    ang = jnp.arange(seq_len, dtype=F32)[:, None] * inv[None, :]
    ang = jnp.concatenate([ang, ang], axis=-1)
    sign = jnp.concatenate([-jnp.ones((dim // 2,), F32), jnp.ones((dim // 2,), F32)])
    return jnp.cos(ang).T, (jnp.sin(ang) * sign[None, :]).T


def _layer(x, mem, mix_norm, w_in, pool_w, pool_scale, q_norm, k_norm, idx_k_norm, w_out,
           xattn_norm, mem_norm, xattn_wq, xattn_wkv, xattn_q_norm, xattn_k_norm, xattn_wo,
           ffn_norm, router_group_w, router_group_b, router_expert_w, router_expert_b,
           expert_w_gate, expert_w_up, expert_w_down, *, tm, tq):
    B, S, D = x.shape
    topk = min(TOPK_MAX, S // 4)
    row = lambda v: v.reshape(1, -1).astype(F32)
    col = lambda v: jnp.broadcast_to(v.astype(F32)[:, None], (v.shape[0], tm))

    wu = w_in[:, :POOL_WIDTH].astype(BF16)
    pad = lambda n: jnp.zeros((n, D), F32)
    wt = jnp.concatenate([
        w_in[:, OFF_Q:OFF_KI].T, w_in[:, OFF_KI:OFF_WI].T, pad(LANES - IDX_DIM),
        w_in[:, OFF_WI:IN_COLS].T, pad(2 * SUBLANES - IDX_HEADS)], axis=0).astype(BF16)
    cos_t, sin_t = _rope_tables_t(S, HEAD_DIM)

    ypool, qt, k, vt, qit, ki, wit = _proj_call(
        x, row(mix_norm), wu, wt, pool_w.astype(BF16), row(pool_scale), col(q_norm), col(k_norm),
        col(idx_k_norm), cos_t, sin_t, tm=tm)
    yattn = _dsa_call(qit, wit, qt, ki, k, vt, tq=tq, topk=topk)
    kmem, vmem = _memkv_call(mem, row(mem_norm), xattn_wkv.astype(BF16), row(xattn_k_norm))

    n_logits = N_GROUPS + N_EXPERTS
    wr = jnp.concatenate([router_group_w, router_expert_w, jnp.zeros((D, LANES - n_logits), F32)], axis=1)
    br = jnp.concatenate([router_group_b, router_expert_b, jnp.zeros((LANES - n_logits,), F32)]).reshape(1, LANES)
    x2, h3, gates = _mid_call(
        x, ypool, yattn, w_out[:POOL_WIDTH].astype(BF16), w_out[POOL_WIDTH:].astype(BF16), row(xattn_norm),
        xattn_wq.astype(BF16), row(xattn_q_norm), kmem, vmem, xattn_wo.astype(BF16), row(ffn_norm), wr, br,
        tm=tm)

    F = expert_w_gate.shape[-1]
    out = _moe_call(
        x2.reshape(B * S, D), h3.reshape(B * S, D), gates.reshape(B * S, LANES),
        expert_w_gate.reshape(N_EXPERTS, D, F).astype(BF16), expert_w_up.reshape(N_EXPERTS, D, F).astype(BF16),
        expert_w_down.reshape(N_EXPERTS, F, D).astype(BF16), tm=tm)
    return out.reshape(B, S, D)


def kernel(x, mem, mix_norm, w_in, pool_w, pool_scale, q_norm, k_norm, idx_k_norm, w_out, xattn_norm, mem_norm,
           xattn_wq, xattn_wkv, xattn_q_norm, xattn_k_norm, xattn_wo, ffn_norm, router_group_w, router_group_b,
           router_expert_w, router_expert_b, expert_w_gate, expert_w_up, expert_w_down):
    depth = mix_norm.shape[0]
    for l in range(depth):
        x = _layer(
            x, mem, mix_norm[l], w_in[l], pool_w[l], pool_scale[l], q_norm[l], k_norm[l], idx_k_norm[l], w_out[l],
            xattn_norm[l], mem_norm[l], xattn_wq[l], xattn_wkv[l], xattn_q_norm[l], xattn_k_norm[l], xattn_wo[l],
            ffn_norm[l], router_group_w[l], router_group_b[l], router_expert_w[l], router_expert_b[l],
            expert_w_gate[l], expert_w_up[l], expert_w_down[l], tm=512, tq=512)
    return x
```

```python
import functools
import math

import jax
import jax.numpy as jnp
from jax import lax
from jax.experimental import pallas as pl
from jax.experimental.pallas import tpu as pltpu

CHUNK = 64
POOL_WINDOWS = (2, 4, 8, 16)
POOL_GROUP = 128
POOL_WIDTH = POOL_GROUP * len(POOL_WINDOWS)
ATTN_HEADS = 8
HEAD_DIM = 64
ATTN_WIDTH = ATTN_HEADS * HEAD_DIM
IDX_HEADS = 8
IDX_DIM = 64
TOPK_MAX = 256
ROPE_THETA = 10000.0
MEM_HEADS = 4
MEM_HEAD_DIM = 128
MEM_WIDTH = MEM_HEADS * MEM_HEAD_DIM
N_GROUPS = 4
EXPERTS_PER_GROUP = 8
N_EXPERTS = N_GROUPS * EXPERTS_PER_GROUP
EPS = 1e-6
OFF_Q = POOL_WIDTH
OFF_K = OFF_Q + ATTN_WIDTH
OFF_V = OFF_K + ATTN_WIDTH
OFF_QI = OFF_V + ATTN_WIDTH
OFF_KI = OFF_QI + IDX_HEADS * IDX_DIM
OFF_WI = OFF_KI + IDX_DIM
IN_COLS = OFF_WI + IDX_HEADS

LANES = 128
SUBLANES = 8
VMEM_LIMIT_BYTES = 56 * 1024 * 1024

HALO = 16
PACKED_SUBLANES = 2 * SUBLANES
INT_MIN = -(2 ** 31)
HALF_MIN = -(2 ** 15)
NEG_BIG = -1e30

F32 = jnp.float32
BF16 = jnp.bfloat16
I32 = jnp.int32
I16 = jnp.int16


def _dot(a, b):
    return jnp.dot(a, b, preferred_element_type=F32)


def _dot_nt(a, b):
    return lax.dot_general(a, b, (((1,), (1,)), ((), ())), preferred_element_type=F32)


def _rms_rows(x, g):
    ms = jnp.mean(x * x, axis=-1, keepdims=True)
    return x * lax.rsqrt(ms + EPS) * g


ROW_Q = 0
ROW_K = ROW_Q + ATTN_WIDTH
ROW_V = ROW_K + ATTN_WIDTH
ROW_QI = ROW_V + ATTN_WIDTH
ROW_KI = ROW_QI + IDX_HEADS * IDX_DIM
ROW_WI = ROW_KI + LANES
ROWS_T = ROW_WI + 2 * SUBLANES


def _norm_rope_t(z, gain, cos, sin_signed):
    if gain is not None:
        ms = jnp.mean(z * z, axis=0, keepdims=True)
        z = z * lax.rsqrt(ms + EPS) * gain
    half = z.shape[0] // 2
    swapped = jnp.concatenate([z[half:], z[:half]], axis=0)
    return z * cos + swapped * sin_signed


def _proj_kernel(x_ref, g_ref, wu_ref, wt_ref, poolw_ref, pscale_ref, qg_ref, kg_ref, ig_ref,
                 cos_ref, sin_ref,
                 ypool_ref, qt_ref, k_ref, vt_ref, qit_ref, ki_ref, wit_ref,
                 halo_ref, ext_ref, kt_ref, *, idx_scale):
    tm = x_ref.shape[0]
    h = _rms_rows(x_ref[...], g_ref[...]).astype(BF16)
    cos = cos_ref[...]
    sin = sin_ref[...]

    u = _dot(h, wu_ref[...])
    @pl.when(pl.program_id(1) == 0)
    def _():
        halo_ref[...] = jnp.zeros_like(halo_ref)

    ext_ref[0:HALO, :] = halo_ref[...]
    ext_ref[HALO:HALO + tm, :] = u
    halo_ref[...] = u[tm - HALO:, :]
    pos1 = pl.program_id(1) * tm + lax.broadcasted_iota(I32, (tm, 1), 0) + 1
    for g, w in enumerate(POOL_WINDOWS):
        cols = slice(g * POOL_GROUP, (g + 1) * POOL_GROUP)
        win = u[:, cols]
        for j in range(1, w):
            win = win + ext_ref[HALO - j:HALO - j + tm, cols]
        cnt = jnp.minimum(pos1, w).astype(F32)
        mixed = (win / cnt - u[:, cols]).astype(BF16)
        y = _dot(mixed, poolw_ref[g]) * pscale_ref[:, cols]
        ypool_ref[:, cols] = y.astype(ypool_ref.dtype)

    qg = qg_ref[...]
    kg = kg_ref[...]
    zq = _dot_nt(wt_ref[ROW_Q:ROW_K, :], h)
    for hd in range(ATTN_HEADS):
        rows = slice(hd * HEAD_DIM, (hd + 1) * HEAD_DIM)
        qt_ref[rows, :] = (_norm_rope_t(zq[rows], qg, cos, sin) * (HEAD_DIM ** -0.5)).astype(qt_ref.dtype)
    zk = _dot_nt(wt_ref[ROW_K:ROW_V, :], h)
    for hd in range(ATTN_HEADS):
        rows = slice(hd * HEAD_DIM, (hd + 1) * HEAD_DIM)
        kt_ref[rows, :] = _norm_rope_t(zk[rows], kg, cos, sin)
    k_ref[...] = kt_ref[...].T.astype(k_ref.dtype)
    vt_ref[...] = _dot_nt(wt_ref[ROW_V:ROW_QI, :], h).astype(vt_ref.dtype)
    zqi = _dot_nt(wt_ref[ROW_QI:ROW_KI, :], h)
    for hd in range(IDX_HEADS):
        rows = slice(hd * IDX_DIM, (hd + 1) * IDX_DIM)
        qit_ref[rows, :] = _norm_rope_t(zqi[rows], None, cos, sin).astype(qit_ref.dtype)
    zi = _dot_nt(wt_ref[ROW_KI:ROWS_T, :], h)
    kit = _norm_rope_t(zi[0:IDX_DIM], ig_ref[...], cos, sin)
    kit = jnp.concatenate([kit, jnp.zeros((LANES - IDX_DIM, tm), F32)], axis=0)
    ki_ref[...] = kit.T.astype(ki_ref.dtype)
    wit_ref[...] = zi[LANES:LANES + IDX_HEADS] * idx_scale


def _proj_call(x, g, wu, wt, poolw, pscale, qg, kg, ig, cos_t, sin_t, *, tm):
    B, S, D = x.shape
    nt = S // tm
    kernel = functools.partial(_proj_kernel, idx_scale=(IDX_DIM ** -0.5) * (IDX_HEADS ** -0.5))
    full = lambda shape: pl.BlockSpec(shape, lambda b, i: (0,) * len(shape))
    out_shape = (
        jax.ShapeDtypeStruct((B, S, POOL_WIDTH), BF16),
        jax.ShapeDtypeStruct((B, ATTN_WIDTH, S), BF16),
        jax.ShapeDtypeStruct((B, S, ATTN_WIDTH), BF16),
        jax.ShapeDtypeStruct((B, nt, ATTN_WIDTH, tm), BF16),
        jax.ShapeDtypeStruct((B, IDX_HEADS * IDX_DIM, S), BF16),
        jax.ShapeDtypeStruct((B, S, LANES), BF16),
        jax.ShapeDtypeStruct((B, IDX_HEADS, S), F32),
    )
    return pl.pallas_call(
        kernel,
        grid=(B, nt),
        in_specs=[
            pl.BlockSpec((None, tm, D), lambda b, i: (b, i, 0)),
            full(g.shape), full(wu.shape), full(wt.shape), full(poolw.shape), full(pscale.shape),
            full(qg.shape), full(kg.shape), full(ig.shape),
            pl.BlockSpec((HEAD_DIM, tm), lambda b, i: (0, i)),
            pl.BlockSpec((HEAD_DIM, tm), lambda b, i: (0, i)),
        ],
        out_specs=(
            pl.BlockSpec((None, tm, POOL_WIDTH), lambda b, i: (b, i, 0)),
            pl.BlockSpec((None, ATTN_WIDTH, tm), lambda b, i: (b, 0, i)),
            pl.BlockSpec((None, tm, ATTN_WIDTH), lambda b, i: (b, i, 0)),
            pl.BlockSpec((None, None, ATTN_WIDTH, tm), lambda b, i: (b, i, 0, 0)),
            pl.BlockSpec((None, IDX_HEADS * IDX_DIM, tm), lambda b, i: (b, 0, i)),
            pl.BlockSpec((None, tm, LANES), lambda b, i: (b, i, 0)),
            pl.BlockSpec((None, IDX_HEADS, tm), lambda b, i: (b, 0, i)),
        ),
        out_shape=out_shape,
        scratch_shapes=[
            pltpu.VMEM((HALO, POOL_WIDTH), F32),
            pltpu.VMEM((HALO + tm, POOL_WIDTH), F32),
            pltpu.VMEM((ATTN_WIDTH, tm), F32),
        ],
        compiler_params=pltpu.CompilerParams(
            dimension_semantics=("arbitrary", "arbitrary"), vmem_limit_bytes=VMEM_LIMIT_BYTES),
    )(x, g, wu, wt, poolw, pscale, qg, kg, ig, cos_t, sin_t)


def _sortable_key(score):
    bits = lax.bitcast_convert_type(score, I32)
    key = jnp.where(bits < 0, bits ^ jnp.int32(0x7FFFFFFF), bits)
    return jnp.where(key == -1, 0, key)


def _colsum8(x):
    rows, t = x.shape
    return jnp.sum(x.reshape(rows // SUBLANES, SUBLANES, t), axis=0)


def _dsa_kernel(qit_ref, wit_ref, qt_ref, ki_ref, k_ref, vt_ref, out_ref, sc_ref, half_ref, ot_ref, m_ref, l_ref,
                *, topk):
    tq = qit_ref.shape[1]
    nkt_max, tk, _ = sc_ref.shape
    q0 = pl.program_id(1) * tq
    nkt = (q0 + tq + tk - 1) // tk
    qpos = q0 + lax.broadcasted_iota(I32, (1, tq), 1)
    qend = (qpos // CHUNK + 1) * CHUNK
    krow = lax.broadcasted_iota(I32, (tk, 1), 0)
    zeros_half = jnp.zeros((LANES - IDX_DIM, tq), BF16)

    def score_tile(kt, carry):
        ki_t = ki_ref[pl.ds(pl.multiple_of(kt * tk, tk), tk), :]
        acc = jnp.zeros((tk, tq), F32)
        for hd in range(IDX_HEADS):
            qh = jnp.concatenate([qit_ref[hd * IDX_DIM:(hd + 1) * IDX_DIM, :], zeros_half], axis=0)
            rel = jnp.maximum(_dot(ki_t, qh), 0.0)
            acc = acc + rel * wit_ref[hd:hd + 1, :]
        key = jnp.where(kt * tk + krow < qend, _sortable_key(acc), INT_MIN)
        sc_ref[kt] = key
        half_ref[kt] = lax.shift_right_arithmetic(key, 16).astype(I16)
        return carry

    lax.fori_loop(0, nkt, score_tile, 0)

    def count(pred_fn):
        def body(kt, c8):
            return c8 + _colsum8(jnp.where(pred_fn(sc_ref[kt], kt), 1, 0).astype(I32))
        c8 = lax.fori_loop(0, nkt, body, jnp.zeros((SUBLANES, tq), I32))
        return jnp.sum(c8, axis=0, keepdims=True)

    def count_half(pred_fn):
        def body(kt, c16):
            ones = jnp.where(pred_fn(half_ref[kt]), jnp.int16(1), jnp.int16(0))
            for j in range(tk // PACKED_SUBLANES):
                c16 = c16 + ones[j * PACKED_SUBLANES:(j + 1) * PACKED_SUBLANES]
            return c16
        c16 = lax.fori_loop(0, nkt, body, jnp.zeros((PACKED_SUBLANES, tq), I16))
        return jnp.sum(c16.astype(I32), axis=0, keepdims=True)

    def bisect_half():
        def step(i, t):
            cand = t + lax.shift_left(jnp.int32(1), 15 - i)
            cand16 = cand.astype(I16)
            c = count_half(lambda half: half >= cand16)
            return jnp.where(c >= topk, cand, t)
        return lax.fori_loop(0, 16, step, jnp.full((1, tq), HALF_MIN, I32))

    thr_hi = bisect_half()

    def low_tile(kt, carry):
        hi = half_ref[kt].astype(I32)
        lo = (sc_ref[kt] & 0xFFFF) + HALF_MIN
        lo = jnp.where(hi == thr_hi, lo, jnp.where(hi > thr_hi, -HALF_MIN - 1, HALF_MIN))
        half_ref[kt] = lo.astype(I16)
        return carry

    lax.fori_loop(0, nkt, low_tile, 0)
    thr_lo = bisect_half()
    thr = thr_hi * 65536 + (thr_lo - HALF_MIN)

    n_above = count(lambda key, kt: key > thr)
    n_ties = count(lambda key, kt: key == thr)
    need = topk - n_above
    nbits = max(1, (nkt_max * tk - 1).bit_length())

    def tie_step(step, cut):
        cand = cut + lax.shift_left(jnp.int32(1), nbits - 1 - step)
        c = count(lambda key, kt: (key == thr) & (kt * tk + krow < cand))
        return jnp.where(c < need, cand, cut)

    def resolve_ties():
        return lax.fori_loop(0, nbits, tie_step, jnp.zeros((1, tq), I32))

    cut = lax.cond(jnp.max(n_ties - need) > 0, resolve_ties,
                   lambda: jnp.full((1, tq), nkt_max * tk, I32))

    def mask_tile(kt, carry):
        key = sc_ref[kt]
        sel = (key > thr) | ((key == thr) & (kt * tk + krow <= cut))
        sel = sel & (key > INT_MIN)
        sc_ref[kt] = lax.bitcast_convert_type(jnp.where(sel, 0.0, NEG_BIG).astype(F32), I32)
        return carry

    lax.fori_loop(0, nkt, mask_tile, 0)

    zeros_head = jnp.zeros((HEAD_DIM, tq), BF16)
    m_ref[...] = jnp.full(m_ref.shape, NEG_BIG, F32)
    l_ref[...] = jnp.zeros(l_ref.shape, F32)
    ot_ref[...] = jnp.zeros(ot_ref.shape, F32)

    def attn_tile(kt, carry):
        krows = pl.ds(pl.multiple_of(kt * tk, tk), tk)

        for hd in range(ATTN_HEADS):
            pair = hd // 2
            rows = slice(hd * HEAD_DIM, (hd + 1) * HEAD_DIM)
            qh = qt_ref[rows, :]
            qh = jnp.concatenate([qh, zeros_head] if hd % 2 == 0 else [zeros_head, qh], axis=0)
            k2 = k_ref[krows, pair * LANES:(pair + 1) * LANES]
            s = _dot(k2, qh) + lax.bitcast_convert_type(sc_ref[kt], F32)
            m_old = m_ref[hd:hd + 1, :]
            m_new = jnp.maximum(m_old, jnp.max(s, axis=0, keepdims=True))
            alpha = jnp.exp(m_old - m_new)
            p = jnp.exp(s - m_new)
            m_ref[hd:hd + 1, :] = m_new
            l_ref[hd:hd + 1, :] = alpha * l_ref[hd:hd + 1, :] + jnp.sum(p, axis=0, keepdims=True)
            ot_ref[rows, :] = alpha * ot_ref[rows, :] + _dot(vt_ref[kt, rows, :], p.astype(BF16))
        return carry

    lax.fori_loop(0, nkt, attn_tile, 0)
    for hd in range(ATTN_HEADS):
        rows = slice(hd * HEAD_DIM, (hd + 1) * HEAD_DIM)
        ot_ref[rows, :] = ot_ref[rows, :] / l_ref[hd:hd + 1, :]
    out_ref[...] = ot_ref[...].T.astype(out_ref.dtype)


def _dsa_call(qit, wit, qt, ki, k, vt, *, tq, topk):
    B, S, _ = k.shape
    _, nkt, _, tk = vt.shape
    kernel = functools.partial(_dsa_kernel, topk=topk)
    return pl.pallas_call(
        kernel,
        grid=(B, S // tq),
        in_specs=[
            pl.BlockSpec((None, IDX_HEADS * IDX_DIM, tq), lambda b, i: (b, 0, i)),
            pl.BlockSpec((None, IDX_HEADS, tq), lambda b, i: (b, 0, i)),
            pl.BlockSpec((None, ATTN_WIDTH, tq), lambda b, i: (b, 0, i)),
            pl.BlockSpec((None, S, LANES), lambda b, i: (b, 0, 0)),
            pl.BlockSpec((None, S, ATTN_WIDTH), lambda b, i: (b, 0, 0)),
            pl.BlockSpec((None, nkt, ATTN_WIDTH, tk), lambda b, i: (b, 0, 0, 0)),
        ],
        out_specs=pl.BlockSpec((None, tq, ATTN_WIDTH), lambda b, i: (b, i, 0)),
        out_shape=jax.ShapeDtypeStruct((B, S, ATTN_WIDTH), BF16),
        scratch_shapes=[
            pltpu.VMEM((nkt, tk, tq), I32),
            pltpu.VMEM((nkt, tk, tq), I16),
            pltpu.VMEM((ATTN_WIDTH, tq), F32),
            pltpu.VMEM((ATTN_HEADS, tq), F32),
            pltpu.VMEM((ATTN_HEADS, tq), F32),
        ],
        compiler_params=pltpu.CompilerParams(
            dimension_semantics=("arbitrary", "arbitrary"), vmem_limit_bytes=VMEM_LIMIT_BYTES),
    )(qit, wit, qt, ki, k, vt)


def _memkv_kernel(mem_ref, g_ref, wkv_ref, kn_ref, k_ref, v_ref):
    m = _rms_rows(mem_ref[...], g_ref[...]).astype(BF16)
    kv = _dot(m, wkv_ref[...])
    for hd in range(MEM_HEADS):
        cols = slice(hd * MEM_HEAD_DIM, (hd + 1) * MEM_HEAD_DIM)
        k_ref[:, cols] = _rms_rows(kv[:, cols], kn_ref[...]).astype(k_ref.dtype)
    v_ref[...] = kv[:, MEM_WIDTH:].astype(v_ref.dtype)


def _memkv_call(mem, g, wkv, kn):
    B, M, D = mem.shape
    full = lambda shape: pl.BlockSpec(shape, lambda b: (0,) * len(shape))
    return pl.pallas_call(
        _memkv_kernel,
        grid=(B,),
        in_specs=[pl.BlockSpec((None, M, D), lambda b: (b, 0, 0)), full(g.shape), full(wkv.shape), full(kn.shape)],
        out_specs=(pl.BlockSpec((None, M, MEM_WIDTH), lambda b: (b, 0, 0)),
                   pl.BlockSpec((None, M, MEM_WIDTH), lambda b: (b, 0, 0))),
        out_shape=(jax.ShapeDtypeStruct((B, M, MEM_WIDTH), BF16), jax.ShapeDtypeStruct((B, M, MEM_WIDTH), BF16)),
        compiler_params=pltpu.CompilerParams(dimension_semantics=("arbitrary",), vmem_limit_bytes=VMEM_LIMIT_BYTES),
    )(mem, g, wkv, kn)


def _split_bf16(a):
    hi = a.astype(BF16)
    lo = (a - hi.astype(F32)).astype(BF16)
    return hi, lo


def _mid_kernel(x_ref, yp_ref, ya_ref, wo1_ref, wo2_ref, gx_ref, wq_ref, qn_ref, km_ref, vm_ref, wo_ref,
                gf_ref, wr_ref, br_ref, x2_ref, h3_ref, gates_ref):
    tm = x_ref.shape[0]
    x1 = x_ref[...] + _dot(yp_ref[...], wo1_ref[...]) + _dot(ya_ref[...], wo2_ref[...])

    h = _rms_rows(x1, gx_ref[...]).astype(BF16)
    q = _dot(h, wq_ref[...])
    heads = []
    for hd in range(MEM_HEADS):
        cols = slice(hd * MEM_HEAD_DIM, (hd + 1) * MEM_HEAD_DIM)
        qh = (_rms_rows(q[:, cols], qn_ref[...]) * (MEM_HEAD_DIM ** -0.5)).astype(BF16)
        s = _dot_nt(qh, km_ref[:, cols])
        p = jnp.exp(s - jnp.max(s, axis=-1, keepdims=True))
        p = p / jnp.sum(p, axis=-1, keepdims=True)
        heads.append(_dot(p.astype(BF16), vm_ref[:, cols]))
    o = jnp.concatenate(heads, axis=-1).astype(BF16)
    x2 = x1 + _dot(o, wo_ref[...])
    x2_ref[...] = x2

    h3 = _rms_rows(x2, gf_ref[...])
    h3_ref[...] = h3.astype(h3_ref.dtype)
    h_hi, h_lo = _split_bf16(h3)
    w_hi, w_lo = _split_bf16(wr_ref[...])
    logits = _dot(h_hi, w_hi) + (_dot(h_hi, w_lo) + _dot(h_lo, w_hi)) + br_ref[...]
    lane = lax.broadcasted_iota(I32, (tm, LANES), 1).astype(F32)
    neg_inf = -jnp.inf
    g_logit = jnp.where(lane < N_GROUPS, logits, neg_inf)
    g_max = jnp.max(g_logit, axis=-1, keepdims=True)
    g_sel = jnp.min(jnp.where(g_logit == g_max, lane, LANES), axis=-1, keepdims=True)
    g_w = 1.0 / jnp.sum(jnp.exp(g_logit - g_max), axis=-1, keepdims=True)
    e_lo = N_GROUPS + g_sel * EXPERTS_PER_GROUP
    in_group = (lane >= e_lo) & (lane < e_lo + EXPERTS_PER_GROUP)
    e_logit = jnp.where(in_group, logits, neg_inf)
    v1 = jnp.max(e_logit, axis=-1, keepdims=True)
    i1 = jnp.min(jnp.where(e_logit == v1, lane, LANES), axis=-1, keepdims=True)
    rest = jnp.where(lane == i1, neg_inf, e_logit)
    v2 = jnp.max(rest, axis=-1, keepdims=True)
    i2 = jnp.min(jnp.where(rest == v2, lane, LANES), axis=-1, keepdims=True)
    e2 = jnp.exp(v2 - v1)
    w1 = g_w / (1.0 + e2)
    w2 = g_w * e2 / (1.0 + e2)
    gates = jnp.where(lane == i1 - N_GROUPS, w1, 0.0) + jnp.where(lane == i2 - N_GROUPS, w2, 0.0)
    gates_ref[...] = gates


def _mid_call(x, ypool, yattn, wo1, wo2, gx, wq, qn, kmem, vmem, wo, gf, wr, br, *, tm):
    B, S, D = x.shape
    M = kmem.shape[1]
    full = lambda shape: pl.BlockSpec(shape, lambda b, i: (0,) * len(shape))
    tile = lambda width: pl.BlockSpec((None, tm, width), lambda b, i: (b, i, 0))
    return pl.pallas_call(
        _mid_kernel,
        grid=(B, S // tm),
        in_specs=[tile(D), tile(POOL_WIDTH), tile(ATTN_WIDTH), full(wo1.shape), full(wo2.shape), full(gx.shape),
                  full(wq.shape), full(qn.shape),
                  pl.BlockSpec((None, M, MEM_WIDTH), lambda b, i: (b, 0, 0)),
                  pl.BlockSpec((None, M, MEM_WIDTH), lambda b, i: (b, 0, 0)),
                  full(wo.shape), full(gf.shape), full(wr.shape), full(br.shape)],
        out_specs=(tile(D), tile(D), tile(LANES)),
        out_shape=(jax.ShapeDtypeStruct((B, S, D), F32), jax.ShapeDtypeStruct((B, S, D), BF16),
                   jax.ShapeDtypeStruct((B, S, LANES), F32)),
        compiler_params=pltpu.CompilerParams(
            dimension_semantics=("arbitrary", "arbitrary"), vmem_limit_bytes=VMEM_LIMIT_BYTES),
    )(x, ypool, yattn, wo1, wo2, gx, wq, qn, kmem, vmem, wo, gf, wr, br)


def _moe_kernel(x2_ref, h3_ref, gates_ref, wg_ref, wu_ref, wd_ref, out_ref):
    e = pl.program_id(1)

    @pl.when(e == 0)
    def _():
        out_ref[...] = x2_ref[...]

    h = h3_ref[...]
    a = _dot(h, wg_ref[...])
    b = _dot(h, wu_ref[...])
    lane = lax.broadcasted_iota(I32, gates_ref.shape, 1)
    gate = jnp.sum(jnp.where(lane == e, gates_ref[...], 0.0), axis=-1, keepdims=True)
    hid = (a * jax.nn.sigmoid(a) * b * gate).astype(BF16)
    out_ref[...] += _dot(hid, wd_ref[...])


def _moe_call(x2, h3, gates, wg, wu, wd, *, tm):
    N, D = x2.shape
    E, _, F = wg.shape
    return pl.pallas_call(
        _moe_kernel,
        grid=(N // tm, E),
        in_specs=[pl.BlockSpec((tm, D), lambda i, e: (i, 0)),
                  pl.BlockSpec((tm, D), lambda i, e: (i, 0)),
                  pl.BlockSpec((tm, LANES), lambda i, e: (i, 0)),
                  pl.BlockSpec((None, D, F), lambda i, e: (e, 0, 0)),
                  pl.BlockSpec((None, D, F), lambda i, e: (e, 0, 0)),
                  pl.BlockSpec((None, F, D), lambda i, e: (e, 0, 0))],
        out_specs=pl.BlockSpec((tm, D), lambda i, e: (i, 0)),
        out_shape=jax.ShapeDtypeStruct((N, D), F32),
        compiler_params=pltpu.CompilerParams(
            dimension_semantics=("arbitrary", "arbitrary"), vmem_limit_bytes=VMEM_LIMIT_BYTES),
    )(x2, h3, gates, wg, wu, wd)


def _rope_tables_t(seq_len, dim):
    inv = ROPE_THETA ** (-jnp.arange(0, dim, 2, dtype=F32) / dim)
    ang = jnp.arange(seq_len, dtype=F32)[:, None] * inv[None, :]
    ang = jnp.concatenate([ang, ang], axis=-1)
    sign = jnp.concatenate([-jnp.ones((dim // 2,), F32), jnp.ones((dim // 2,), F32)])
    return jnp.cos(ang).T, (jnp.sin(ang) * sign[None, :]).T


def _layer(x, mem, mix_norm, w_in, pool_w, pool_scale, q_norm, k_norm, idx_k_norm, w_out,
           xattn_norm, mem_norm, xattn_wq, xattn_wkv, xattn_q_norm, xattn_k_norm, xattn_wo,
           ffn_norm, router_group_w, router_group_b, router_expert_w, router_expert_b,
           expert_w_gate, expert_w_up, expert_w_down, *, tm, tq):
    B, S, D = x.shape
    topk = min(TOPK_MAX, S // 4)
    row = lambda v: v.reshape(1, -1).astype(F32)
    col = lambda v: jnp.broadcast_to(v.astype(F32)[:, None], (v.shape[0], tm))

    wu = w_in[:, :POOL_WIDTH].astype(BF16)
    pad = lambda n: jnp.zeros((n, D), F32)
    wt = jnp.concatenate([
        w_in[:, OFF_Q:OFF_KI].T, w_in[:, OFF_KI:OFF_WI].T, pad(LANES - IDX_DIM),
        w_in[:, OFF_WI:IN_COLS].T, pad(2 * SUBLANES - IDX_HEADS)], axis=0).astype(BF16)
    cos_t, sin_t = _rope_tables_t(S, HEAD_DIM)

    ypool, qt, k, vt, qit, ki, wit = _proj_call(
        x, row(mix_norm), wu, wt, pool_w.astype(BF16), row(pool_scale), col(q_norm), col(k_norm),
        col(idx_k_norm), cos_t, sin_t, tm=tm)
    yattn = _dsa_call(qit, wit, qt, ki, k, vt, tq=tq, topk=topk)
    kmem, vmem = _memkv_call(mem, row(mem_norm), xattn_wkv.astype(BF16), row(xattn_k_norm))

    n_logits = N_GROUPS + N_EXPERTS
    wr = jnp.concatenate([router_group_w, router_expert_w, jnp.zeros((D, LANES - n_logits), F32)], axis=1)
    br = jnp.concatenate([router_group_b, router_expert_b, jnp.zeros((LANES - n_logits,), F32)]).reshape(1, LANES)
    x2, h3, gates = _mid_call(
        x, ypool, yattn, w_out[:POOL_WIDTH].astype(BF16), w_out[POOL_WIDTH:].astype(BF16), row(xattn_norm),
        xattn_wq.astype(BF16), row(xattn_q_norm), kmem, vmem, xattn_wo.astype(BF16), row(ffn_norm), wr, br,
        tm=tm)

    F = expert_w_gate.shape[-1]
    out = _moe_call(
        x2.reshape(B * S, D), h3.reshape(B * S, D), gates.reshape(B * S, LANES),
        expert_w_gate.reshape(N_EXPERTS, D, F).astype(BF16), expert_w_up.reshape(N_EXPERTS, D, F).astype(BF16),
        expert_w_down.reshape(N_EXPERTS, F, D).astype(BF16), tm=tm)
    return out.reshape(B, S, D)


def kernel(x, mem, mix_norm, w_in, pool_w, pool_scale, q_norm, k_norm, idx_k_norm, w_out, xattn_norm, mem_norm,
           xattn_wq, xattn_wkv, xattn_q_norm, xattn_k_norm, xattn_wo, ffn_norm, router_group_w, router_group_b,
           router_expert_w, router_expert_b, expert_w_gate, expert_w_up, expert_w_down):
    depth = mix_norm.shape[0]
    for l in range(depth):
        x = _layer(
            x, mem, mix_norm[l], w_in[l], pool_w[l], pool_scale[l], q_norm[l], k_norm[l], idx_k_norm[l], w_out[l],
            xattn_norm[l], mem_norm[l], xattn_wq[l], xattn_wkv[l], xattn_q_norm[l], xattn_k_norm[l], xattn_wo[l],
            ffn_norm[l], router_group_w[l], router_group_b[l], router_expert_w[l], router_expert_b[l],
            expert_w_gate[l], expert_w_up[l], expert_w_down[l], tm=512, tq=512)
    return x
```

```python
import functools
import math

import jax
import jax.numpy as jnp
from jax import lax
from jax.experimental import pallas as pl
from jax.experimental.pallas import tpu as pltpu

CHUNK = 64
POOL_WINDOWS = (2, 4, 8, 16)
POOL_GROUP = 128
POOL_WIDTH = POOL_GROUP * len(POOL_WINDOWS)
ATTN_HEADS = 8
HEAD_DIM = 64
ATTN_WIDTH = ATTN_HEADS * HEAD_DIM
IDX_HEADS = 8
IDX_DIM = 64
TOPK_MAX = 256
ROPE_THETA = 10000.0
MEM_HEADS = 4
MEM_HEAD_DIM = 128
MEM_WIDTH = MEM_HEADS * MEM_HEAD_DIM
N_GROUPS = 4
EXPERTS_PER_GROUP = 8
N_EXPERTS = N_GROUPS * EXPERTS_PER_GROUP
EPS = 1e-6
OFF_Q = POOL_WIDTH
OFF_K = OFF_Q + ATTN_WIDTH
OFF_V = OFF_K + ATTN_WIDTH
OFF_QI = OFF_V + ATTN_WIDTH
OFF_KI = OFF_QI + IDX_HEADS * IDX_DIM
OFF_WI = OFF_KI + IDX_DIM
IN_COLS = OFF_WI + IDX_HEADS

LANES = 128
SUBLANES = 8
VMEM_LIMIT_BYTES = 56 * 1024 * 1024

HALO = 16
PACKED_SUBLANES = 2 * SUBLANES
INT_MIN = -(2 ** 31)
HALF_MIN = -(2 ** 15)
NEG_BIG = -1e30

F32 = jnp.float32
BF16 = jnp.bfloat16
I32 = jnp.int32
I16 = jnp.int16


def _dot(a, b):
    return jnp.dot(a, b, preferred_element_type=F32)


def _dot_nt(a, b):
    return lax.dot_general(a, b, (((1,), (1,)), ((), ())), preferred_element_type=F32)


def _rms_rows(x, g):
    ms = jnp.mean(x * x, axis=-1, keepdims=True)
    return x * lax.rsqrt(ms + EPS) * g


ROW_Q = 0
ROW_K = ROW_Q + ATTN_WIDTH
ROW_V = ROW_K + ATTN_WIDTH
ROW_QI = ROW_V + ATTN_WIDTH
ROW_KI = ROW_QI + IDX_HEADS * IDX_DIM
ROW_WI = ROW_KI + LANES
ROWS_T = ROW_WI + 2 * SUBLANES


def _norm_rope_t(z, gain, cos, sin_signed):
    if gain is not None:
        ms = jnp.mean(z * z, axis=0, keepdims=True)
        z = z * lax.rsqrt(ms + EPS) * gain
    half = z.shape[0] // 2
    swapped = jnp.concatenate([z[half:], z[:half]], axis=0)
    return z * cos + swapped * sin_signed


def _proj_kernel(x_ref, g_ref, wu_ref, wt_ref, poolw_ref, pscale_ref, qg_ref, kg_ref, ig_ref,
                 cos_ref, sin_ref,
                 ypool_ref, qt_ref, k_ref, vt_ref, qit_ref, ki_ref, wit_ref,
                 halo_ref, ext_ref, kt_ref, *, idx_scale):
    tm = x_ref.shape[0]
    h = _rms_rows(x_ref[...], g_ref[...]).astype(BF16)
    cos = cos_ref[...]
    sin = sin_ref[...]

    u = _dot(h, wu_ref[...])
    @pl.when(pl.program_id(1) == 0)
    def _():
        halo_ref[...] = jnp.zeros_like(halo_ref)

    ext_ref[0:HALO, :] = halo_ref[...]
    ext_ref[HALO:HALO + tm, :] = u
    halo_ref[...] = u[tm - HALO:, :]
    pos1 = pl.program_id(1) * tm + lax.broadcasted_iota(I32, (tm, 1), 0) + 1
    for g, w in enumerate(POOL_WINDOWS):
        cols = slice(g * POOL_GROUP, (g + 1) * POOL_GROUP)
        win = u[:, cols]
        for j in range(1, w):
            win = win + ext_ref[HALO - j:HALO - j + tm, cols]
        cnt = jnp.minimum(pos1, w).astype(F32)
        mixed = (win / cnt - u[:, cols]).astype(BF16)
        y = _dot(mixed, poolw_ref[g]) * pscale_ref[:, cols]
        ypool_ref[:, cols] = y.astype(ypool_ref.dtype)

    qg = qg_ref[...]
    kg = kg_ref[...]
    zq = _dot_nt(wt_ref[ROW_Q:ROW_K, :], h)
    for hd in range(ATTN_HEADS):
        rows = slice(hd * HEAD_DIM, (hd + 1) * HEAD_DIM)
        qt_ref[rows, :] = (_norm_rope_t(zq[rows], qg, cos, sin) * (HEAD_DIM ** -0.5)).astype(qt_ref.dtype)
    zk = _dot_nt(wt_ref[ROW_K:ROW_V, :], h)
    for hd in range(ATTN_HEADS):
        rows = slice(hd * HEAD_DIM, (hd + 1) * HEAD_DIM)
        kt_ref[rows, :] = _norm_rope_t(zk[rows], kg, cos, sin)
    k_ref[...] = kt_ref[...].T.astype(k_ref.dtype)
    vt_ref[...] = _dot_nt(wt_ref[ROW_V:ROW_QI, :], h).astype(vt_ref.dtype)
    zqi = _dot_nt(wt_ref[ROW_QI:ROW_KI, :], h)
    for hd in range(IDX_HEADS):
        rows = slice(hd * IDX_DIM, (hd + 1) * IDX_DIM)
        qit_ref[rows, :] = _norm_rope_t(zqi[rows], None, cos, sin).astype(qit_ref.dtype)
    zi = _dot_nt(wt_ref[ROW_KI:ROWS_T, :], h)
    kit = _norm_rope_t(zi[0:IDX_DIM], ig_ref[...], cos, sin)
    kit = jnp.concatenate([kit, jnp.zeros((LANES - IDX_DIM, tm), F32)], axis=0)
    ki_ref[...] = kit.T.astype(ki_ref.dtype)
    wit_ref[...] = zi[LANES:LANES + IDX_HEADS] * idx_scale


def _proj_call(x, g, wu, wt, poolw, pscale, qg, kg, ig, cos_t, sin_t, *, tm):
    B, S, D = x.shape
    nt = S // tm
    kernel = functools.partial(_proj_kernel, idx_scale=(IDX_DIM ** -0.5) * (IDX_HEADS ** -0.5))
    full = lambda shape: pl.BlockSpec(shape, lambda b, i: (0,) * len(shape))
    out_shape = (
        jax.ShapeDtypeStruct((B, S, POOL_WIDTH), BF16),
        jax.ShapeDtypeStruct((B, ATTN_WIDTH, S), BF16),
        jax.ShapeDtypeStruct((B, S, ATTN_WIDTH), BF16),
        jax.ShapeDtypeStruct((B, nt, ATTN_WIDTH, tm), BF16),
        jax.ShapeDtypeStruct((B, IDX_HEADS * IDX_DIM, S), BF16),
        jax.ShapeDtypeStruct((B, S, LANES), BF16),
        jax.ShapeDtypeStruct((B, IDX_HEADS, S), F32),
    )
    return pl.pallas_call(
        kernel,
        grid=(B, nt),
        in_specs=[
            pl.BlockSpec((None, tm, D), lambda b, i: (b, i, 0)),
            full(g.shape), full(wu.shape), full(wt.shape), full(poolw.shape), full(pscale.shape),
            full(qg.shape), full(kg.shape), full(ig.shape),
            pl.BlockSpec((HEAD_DIM, tm), lambda b, i: (0, i)),
            pl.BlockSpec((HEAD_DIM, tm), lambda b, i: (0, i)),
        ],
        out_specs=(
            pl.BlockSpec((None, tm, POOL_WIDTH), lambda b, i: (b, i, 0)),
            pl.BlockSpec((None, ATTN_WIDTH, tm), lambda b, i: (b, 0, i)),
            pl.BlockSpec((None, tm, ATTN_WIDTH), lambda b, i: (b, i, 0)),
            pl.BlockSpec((None, None, ATTN_WIDTH, tm), lambda b, i: (b, i, 0, 0)),
            pl.BlockSpec((None, IDX_HEADS * IDX_DIM, tm), lambda b, i: (b, 0, i)),
            pl.BlockSpec((None, tm, LANES), lambda b, i: (b, i, 0)),
            pl.BlockSpec((None, IDX_HEADS, tm), lambda b, i: (b, 0, i)),
        ),
        out_shape=out_shape,
        scratch_shapes=[
            pltpu.VMEM((HALO, POOL_WIDTH), F32),
            pltpu.VMEM((HALO + tm, POOL_WIDTH), F32),
            pltpu.VMEM((ATTN_WIDTH, tm), F32),
        ],
        compiler_params=pltpu.CompilerParams(
            dimension_semantics=("arbitrary", "arbitrary"), vmem_limit_bytes=VMEM_LIMIT_BYTES),
    )(x, g, wu, wt, poolw, pscale, qg, kg, ig, cos_t, sin_t)


def _sortable_key(score):
    bits = lax.bitcast_convert_type(score, I32)
    key = jnp.where(bits < 0, bits ^ jnp.int32(0x7FFFFFFF), bits)
    return jnp.where(key == -1, 0, key)


def _colsum8(x):
    rows, t = x.shape
    return jnp.sum(x.reshape(rows // SUBLANES, SUBLANES, t), axis=0)


def _dsa_kernel(qit_ref, wit_ref, qt_ref, ki_ref, k_ref, vt_ref, out_ref, sc_ref, half_ref, ot_ref, m_ref, l_ref,
                *, topk):
    tq = qit_ref.shape[1]
    nkt_max, tk, _ = sc_ref.shape
    q0 = pl.program_id(1) * tq
    nkt = (q0 + tq + tk - 1) // tk
    qpos = q0 + lax.broadcasted_iota(I32, (1, tq), 1)
    qend = (qpos // CHUNK + 1) * CHUNK
    krow = lax.broadcasted_iota(I32, (tk, 1), 0)
    zeros_half = jnp.zeros((LANES - IDX_DIM, tq), BF16)

    def score_tile(kt, carry):
        ki_t = ki_ref[pl.ds(pl.multiple_of(kt * tk, tk), tk), :]
        acc = jnp.zeros((tk, tq), F32)
        for hd in range(IDX_HEADS):
            qh = jnp.concatenate([qit_ref[hd * IDX_DIM:(hd + 1) * IDX_DIM, :], zeros_half], axis=0)
            rel = jnp.maximum(_dot(ki_t, qh), 0.0)
            acc = acc + rel * wit_ref[hd:hd + 1, :]
        key = jnp.where(kt * tk + krow < qend, _sortable_key(acc), INT_MIN)
        sc_ref[kt] = key
        half_ref[kt] = lax.shift_right_arithmetic(key, 16).astype(I16)
        return carry

    lax.fori_loop(0, nkt, score_tile, 0)

    def count(pred_fn):
        def body(kt, c8):
            return c8 + _colsum8(jnp.where(pred_fn(sc_ref[kt], kt), 1, 0).astype(I32))
        c8 = lax.fori_loop(0, nkt, body, jnp.zeros((SUBLANES, tq), I32))
        return jnp.sum(c8, axis=0, keepdims=True)

    def count_half(pred_fn):
        def body(kt, c16):
            ones = jnp.where(pred_fn(half_ref[kt]), jnp.int16(1), jnp.int16(0))
            for j in range(tk // PACKED_SUBLANES):
                c16 = c16 + ones[j * PACKED_SUBLANES:(j + 1) * PACKED_SUBLANES]
            return c16
        c16 = lax.fori_loop(0, nkt, body, jnp.zeros((PACKED_SUBLANES, tq), I16))
        return jnp.sum(c16.astype(I32), axis=0, keepdims=True)

    def bisect_half():
        def step(i, t):
            cand = t + lax.shift_left(jnp.int32(1), 15 - i)
            cand16 = cand.astype(I16)
            c = count_half(lambda half: half >= cand16)
            return jnp.where(c >= topk, cand, t)
        return lax.fori_loop(0, 16, step, jnp.full((1, tq), HALF_MIN, I32))

    thr_hi = bisect_half()

    def low_tile(kt, carry):
        hi = half_ref[kt].astype(I32)
        lo = (sc_ref[kt] & 0xFFFF) + HALF_MIN
        lo = jnp.where(hi == thr_hi, lo, jnp.where(hi > thr_hi, -HALF_MIN - 1, HALF_MIN))
        half_ref[kt] = lo.astype(I16)
        return carry

    lax.fori_loop(0, nkt, low_tile, 0)
    thr_lo = bisect_half()
    thr = thr_hi * 65536 + (thr_lo - HALF_MIN)

    n_above = count(lambda key, kt: key > thr)
    n_ties = count(lambda key, kt: key == thr)
    need = topk - n_above
    nbits = max(1, (nkt_max * tk - 1).bit_length())

    def tie_step(step, cut):
        cand = cut + lax.shift_left(jnp.int32(1), nbits - 1 - step)
        c = count(lambda key, kt: (key == thr) & (kt * tk + krow < cand))
        return jnp.where(c < need, cand, cut)

    def resolve_ties():
        return lax.fori_loop(0, nbits, tie_step, jnp.zeros((1, tq), I32))

    cut = lax.cond(jnp.max(n_ties - need) > 0, resolve_ties,
                   lambda: jnp.full((1, tq), nkt_max * tk, I32))

    def mask_tile(kt, carry):
        key = sc_ref[kt]
        sel = (key > thr) | ((key == thr) & (kt * tk + krow <= cut))
        sel = sel & (key > INT_MIN)
        sc_ref[kt] = lax.bitcast_convert_type(jnp.where(sel, 0.0, NEG_BIG).astype(F32), I32)
        return carry

    lax.fori_loop(0, nkt, mask_tile, 0)

    zeros_head = jnp.zeros((HEAD_DIM, tq), BF16)
    m_ref[...] = jnp.full(m_ref.shape, NEG_BIG, F32)
    l_ref[...] = jnp.zeros(l_ref.shape, F32)
    ot_ref[...] = jnp.zeros(ot_ref.shape, F32)

    def attn_tile(kt, carry):
        krows = pl.ds(pl.multiple_of(kt * tk, tk), tk)

        for hd in range(ATTN_HEADS):
            pair = hd // 2
            rows = slice(hd * HEAD_DIM, (hd + 1) * HEAD_DIM)
            qh = qt_ref[rows, :]
            qh = jnp.concatenate([qh, zeros_head] if hd % 2 == 0 else [zeros_head, qh], axis=0)
            k2 = k_ref[krows, pair * LANES:(pair + 1) * LANES]
            s = _dot(k2, qh) + lax.bitcast_convert_type(sc_ref[kt], F32)
            m_old = m_ref[hd:hd + 1, :]
            m_new = jnp.maximum(m_old, jnp.max(s, axis=0, keepdims=True))
            alpha = jnp.exp(m_old - m_new)
            p = jnp.exp(s - m_new)
            m_ref[hd:hd + 1, :] = m_new
            l_ref[hd:hd + 1, :] = alpha * l_ref[hd:hd + 1, :] + jnp.sum(p, axis=0, keepdims=True)
            ot_ref[rows, :] = alpha * ot_ref[rows, :] + _dot(vt_ref[kt, rows, :], p.astype(BF16))
        return carry

    lax.fori_loop(0, nkt, attn_tile, 0)
    for hd in range(ATTN_HEADS):
        rows = slice(hd * HEAD_DIM, (hd + 1) * HEAD_DIM)
        ot_ref[rows, :] = ot_ref[rows, :] / l_ref[hd:hd + 1, :]
    out_ref[...] = ot_ref[...].T.astype(out_ref.dtype)


def _dsa_call(qit, wit, qt, ki, k, vt, *, tq, topk):
    B, S, _ = k.shape
    _, nkt, _, tk = vt.shape
    kernel = functools.partial(_dsa_kernel, topk=topk)
    return pl.pallas_call(
        kernel,
        grid=(B, S // tq),
        in_specs=[
            pl.BlockSpec((None, IDX_HEADS * IDX_DIM, tq), lambda b, i: (b, 0, i)),
            pl.BlockSpec((None, IDX_HEADS, tq), lambda b, i: (b, 0, i)),
            pl.BlockSpec((None, ATTN_WIDTH, tq), lambda b, i: (b, 0, i)),
            pl.BlockSpec((None, S, LANES), lambda b, i: (b, 0, 0)),
            pl.BlockSpec((None, S, ATTN_WIDTH), lambda b, i: (b, 0, 0)),
            pl.BlockSpec((None, nkt, ATTN_WIDTH, tk), lambda b, i: (b, 0, 0, 0)),
        ],
        out_specs=pl.BlockSpec((None, tq, ATTN_WIDTH), lambda b, i: (b, i, 0)),
        out_shape=jax.ShapeDtypeStruct((B, S, ATTN_WIDTH), BF16),
        scratch_shapes=[
            pltpu.VMEM((nkt, tk, tq), I32),
            pltpu.VMEM((nkt, tk, tq), I16),
            pltpu.VMEM((ATTN_WIDTH, tq), F32),
            pltpu.VMEM((ATTN_HEADS, tq), F32),
            pltpu.VMEM((ATTN_HEADS, tq), F32),
        ],
        compiler_params=pltpu.CompilerParams(
            dimension_semantics=("arbitrary", "arbitrary"), vmem_limit_bytes=VMEM_LIMIT_BYTES),
    )(qit, wit, qt, ki, k, vt)


def _memkv_kernel(mem_ref, g_ref, wkv_ref, kn_ref, k_ref, v_ref):
    m = _rms_rows(mem_ref[...], g_ref[...]).astype(BF16)
    kv = _dot(m, wkv_ref[...])
    for hd in range(MEM_HEADS):
        cols = slice(hd * MEM_HEAD_DIM, (hd + 1) * MEM_HEAD_DIM)
        k_ref[:, cols] = _rms_rows(kv[:, cols], kn_ref[...]).astype(k_ref.dtype)
    v_ref[...] = kv[:, MEM_WIDTH:].astype(v_ref.dtype)


def _memkv_call(mem, g, wkv, kn):
    B, M, D = mem.shape
    full = lambda shape: pl.BlockSpec(shape, lambda b: (0,) * len(shape))
    return pl.pallas_call(
        _memkv_kernel,
        grid=(B,),
        in_specs=[pl.BlockSpec((None, M, D), lambda b: (b, 0, 0)), full(g.shape), full(wkv.shape), full(kn.shape)],
        out_specs=(pl.BlockSpec((None, M, MEM_WIDTH), lambda b: (b, 0, 0)),
                   pl.BlockSpec((None, M, MEM_WIDTH), lambda b: (b, 0, 0))),
        out_shape=(jax.ShapeDtypeStruct((B, M, MEM_WIDTH), BF16), jax.ShapeDtypeStruct((B, M, MEM_WIDTH), BF16)),
        compiler_params=pltpu.CompilerParams(dimension_semantics=("arbitrary",), vmem_limit_bytes=VMEM_LIMIT_BYTES),
    )(mem, g, wkv, kn)


def _split_bf16(a):
    hi = a.astype(BF16)
    lo = (a - hi.astype(F32)).astype(BF16)
    return hi, lo


def _mid_kernel(x_ref, yp_ref, ya_ref, wo1_ref, wo2_ref, gx_ref, wq_ref, qn_ref, km_ref, vm_ref, wo_ref,
                gf_ref, wr_ref, br_ref, rows_ref, gsel_ref):
    tm, d = x_ref.shape
    x1 = x_ref[...] + _dot(yp_ref[...], wo1_ref[...]) + _dot(ya_ref[...], wo2_ref[...])

    h = _rms_rows(x1, gx_ref[...]).astype(BF16)
    q = _dot(h, wq_ref[...])
    heads = []
    for hd in range(MEM_HEADS):
        cols = slice(hd * MEM_HEAD_DIM, (hd + 1) * MEM_HEAD_DIM)
        qh = (_rms_rows(q[:, cols], qn_ref[...]) * (MEM_HEAD_DIM ** -0.5)).astype(BF16)
        s = _dot_nt(qh, km_ref[:, cols])
        p = jnp.exp(s - jnp.max(s, axis=-1, keepdims=True))
        p = p / jnp.sum(p, axis=-1, keepdims=True)
        heads.append(_dot(p.astype(BF16), vm_ref[:, cols]))
    o = jnp.concatenate(heads, axis=-1).astype(BF16)
    x2 = x1 + _dot(o, wo_ref[...])
    rows_ref[:, :d] = x2

    h_hi, h_lo = _split_bf16(_rms_rows(x2, gf_ref[...]))
    w_hi, w_lo = _split_bf16(wr_ref[...])
    logits = _dot(h_hi, w_hi) + (_dot(h_hi, w_lo) + _dot(h_lo, w_hi)) + br_ref[...]
    lane = lax.broadcasted_iota(I32, (tm, LANES), 1).astype(F32)
    neg_inf = -jnp.inf
    g_logit = jnp.where(lane < N_GROUPS, logits, neg_inf)
    g_max = jnp.max(g_logit, axis=-1, keepdims=True)
    g_sel = jnp.min(jnp.where(g_logit == g_max, lane, LANES), axis=-1, keepdims=True)
    g_w = 1.0 / jnp.sum(jnp.exp(g_logit - g_max), axis=-1, keepdims=True)
    e_lo = N_GROUPS + g_sel * EXPERTS_PER_GROUP
    in_group = (lane >= e_lo) & (lane < e_lo + EXPERTS_PER_GROUP)
    e_logit = jnp.where(in_group, logits, neg_inf)
    v1 = jnp.max(e_logit, axis=-1, keepdims=True)
    i1 = jnp.min(jnp.where(e_logit == v1, lane, LANES), axis=-1, keepdims=True)
    rest = jnp.where(lane == i1, neg_inf, e_logit)
    v2 = jnp.max(rest, axis=-1, keepdims=True)
    i2 = jnp.min(jnp.where(rest == v2, lane, LANES), axis=-1, keepdims=True)
    e2 = jnp.exp(v2 - v1)
    w1 = g_w / (1.0 + e2)
    w2 = g_w * e2 / (1.0 + e2)
    gates = jnp.where(lane == i1 - N_GROUPS, w1, 0.0) + jnp.where(lane == i2 - N_GROUPS, w2, 0.0)
    rows_ref[:, d:] = gates
    gsel_ref[...] = jnp.broadcast_to(g_sel, (tm, LANES)).T[0:1, :]


def _mid_call(x, ypool, yattn, wo1, wo2, gx, wq, qn, kmem, vmem, wo, gf, wr, br, *, tm):
    B, S, D = x.shape
    M = kmem.shape[1]
    nt = S // tm
    full = lambda shape: pl.BlockSpec(shape, lambda b, i: (0,) * len(shape))
    tile = lambda width: pl.BlockSpec((None, tm, width), lambda b, i: (b, i, 0))
    return pl.pallas_call(
        _mid_kernel,
        grid=(B, nt),
        in_specs=[tile(D), tile(POOL_WIDTH), tile(ATTN_WIDTH), full(wo1.shape), full(wo2.shape), full(gx.shape),
                  full(wq.shape), full(qn.shape),
                  pl.BlockSpec((None, M, MEM_WIDTH), lambda b, i: (b, 0, 0)),
                  pl.BlockSpec((None, M, MEM_WIDTH), lambda b, i: (b, 0, 0)),
                  full(wo.shape), full(gf.shape), full(wr.shape), full(br.shape)],
        out_specs=(tile(D + LANES), pl.BlockSpec((None, 1, tm), lambda b, i: (b * nt + i, 0, 0))),
        out_shape=(jax.ShapeDtypeStruct((B, S, D + LANES), F32),
                   jax.ShapeDtypeStruct((B * nt, 1, tm), F32)),
        compiler_params=pltpu.CompilerParams(
            dimension_semantics=("arbitrary", "arbitrary"), vmem_limit_bytes=VMEM_LIMIT_BYTES),
    )(x, ypool, yattn, wo1, wo2, gx, wq, qn, kmem, vmem, wo, gf, wr, br)


META_NUSED = 64
ROW_DMA_CHUNK = 256


def _route_kernel(gsel_ref, pos_ref, meta_ref, *, tr):
    nb, w = gsel_ref.shape
    gsel = gsel_ref[...]
    lane = lax.broadcasted_iota(I32, (1, LANES), 1)
    before = (lax.broadcasted_iota(I32, (w, w), 0) < lax.broadcasted_iota(I32, (w, w), 1)).astype(BF16)
    onehot = [(gsel == g).astype(F32) for g in range(N_GROUPS)]
    cnt = sum(jnp.where(lane == g, jnp.sum(onehot[g], axis=1, keepdims=True), 0.0) for g in range(N_GROUPS))
    run = jnp.zeros((1, LANES), F32)
    carries = []
    for b in range(nb):
        carries.append(run)
        run = run + cnt[b:b + 1]
    carry = jnp.concatenate(carries, axis=0)
    shift = tr.bit_length() - 1
    padded = lax.shift_left(lax.shift_right_logical(run.astype(I32) + (tr - 1), shift), shift)
    size = [jnp.sum(jnp.where(lane == g, padded, 0), axis=1, keepdims=True) for g in range(N_GROUPS)]
    start = [sum(size[:g], jnp.zeros((1, 1), I32)) for g in range(N_GROUPS)]
    total = sum(size, jnp.zeros((1, 1), I32))
    pos = jnp.zeros((nb, w), F32)
    for g in range(N_GROUPS):
        rank = _dot(onehot[g].astype(BF16), before)
        base = jnp.sum(jnp.where(lane == g, carry, 0.0), axis=1, keepdims=True) + start[g].astype(F32)
        pos = pos + onehot[g] * (rank + base)
    pos_ref[...] = pos.astype(I32)
    tile_row = lane * tr
    tile_group = jnp.zeros((1, LANES), I32)
    last_group = jnp.zeros((1, 1), I32)
    for g in range(N_GROUPS):
        tile_group = tile_group + jnp.where((tile_row >= start[g]) & (tile_row < start[g] + size[g]), g, 0)
        last_group = jnp.where(size[g] > 0, g, last_group)
    tile_group = jnp.where(tile_row < total, tile_group, last_group)
    meta_ref[...] = jnp.where(lane == META_NUSED, lax.shift_right_logical(total, shift), tile_group)


def _route_call(gsel, *, tr):
    nb, w = gsel.shape
    return pl.pallas_call(
        functools.partial(_route_kernel, tr=tr),
        out_shape=(jax.ShapeDtypeStruct((nb, w), I32), jax.ShapeDtypeStruct((1, LANES), I32)),
        compiler_params=pltpu.CompilerParams(vmem_limit_bytes=VMEM_LIMIT_BYTES),
    )(gsel)


def _permute_rows_kernel(pos_ref, src_ref, *rest, scatter):
    dst_ref, sem = rest[-2], rest[-1]
    nchunks = pos_ref.shape[0] // ROW_DMA_CHUNK

    def row_copy(t, slot):
        p = pos_ref[t]
        s, d = (t, p) if scatter else (p, t)
        return pltpu.make_async_copy(src_ref.at[pl.ds(s, 1)], dst_ref.at[pl.ds(d, 1)], sem.at[slot])

    def issue(c):
        def body(r, carry):
            row_copy(c * ROW_DMA_CHUNK + r, c % 2).start()
            return carry
        lax.fori_loop(0, ROW_DMA_CHUNK, body, 0)

    def drain(c):
        rows = pl.ds(0, ROW_DMA_CHUNK)
        pltpu.make_async_copy(src_ref.at[rows], dst_ref.at[rows], sem.at[c % 2]).wait()

    issue(0)

    def step(c, carry):
        issue(c)
        drain(c - 1)
        return carry

    lax.fori_loop(1, nchunks, step, 0)
    drain(nchunks - 1)


def _permute_rows_call(pos, src, *, n_out, scatter):
    width = src.shape[1]
    any_spec = pl.BlockSpec(memory_space=pl.ANY)
    operands = (pos, src) + ((jnp.zeros((n_out, width), src.dtype),) if scatter else ())
    return pl.pallas_call(
        functools.partial(_permute_rows_kernel, scatter=scatter),
        grid_spec=pltpu.PrefetchScalarGridSpec(
            num_scalar_prefetch=1, grid=(1,),
            in_specs=[any_spec] * (len(operands) - 1), out_specs=any_spec,
            scratch_shapes=[pltpu.SemaphoreType.DMA((2,))]),
        out_shape=jax.ShapeDtypeStruct((n_out, width), src.dtype),
        input_output_aliases={2: 0} if scatter else {},
        compiler_params=pltpu.CompilerParams(dimension_semantics=("arbitrary",), has_side_effects=True),
    )(*operands)


def _expert_kernel(meta_ref, rows_ref, gf_ref, wg_ref, wu_ref, wd_ref, out_ref):
    j = pl.program_id(0)
    tr, d = out_ref.shape
    n_exp, _, ff = wg_ref.shape

    @pl.when(j < meta_ref[META_NUSED])
    def _():
        x2 = rows_ref[:, :d]
        gates = rows_ref[:, d:]
        h = _rms_rows(x2, gf_ref[...]).astype(BF16)
        lane = lax.broadcasted_iota(I32, (tr, LANES), 1)
        first = meta_ref[j] * n_exp
        hid = []
        for e in range(n_exp):
            a = _dot(h, wg_ref[e])
            b = _dot(h, wu_ref[e])
            gate = jnp.sum(jnp.where(lane == first + e, gates, 0.0), axis=-1, keepdims=True)
            hid.append((a * jax.nn.sigmoid(a) * b * gate).astype(BF16))
        hid = jnp.concatenate(hid, axis=-1)
        out_ref[...] = x2 + _dot(hid, wd_ref[...].reshape(n_exp * ff, d))

    @pl.when(j >= meta_ref[META_NUSED])
    def _():
        out_ref[...] = jnp.zeros_like(out_ref)


def _expert_call(meta, rows, gf, wg, wu, wd, *, tr):
    n_rows, width = rows.shape
    d = width - LANES
    _, n_exp, _, ff = wg.shape
    by_group = lambda shape: pl.BlockSpec((None,) + shape, lambda j, meta: (meta[j], 0, 0, 0))
    return pl.pallas_call(
        _expert_kernel,
        grid_spec=pltpu.PrefetchScalarGridSpec(
            num_scalar_prefetch=1, grid=(n_rows // tr,),
            in_specs=[pl.BlockSpec((tr, width), lambda j, meta: (j, 0)),
                      pl.BlockSpec(gf.shape, lambda j, meta: (0, 0)),
                      by_group((n_exp, d, ff)), by_group((n_exp, d, ff)), by_group((n_exp, ff, d))],
            out_specs=pl.BlockSpec((tr, d), lambda j, meta: (j, 0))),
        out_shape=jax.ShapeDtypeStruct((n_rows, d), F32),
        compiler_params=pltpu.CompilerParams(dimension_semantics=("arbitrary",), vmem_limit_bytes=VMEM_LIMIT_BYTES),
    )(meta, rows, gf, wg, wu, wd)


def _rope_tables_t(seq_len, dim):
    inv = ROPE_THETA ** (-jnp.arange(0, dim, 2, dtype=F32) / dim)
    ang = jnp.arange(seq_len, dtype=F32)[:, None] * inv[None, :]
    ang = jnp.concatenate([ang, ang], axis=-1)
    sign = jnp.concatenate([-jnp.ones((dim // 2,), F32), jnp.ones((dim // 2,), F32)])
    return jnp.cos(ang).T, (jnp.sin(ang) * sign[None, :]).T


def _layer(x, mem, mix_norm, w_in, pool_w, pool_scale, q_norm, k_norm, idx_k_norm, w_out,
           xattn_norm, mem_norm, xattn_wq, xattn_wkv, xattn_q_norm, xattn_k_norm, xattn_wo,
           ffn_norm, router_group_w, router_group_b, router_expert_w, router_expert_b,
           expert_w_gate, expert_w_up, expert_w_down, *, tm, tq, tr):
    B, S, D = x.shape
    topk = min(TOPK_MAX, S // 4)
    row = lambda v: v.reshape(1, -1).astype(F32)
    col = lambda v: jnp.broadcast_to(v.astype(F32)[:, None], (v.shape[0], tm))

    wu = w_in[:, :POOL_WIDTH].astype(BF16)
    pad = lambda n: jnp.zeros((n, D), F32)
    wt = jnp.concatenate([
        w_in[:, OFF_Q:OFF_KI].T, w_in[:, OFF_KI:OFF_WI].T, pad(LANES - IDX_DIM),
        w_in[:, OFF_WI:IN_COLS].T, pad(2 * SUBLANES - IDX_HEADS)], axis=0).astype(BF16)
    cos_t, sin_t = _rope_tables_t(S, HEAD_DIM)

    ypool, qt, k, vt, qit, ki, wit = _proj_call(
        x, row(mix_norm), wu, wt, pool_w.astype(BF16), row(pool_scale), col(q_norm), col(k_norm),
        col(idx_k_norm), cos_t, sin_t, tm=tm)
    yattn = _dsa_call(qit, wit, qt, ki, k, vt, tq=tq, topk=topk)
    kmem, vmem = _memkv_call(mem, row(mem_norm), xattn_wkv.astype(BF16), row(xattn_k_norm))

    n_logits = N_GROUPS + N_EXPERTS
    wr = jnp.concatenate([router_group_w, router_expert_w, jnp.zeros((D, LANES - n_logits), F32)], axis=1)
    br = jnp.concatenate([router_group_b, router_expert_b, jnp.zeros((LANES - n_logits,), F32)]).reshape(1, LANES)
    rows, gsel = _mid_call(
        x, ypool, yattn, w_out[:POOL_WIDTH].astype(BF16), w_out[POOL_WIDTH:].astype(BF16), row(xattn_norm),
        xattn_wq.astype(BF16), row(xattn_q_norm), kmem, vmem, xattn_wo.astype(BF16), row(ffn_norm), wr, br,
        tm=tm)

    n_tok = B * S
    n_sorted = n_tok + N_GROUPS * tr
    pos, meta = _route_call(gsel.reshape(n_tok // tm, tm), tr=tr)
    pos = pos.reshape(n_tok)
    sorted_rows = _permute_rows_call(pos, rows.reshape(n_tok, D + LANES), n_out=n_sorted, scatter=True)
    sorted_out = _expert_call(
        meta.reshape(LANES), sorted_rows, row(ffn_norm), expert_w_gate.astype(BF16), expert_w_up.astype(BF16),
        expert_w_down.astype(BF16), tr=tr)
    out = _permute_rows_call(pos, sorted_out, n_out=n_tok, scatter=False)
    return out.reshape(B, S, D)


def kernel(x, mem, mix_norm, w_in, pool_w, pool_scale, q_norm, k_norm, idx_k_norm, w_out, xattn_norm, mem_norm,
           xattn_wq, xattn_wkv, xattn_q_norm, xattn_k_norm, xattn_wo, ffn_norm, router_group_w, router_group_b,
           router_expert_w, router_expert_b, expert_w_gate, expert_w_up, expert_w_down):
    depth = mix_norm.shape[0]
    for l in range(depth):
        x = _layer(
            x, mem, mix_norm[l], w_in[l], pool_w[l], pool_scale[l], q_norm[l], k_norm[l], idx_k_norm[l], w_out[l],
            xattn_norm[l], mem_norm[l], xattn_wq[l], xattn_wkv[l], xattn_q_norm[l], xattn_k_norm[l], xattn_wo[l],
            ffn_norm[l], router_group_w[l], router_group_b[l], router_expert_w[l], router_expert_b[l],
            expert_w_gate[l], expert_w_up[l], expert_w_down[l], tm=512, tq=512, tr=512)
    return x
```

```python
import functools
import math

import jax
import jax.numpy as jnp
from jax import lax
from jax.experimental import pallas as pl
from jax.experimental.pallas import tpu as pltpu

CHUNK = 64
POOL_WINDOWS = (2, 4, 8, 16)
POOL_GROUP = 128
POOL_WIDTH = POOL_GROUP * len(POOL_WINDOWS)
ATTN_HEADS = 8
HEAD_DIM = 64
ATTN_WIDTH = ATTN_HEADS * HEAD_DIM
IDX_HEADS = 8
IDX_DIM = 64
TOPK_MAX = 256
ROPE_THETA = 10000.0
MEM_HEADS = 4
MEM_HEAD_DIM = 128
MEM_WIDTH = MEM_HEADS * MEM_HEAD_DIM
N_GROUPS = 4
EXPERTS_PER_GROUP = 8
N_EXPERTS = N_GROUPS * EXPERTS_PER_GROUP
EPS = 1e-6
OFF_Q = POOL_WIDTH
OFF_K = OFF_Q + ATTN_WIDTH
OFF_V = OFF_K + ATTN_WIDTH
OFF_QI = OFF_V + ATTN_WIDTH
OFF_KI = OFF_QI + IDX_HEADS * IDX_DIM
OFF_WI = OFF_KI + IDX_DIM
IN_COLS = OFF_WI + IDX_HEADS

LANES = 128
SUBLANES = 8
VMEM_LIMIT_BYTES = 56 * 1024 * 1024

HALO = 16
PACKED_SUBLANES = 2 * SUBLANES
INT_MIN = -(2 ** 31)
HALF_MIN = -(2 ** 15)
NEG_BIG = -1e30

F32 = jnp.float32
BF16 = jnp.bfloat16
I32 = jnp.int32
I16 = jnp.int16


def _dot(a, b):
    return jnp.dot(a, b, preferred_element_type=F32)


def _dot_nt(a, b):
    return lax.dot_general(a, b, (((1,), (1,)), ((), ())), preferred_element_type=F32)


def _rms_rows(x, g):
    ms = jnp.mean(x * x, axis=-1, keepdims=True)
    return x * lax.rsqrt(ms + EPS) * g


ROW_Q = 0
ROW_K = ROW_Q + ATTN_WIDTH
ROW_V = ROW_K + ATTN_WIDTH
ROW_QI = ROW_V + ATTN_WIDTH
ROW_KI = ROW_QI + IDX_HEADS * IDX_DIM
ROW_WI = ROW_KI + LANES
ROWS_T = ROW_WI + 2 * SUBLANES


def _norm_rope_t(z, gain, cos, sin_signed):
    if gain is not None:
        ms = jnp.mean(z * z, axis=0, keepdims=True)
        z = z * lax.rsqrt(ms + EPS) * gain
    half = z.shape[0] // 2
    swapped = jnp.concatenate([z[half:], z[:half]], axis=0)
    return z * cos + swapped * sin_signed


def _proj_kernel(x_ref, g_ref, wu_ref, wt_ref, poolw_ref, pscale_ref, qg_ref, kg_ref, ig_ref,
                 cos_ref, sin_ref,
                 ypool_ref, qt_ref, k_ref, vt_ref, qit_ref, ki_ref, wit_ref,
                 halo_ref, ext_ref, kt_ref, *, idx_scale):
    tm = x_ref.shape[0]
    h = _rms_rows(x_ref[...], g_ref[...]).astype(BF16)
    cos = cos_ref[...]
    sin = sin_ref[...]

    u = _dot(h, wu_ref[...])
    @pl.when(pl.program_id(1) == 0)
    def _():
        halo_ref[...] = jnp.zeros_like(halo_ref)

    ext_ref[0:HALO, :] = halo_ref[...]
    ext_ref[HALO:HALO + tm, :] = u
    halo_ref[...] = u[tm - HALO:, :]
    pos1 = pl.program_id(1) * tm + lax.broadcasted_iota(I32, (tm, 1), 0) + 1
    for g, w in enumerate(POOL_WINDOWS):
        cols = slice(g * POOL_GROUP, (g + 1) * POOL_GROUP)
        win = u[:, cols]
        for j in range(1, w):
            win = win + ext_ref[HALO - j:HALO - j + tm, cols]
        cnt = jnp.minimum(pos1, w).astype(F32)
        mixed = (win / cnt - u[:, cols]).astype(BF16)
        y = _dot(mixed, poolw_ref[g]) * pscale_ref[:, cols]
        ypool_ref[:, cols] = y.astype(ypool_ref.dtype)

    qg = qg_ref[...]
    kg = kg_ref[...]
    zq = _dot_nt(wt_ref[ROW_Q:ROW_K, :], h)
    for hd in range(ATTN_HEADS):
        rows = slice(hd * HEAD_DIM, (hd + 1) * HEAD_DIM)
        qt_ref[rows, :] = (_norm_rope_t(zq[rows], qg, cos, sin) * (HEAD_DIM ** -0.5)).astype(qt_ref.dtype)
    zk = _dot_nt(wt_ref[ROW_K:ROW_V, :], h)
    for hd in range(ATTN_HEADS):
        rows = slice(hd * HEAD_DIM, (hd + 1) * HEAD_DIM)
        kt_ref[rows, :] = _norm_rope_t(zk[rows], kg, cos, sin)
    k_ref[...] = kt_ref[...].T.astype(k_ref.dtype)
    vt_ref[...] = _dot_nt(wt_ref[ROW_V:ROW_QI, :], h).astype(vt_ref.dtype)
    zqi = _dot_nt(wt_ref[ROW_QI:ROW_KI, :], h)
    for hd in range(IDX_HEADS):
        rows = slice(hd * IDX_DIM, (hd + 1) * IDX_DIM)
        qit_ref[rows, :] = _norm_rope_t(zqi[rows], None, cos, sin).astype(qit_ref.dtype)
    zi = _dot_nt(wt_ref[ROW_KI:ROWS_T, :], h)
    kit = _norm_rope_t(zi[0:IDX_DIM], ig_ref[...], cos, sin)
    kit = jnp.concatenate([kit, jnp.zeros((LANES - IDX_DIM, tm), F32)], axis=0)
    ki_ref[...] = kit.T.astype(ki_ref.dtype)
    wit_ref[...] = zi[LANES:LANES + IDX_HEADS] * idx_scale


def _proj_call(x, g, wu, wt, poolw, pscale, qg, kg, ig, cos_t, sin_t, *, tm):
    B, S, D = x.shape
    nt = S // tm
    kernel = functools.partial(_proj_kernel, idx_scale=(IDX_DIM ** -0.5) * (IDX_HEADS ** -0.5))
    full = lambda shape: pl.BlockSpec(shape, lambda b, i: (0,) * len(shape))
    out_shape = (
        jax.ShapeDtypeStruct((B, S, POOL_WIDTH), BF16),
        jax.ShapeDtypeStruct((B, ATTN_WIDTH, S), BF16),
        jax.ShapeDtypeStruct((B, S, ATTN_WIDTH), BF16),
        jax.ShapeDtypeStruct((B, nt, ATTN_WIDTH, tm), BF16),
        jax.ShapeDtypeStruct((B, IDX_HEADS * IDX_DIM, S), BF16),
        jax.ShapeDtypeStruct((B, S, LANES), BF16),
        jax.ShapeDtypeStruct((B, IDX_HEADS, S), F32),
    )
    return pl.pallas_call(
        kernel,
        grid=(B, nt),
        in_specs=[
            pl.BlockSpec((None, tm, D), lambda b, i: (b, i, 0)),
            full(g.shape), full(wu.shape), full(wt.shape), full(poolw.shape), full(pscale.shape),
            full(qg.shape), full(kg.shape), full(ig.shape),
            pl.BlockSpec((HEAD_DIM, tm), lambda b, i: (0, i)),
            pl.BlockSpec((HEAD_DIM, tm), lambda b, i: (0, i)),
        ],
        out_specs=(
            pl.BlockSpec((None, tm, POOL_WIDTH), lambda b, i: (b, i, 0)),
            pl.BlockSpec((None, ATTN_WIDTH, tm), lambda b, i: (b, 0, i)),
            pl.BlockSpec((None, tm, ATTN_WIDTH), lambda b, i: (b, i, 0)),
            pl.BlockSpec((None, None, ATTN_WIDTH, tm), lambda b, i: (b, i, 0, 0)),
            pl.BlockSpec((None, IDX_HEADS * IDX_DIM, tm), lambda b, i: (b, 0, i)),
            pl.BlockSpec((None, tm, LANES), lambda b, i: (b, i, 0)),
            pl.BlockSpec((None, IDX_HEADS, tm), lambda b, i: (b, 0, i)),
        ),
        out_shape=out_shape,
        scratch_shapes=[
            pltpu.VMEM((HALO, POOL_WIDTH), F32),
            pltpu.VMEM((HALO + tm, POOL_WIDTH), F32),
            pltpu.VMEM((ATTN_WIDTH, tm), F32),
        ],
        compiler_params=pltpu.CompilerParams(
            dimension_semantics=("arbitrary", "arbitrary"), vmem_limit_bytes=VMEM_LIMIT_BYTES),
    )(x, g, wu, wt, poolw, pscale, qg, kg, ig, cos_t, sin_t)


def _sortable_key(score):
    bits = lax.bitcast_convert_type(score, I32)
    key = jnp.where(bits < 0, bits ^ jnp.int32(0x7FFFFFFF), bits)
    return jnp.where(key == -1, 0, key)


def _colsum8(x):
    rows, t = x.shape
    return jnp.sum(x.reshape(rows // SUBLANES, SUBLANES, t), axis=0)


def _dsa_kernel(qit_ref, wit_ref, qt_ref, ki_ref, k_ref, vt_ref, out_ref, sc_ref, half_ref, ot_ref, m_ref, l_ref,
                *, topk):
    tq = qit_ref.shape[1]
    nkt_max, tk, _ = sc_ref.shape
    q0 = pl.program_id(1) * tq
    nkt = (q0 + tq + tk - 1) // tk
    qpos = q0 + lax.broadcasted_iota(I32, (1, tq), 1)
    qend = (qpos // CHUNK + 1) * CHUNK
    krow = lax.broadcasted_iota(I32, (tk, 1), 0)
    zeros_half = jnp.zeros((LANES - IDX_DIM, tq), BF16)

    def score_tile(kt, carry):
        ki_t = ki_ref[pl.ds(pl.multiple_of(kt * tk, tk), tk), :]
        acc = jnp.zeros((tk, tq), F32)
        for hd in range(IDX_HEADS):
            qh = jnp.concatenate([qit_ref[hd * IDX_DIM:(hd + 1) * IDX_DIM, :], zeros_half], axis=0)
            rel = jnp.maximum(_dot(ki_t, qh), 0.0)
            acc = acc + rel * wit_ref[hd:hd + 1, :]
        key = jnp.where(kt * tk + krow < qend, _sortable_key(acc), INT_MIN)
        sc_ref[kt] = key
        half_ref[kt] = lax.shift_right_arithmetic(key, 16).astype(I16)
        return carry

    lax.fori_loop(0, nkt, score_tile, 0)

    def count(pred_fn):
        def body(kt, c8):
            return c8 + _colsum8(jnp.where(pred_fn(sc_ref[kt], kt), 1, 0).astype(I32))
        c8 = lax.fori_loop(0, nkt, body, jnp.zeros((SUBLANES, tq), I32))
        return jnp.sum(c8, axis=0, keepdims=True)

    def count_half(pred_fn):
        def body(kt, c16):
            ones = jnp.where(pred_fn(half_ref[kt]), jnp.int16(1), jnp.int16(0))
            for j in range(tk // PACKED_SUBLANES):
                c16 = c16 + ones[j * PACKED_SUBLANES:(j + 1) * PACKED_SUBLANES]
            return c16
        c16 = lax.fori_loop(0, nkt, body, jnp.zeros((PACKED_SUBLANES, tq), I16))
        return jnp.sum(c16.astype(I32), axis=0, keepdims=True)

    def bisect_half():
        def step(i, t):
            cand = t + lax.shift_left(jnp.int32(1), 15 - i)
            cand16 = cand.astype(I16)
            c = count_half(lambda half: half >= cand16)
            return jnp.where(c >= topk, cand, t)
        return lax.fori_loop(0, 16, step, jnp.full((1, tq), HALF_MIN, I32))

    thr_hi = bisect_half()

    def low_tile(kt, carry):
        hi = half_ref[kt].astype(I32)
        lo = (sc_ref[kt] & 0xFFFF) + HALF_MIN
        lo = jnp.where(hi == thr_hi, lo, jnp.where(hi > thr_hi, -HALF_MIN - 1, HALF_MIN))
        half_ref[kt] = lo.astype(I16)
        return carry

    lax.fori_loop(0, nkt, low_tile, 0)
    thr_lo = bisect_half()
    thr = thr_hi * 65536 + (thr_lo - HALF_MIN)

    n_above = count(lambda key, kt: key > thr)
    n_ties = count(lambda key, kt: key == thr)
    need = topk - n_above
    nbits = max(1, (nkt_max * tk - 1).bit_length())

    def tie_step(step, cut):
        cand = cut + lax.shift_left(jnp.int32(1), nbits - 1 - step)
        c = count(lambda key, kt: (key == thr) & (kt * tk + krow < cand))
        return jnp.where(c < need, cand, cut)

    def resolve_ties():
        return lax.fori_loop(0, nbits, tie_step, jnp.zeros((1, tq), I32))

    cut = lax.cond(jnp.max(n_ties - need) > 0, resolve_ties,
                   lambda: jnp.full((1, tq), nkt_max * tk, I32))

    def mask_tile(kt, carry):
        key = sc_ref[kt]
        sel = (key > thr) | ((key == thr) & (kt * tk + krow <= cut))
        sel = sel & (key > INT_MIN)
        sc_ref[kt] = lax.bitcast_convert_type(jnp.where(sel, 0.0, NEG_BIG).astype(F32), I32)
        return carry

    lax.fori_loop(0, nkt, mask_tile, 0)

    zeros_head = jnp.zeros((HEAD_DIM, tq), BF16)
    m_ref[...] = jnp.full(m_ref.shape, NEG_BIG, F32)
    l_ref[...] = jnp.zeros(l_ref.shape, F32)
    ot_ref[...] = jnp.zeros(ot_ref.shape, F32)

    def attn_tile(kt, carry):
        krows = pl.ds(pl.multiple_of(kt * tk, tk), tk)

        for hd in range(ATTN_HEADS):
            pair = hd // 2
            rows = slice(hd * HEAD_DIM, (hd + 1) * HEAD_DIM)
            qh = qt_ref[rows, :]
            qh = jnp.concatenate([qh, zeros_head] if hd % 2 == 0 else [zeros_head, qh], axis=0)
            k2 = k_ref[krows, pair * LANES:(pair + 1) * LANES]
            s = _dot(k2, qh) + lax.bitcast_convert_type(sc_ref[kt], F32)
            m_old = m_ref[hd:hd + 1, :]
            m_new = jnp.maximum(m_old, jnp.max(s, axis=0, keepdims=True))
            alpha = jnp.exp(m_old - m_new)
            p = jnp.exp(s - m_new)
            m_ref[hd:hd + 1, :] = m_new
            l_ref[hd:hd + 1, :] = alpha * l_ref[hd:hd + 1, :] + jnp.sum(p, axis=0, keepdims=True)
            ot_ref[rows, :] = alpha * ot_ref[rows, :] + _dot(vt_ref[kt, rows, :], p.astype(BF16))
        return carry

    lax.fori_loop(0, nkt, attn_tile, 0)
    for hd in range(ATTN_HEADS):
        rows = slice(hd * HEAD_DIM, (hd + 1) * HEAD_DIM)
        ot_ref[rows, :] = ot_ref[rows, :] / l_ref[hd:hd + 1, :]
    out_ref[...] = ot_ref[...].T.astype(out_ref.dtype)


def _dsa_call(qit, wit, qt, ki, k, vt, *, tq, topk):
    B, S, _ = k.shape
    _, nkt, _, tk = vt.shape
    kernel = functools.partial(_dsa_kernel, topk=topk)
    return pl.pallas_call(
        kernel,
        grid=(B, S // tq),
        in_specs=[
            pl.BlockSpec((None, IDX_HEADS * IDX_DIM, tq), lambda b, i: (b, 0, i)),
            pl.BlockSpec((None, IDX_HEADS, tq), lambda b, i: (b, 0, i)),
            pl.BlockSpec((None, ATTN_WIDTH, tq), lambda b, i: (b, 0, i)),
            pl.BlockSpec((None, S, LANES), lambda b, i: (b, 0, 0)),
            pl.BlockSpec((None, S, ATTN_WIDTH), lambda b, i: (b, 0, 0)),
            pl.BlockSpec((None, nkt, ATTN_WIDTH, tk), lambda b, i: (b, 0, 0, 0)),
        ],
        out_specs=pl.BlockSpec((None, tq, ATTN_WIDTH), lambda b, i: (b, i, 0)),
        out_shape=jax.ShapeDtypeStruct((B, S, ATTN_WIDTH), BF16),
        scratch_shapes=[
            pltpu.VMEM((nkt, tk, tq), I32),
            pltpu.VMEM((nkt, tk, tq), I16),
            pltpu.VMEM((ATTN_WIDTH, tq), F32),
            pltpu.VMEM((ATTN_HEADS, tq), F32),
            pltpu.VMEM((ATTN_HEADS, tq), F32),
        ],
        compiler_params=pltpu.CompilerParams(
            dimension_semantics=("arbitrary", "arbitrary"), vmem_limit_bytes=VMEM_LIMIT_BYTES),
    )(qit, wit, qt, ki, k, vt)


def _memkv_kernel(mem_ref, g_ref, wkv_ref, kn_ref, k_ref, v_ref):
    m = _rms_rows(mem_ref[...], g_ref[...]).astype(BF16)
    kv = _dot(m, wkv_ref[...])
    for hd in range(MEM_HEADS):
        cols = slice(hd * MEM_HEAD_DIM, (hd + 1) * MEM_HEAD_DIM)
        k_ref[:, cols] = _rms_rows(kv[:, cols], kn_ref[...]).astype(k_ref.dtype)
    v_ref[...] = kv[:, MEM_WIDTH:].astype(v_ref.dtype)


def _memkv_call(mem, g, wkv, kn):
    B, M, D = mem.shape
    full = lambda shape: pl.BlockSpec(shape, lambda b: (0,) * len(shape))
    return pl.pallas_call(
        _memkv_kernel,
        grid=(B,),
        in_specs=[pl.BlockSpec((None, M, D), lambda b: (b, 0, 0)), full(g.shape), full(wkv.shape), full(kn.shape)],
        out_specs=(pl.BlockSpec((None, M, MEM_WIDTH), lambda b: (b, 0, 0)),
                   pl.BlockSpec((None, M, MEM_WIDTH), lambda b: (b, 0, 0))),
        out_shape=(jax.ShapeDtypeStruct((B, M, MEM_WIDTH), BF16), jax.ShapeDtypeStruct((B, M, MEM_WIDTH), BF16)),
        compiler_params=pltpu.CompilerParams(dimension_semantics=("arbitrary",), vmem_limit_bytes=VMEM_LIMIT_BYTES),
    )(mem, g, wkv, kn)


def _split_bf16(a):
    hi = a.astype(BF16)
    lo = (a - hi.astype(F32)).astype(BF16)
    return hi, lo


def _mid_kernel(x_ref, yp_ref, ya_ref, wo1_ref, wo2_ref, gx_ref, wq_ref, qn_ref, km_ref, vm_ref, wo_ref,
                gf_ref, wr_ref, br_ref, rows_ref, gsel_ref):
    tm, d = x_ref.shape
    x1 = x_ref[...] + _dot(yp_ref[...], wo1_ref[...]) + _dot(ya_ref[...], wo2_ref[...])

    h = _rms_rows(x1, gx_ref[...]).astype(BF16)
    q = _dot(h, wq_ref[...])
    heads = []
    for hd in range(MEM_HEADS):
        cols = slice(hd * MEM_HEAD_DIM, (hd + 1) * MEM_HEAD_DIM)
        qh = (_rms_rows(q[:, cols], qn_ref[...]) * (MEM_HEAD_DIM ** -0.5)).astype(BF16)
        s = _dot_nt(qh, km_ref[:, cols])
        p = jnp.exp(s - jnp.max(s, axis=-1, keepdims=True))
        p = p / jnp.sum(p, axis=-1, keepdims=True)
        heads.append(_dot(p.astype(BF16), vm_ref[:, cols]))
    o = jnp.concatenate(heads, axis=-1).astype(BF16)
    x2 = x1 + _dot(o, wo_ref[...])
    rows_ref[:, :d] = x2

    h_hi, h_lo = _split_bf16(_rms_rows(x2, gf_ref[...]))
    w_hi, w_lo = _split_bf16(wr_ref[...])
    logits = _dot(h_hi, w_hi) + (_dot(h_hi, w_lo) + _dot(h_lo, w_hi)) + br_ref[...]
    lane = lax.broadcasted_iota(I32, (tm, LANES), 1).astype(F32)
    neg_inf = -jnp.inf
    g_logit = jnp.where(lane < N_GROUPS, logits, neg_inf)
    g_max = jnp.max(g_logit, axis=-1, keepdims=True)
    g_sel = jnp.min(jnp.where(g_logit == g_max, lane, LANES), axis=-1, keepdims=True)
    g_w = 1.0 / jnp.sum(jnp.exp(g_logit - g_max), axis=-1, keepdims=True)
    e_lo = N_GROUPS + g_sel * EXPERTS_PER_GROUP
    in_group = (lane >= e_lo) & (lane < e_lo + EXPERTS_PER_GROUP)
    e_logit = jnp.where(in_group, logits, neg_inf)
    v1 = jnp.max(e_logit, axis=-1, keepdims=True)
    i1 = jnp.min(jnp.where(e_logit == v1, lane, LANES), axis=-1, keepdims=True)
    rest = jnp.where(lane == i1, neg_inf, e_logit)
    v2 = jnp.max(rest, axis=-1, keepdims=True)
    i2 = jnp.min(jnp.where(rest == v2, lane, LANES), axis=-1, keepdims=True)
    e2 = jnp.exp(v2 - v1)
    w1 = g_w / (1.0 + e2)
    w2 = g_w * e2 / (1.0 + e2)
    gates = jnp.where(lane == i1 - N_GROUPS, w1, 0.0) + jnp.where(lane == i2 - N_GROUPS, w2, 0.0)
    rows_ref[:, d:] = gates
    gsel_ref[...] = jnp.broadcast_to(g_sel, (tm, LANES)).T[0:1, :]


def _mid_call(x, ypool, yattn, wo1, wo2, gx, wq, qn, kmem, vmem, wo, gf, wr, br, *, tm):
    B, S, D = x.shape
    M = kmem.shape[1]
    nt = S // tm
    full = lambda shape: pl.BlockSpec(shape, lambda b, i: (0,) * len(shape))
    tile = lambda width: pl.BlockSpec((None, tm, width), lambda b, i: (b, i, 0))
    return pl.pallas_call(
        _mid_kernel,
        grid=(B, nt),
        in_specs=[tile(D), tile(POOL_WIDTH), tile(ATTN_WIDTH), full(wo1.shape), full(wo2.shape), full(gx.shape),
                  full(wq.shape), full(qn.shape),
                  pl.BlockSpec((None, M, MEM_WIDTH), lambda b, i: (b, 0, 0)),
                  pl.BlockSpec((None, M, MEM_WIDTH), lambda b, i: (b, 0, 0)),
                  full(wo.shape), full(gf.shape), full(wr.shape), full(br.shape)],
        out_specs=(tile(D + LANES), pl.BlockSpec((None, 1, tm), lambda b, i: (b * nt + i, 0, 0))),
        out_shape=(jax.ShapeDtypeStruct((B, S, D + LANES), F32),
                   jax.ShapeDtypeStruct((B * nt, 1, tm), F32)),
        compiler_params=pltpu.CompilerParams(
            dimension_semantics=("arbitrary", "arbitrary"), vmem_limit_bytes=VMEM_LIMIT_BYTES),
    )(x, ypool, yattn, wo1, wo2, gx, wq, qn, kmem, vmem, wo, gf, wr, br)


META_NUSED = 64
ROW_DMA_UNROLL = 8


def _route_kernel(gsel_ref, pos_ref, meta_ref, *, tr):
    nb, w = gsel_ref.shape
    gsel = gsel_ref[...]
    lane = lax.broadcasted_iota(I32, (1, LANES), 1)
    before = (lax.broadcasted_iota(I32, (w, w), 0) < lax.broadcasted_iota(I32, (w, w), 1)).astype(BF16)
    onehot = [(gsel == g).astype(F32) for g in range(N_GROUPS)]
    cnt = sum(jnp.where(lane == g, jnp.sum(onehot[g], axis=1, keepdims=True), 0.0) for g in range(N_GROUPS))
    run = jnp.zeros((1, LANES), F32)
    carries = []
    for b in range(nb):
        carries.append(run)
        run = run + cnt[b:b + 1]
    carry = jnp.concatenate(carries, axis=0)
    shift = tr.bit_length() - 1
    padded = lax.shift_left(lax.shift_right_logical(run.astype(I32) + (tr - 1), shift), shift)
    size = [jnp.sum(jnp.where(lane == g, padded, 0), axis=1, keepdims=True) for g in range(N_GROUPS)]
    start = [sum(size[:g], jnp.zeros((1, 1), I32)) for g in range(N_GROUPS)]
    total = sum(size, jnp.zeros((1, 1), I32))
    pos = jnp.zeros((nb, w), F32)
    for g in range(N_GROUPS):
        rank = _dot(onehot[g].astype(BF16), before)
        base = jnp.sum(jnp.where(lane == g, carry, 0.0), axis=1, keepdims=True) + start[g].astype(F32)
        pos = pos + onehot[g] * (rank + base)
    pos_ref[...] = pos.astype(I32)
    tile_row = lane * tr
    tile_group = jnp.zeros((1, LANES), I32)
    last_group = jnp.zeros((1, 1), I32)
    for g in range(N_GROUPS):
        tile_group = tile_group + jnp.where((tile_row >= start[g]) & (tile_row < start[g] + size[g]), g, 0)
        last_group = jnp.where(size[g] > 0, g, last_group)
    tile_group = jnp.where(tile_row < total, tile_group, last_group)
    meta_ref[...] = jnp.where(lane == META_NUSED, lax.shift_right_logical(total, shift), tile_group)


def _route_call(gsel, *, tr):
    nb, w = gsel.shape
    return pl.pallas_call(
        functools.partial(_route_kernel, tr=tr),
        out_shape=(jax.ShapeDtypeStruct((nb, w), I32), jax.ShapeDtypeStruct((1, LANES), I32)),
        compiler_params=pltpu.CompilerParams(vmem_limit_bytes=VMEM_LIMIT_BYTES),
    )(gsel)


def _scatter_rows_kernel(pos_ref, src_ref, zeros_ref, dst_ref, sem):
    del zeros_ref
    tile = src_ref.shape[0]
    base = pl.program_id(0) * tile

    def body(r, carry):
        pltpu.make_async_copy(src_ref.at[pl.ds(r, 1)], dst_ref.at[pl.ds(pos_ref[base + r], 1)], sem).start()
        return carry

    lax.fori_loop(0, tile, body, 0, unroll=ROW_DMA_UNROLL)
    pltpu.make_async_copy(src_ref, dst_ref.at[pl.ds(0, tile)], sem).wait()


def _scatter_rows_call(pos, src, *, n_out, tile):
    n, width = src.shape
    any_spec = pl.BlockSpec(memory_space=pl.ANY)
    return pl.pallas_call(
        _scatter_rows_kernel,
        grid_spec=pltpu.PrefetchScalarGridSpec(
            num_scalar_prefetch=1, grid=(n // tile,),
            in_specs=[pl.BlockSpec((tile, width), lambda i, pos: (i, 0)), any_spec], out_specs=any_spec,
            scratch_shapes=[pltpu.SemaphoreType.DMA(())]),
        out_shape=jax.ShapeDtypeStruct((n_out, width), src.dtype),
        input_output_aliases={2: 0},
        compiler_params=pltpu.CompilerParams(dimension_semantics=("arbitrary",), has_side_effects=True),
    )(pos, src, jnp.zeros((n_out, width), src.dtype))


def _gather_rows_kernel(pos_ref, src_ref, out_ref, sem):
    tile = out_ref.shape[0]
    base = pl.program_id(0) * tile

    def body(r, carry):
        pltpu.make_async_copy(src_ref.at[pl.ds(pos_ref[base + r], 1)], out_ref.at[pl.ds(r, 1)], sem).start()
        return carry

    lax.fori_loop(0, tile, body, 0, unroll=ROW_DMA_UNROLL)
    pltpu.make_async_copy(src_ref.at[pl.ds(0, tile)], out_ref, sem).wait()


def _gather_rows_call(pos, src, *, tile):
    n = pos.shape[0]
    width = src.shape[1]
    return pl.pallas_call(
        _gather_rows_kernel,
        grid_spec=pltpu.PrefetchScalarGridSpec(
            num_scalar_prefetch=1, grid=(n // tile,),
            in_specs=[pl.BlockSpec(memory_space=pl.ANY)],
            out_specs=pl.BlockSpec((tile, width), lambda i, pos: (i, 0)),
            scratch_shapes=[pltpu.SemaphoreType.DMA(())]),
        out_shape=jax.ShapeDtypeStruct((n, width), src.dtype),
        compiler_params=pltpu.CompilerParams(dimension_semantics=("arbitrary",)),
    )(pos, src)


def _expert_kernel(meta_ref, rows_ref, gf_ref, wg_ref, wu_ref, wd_ref, out_ref):
    j = pl.program_id(0)
    tr, d = out_ref.shape
    n_exp, _, ff = wg_ref.shape

    @pl.when(j < meta_ref[META_NUSED])
    def _():
        x2 = rows_ref[:, :d]
        gates = rows_ref[:, d:]
        h = _rms_rows(x2, gf_ref[...]).astype(BF16)
        lane = lax.broadcasted_iota(I32, (tr, LANES), 1)
        first = meta_ref[j] * n_exp
        hid = []
        for e in range(n_exp):
            a = _dot(h, wg_ref[e])
            b = _dot(h, wu_ref[e])
            gate = jnp.sum(jnp.where(lane == first + e, gates, 0.0), axis=-1, keepdims=True)
            hid.append((a * jax.nn.sigmoid(a) * b * gate).astype(BF16))
        hid = jnp.concatenate(hid, axis=-1)
        out_ref[...] = x2 + _dot(hid, wd_ref[...].reshape(n_exp * ff, d))

    @pl.when(j >= meta_ref[META_NUSED])
    def _():
        out_ref[...] = jnp.zeros_like(out_ref)


def _expert_call(meta, rows, gf, wg, wu, wd, *, tr):
    n_rows, width = rows.shape
    d = width - LANES
    _, n_exp, _, ff = wg.shape
    by_group = lambda shape: pl.BlockSpec((None,) + shape, lambda j, meta: (meta[j], 0, 0, 0))
    return pl.pallas_call(
        _expert_kernel,
        grid_spec=pltpu.PrefetchScalarGridSpec(
            num_scalar_prefetch=1, grid=(n_rows // tr,),
            in_specs=[pl.BlockSpec((tr, width), lambda j, meta: (j, 0)),
                      pl.BlockSpec(gf.shape, lambda j, meta: (0, 0)),
                      by_group((n_exp, d, ff)), by_group((n_exp, d, ff)), by_group((n_exp, ff, d))],
            out_specs=pl.BlockSpec((tr, d), lambda j, meta: (j, 0))),
        out_shape=jax.ShapeDtypeStruct((n_rows, d), F32),
        compiler_params=pltpu.CompilerParams(dimension_semantics=("arbitrary",), vmem_limit_bytes=VMEM_LIMIT_BYTES),
    )(meta, rows, gf, wg, wu, wd)


def _rope_tables_t(seq_len, dim):
    inv = ROPE_THETA ** (-jnp.arange(0, dim, 2, dtype=F32) / dim)
    ang = jnp.arange(seq_len, dtype=F32)[:, None] * inv[None, :]
    ang = jnp.concatenate([ang, ang], axis=-1)
    sign = jnp.concatenate([-jnp.ones((dim // 2,), F32), jnp.ones((dim // 2,), F32)])
    return jnp.cos(ang).T, (jnp.sin(ang) * sign[None, :]).T


def _layer(x, mem, mix_norm, w_in, pool_w, pool_scale, q_norm, k_norm, idx_k_norm, w_out,
           xattn_norm, mem_norm, xattn_wq, xattn_wkv, xattn_q_norm, xattn_k_norm, xattn_wo,
           ffn_norm, router_group_w, router_group_b, router_expert_w, router_expert_b,
           expert_w_gate, expert_w_up, expert_w_down, *, tm, tq, tr):
    B, S, D = x.shape
    topk = min(TOPK_MAX, S // 4)
    row = lambda v: v.reshape(1, -1).astype(F32)
    col = lambda v: jnp.broadcast_to(v.astype(F32)[:, None], (v.shape[0], tm))

    wu = w_in[:, :POOL_WIDTH].astype(BF16)
    pad = lambda n: jnp.zeros((n, D), F32)
    wt = jnp.concatenate([
        w_in[:, OFF_Q:OFF_KI].T, w_in[:, OFF_KI:OFF_WI].T, pad(LANES - IDX_DIM),
        w_in[:, OFF_WI:IN_COLS].T, pad(2 * SUBLANES - IDX_HEADS)], axis=0).astype(BF16)
    cos_t, sin_t = _rope_tables_t(S, HEAD_DIM)

    ypool, qt, k, vt, qit, ki, wit = _proj_call(
        x, row(mix_norm), wu, wt, pool_w.astype(BF16), row(pool_scale), col(q_norm), col(k_norm),
        col(idx_k_norm), cos_t, sin_t, tm=tm)
    yattn = _dsa_call(qit, wit, qt, ki, k, vt, tq=tq, topk=topk)
    kmem, vmem = _memkv_call(mem, row(mem_norm), xattn_wkv.astype(BF16), row(xattn_k_norm))

    n_logits = N_GROUPS + N_EXPERTS
    wr = jnp.concatenate([router_group_w, router_expert_w, jnp.zeros((D, LANES - n_logits), F32)], axis=1)
    br = jnp.concatenate([router_group_b, router_expert_b, jnp.zeros((LANES - n_logits,), F32)]).reshape(1, LANES)
    rows, gsel = _mid_call(
        x, ypool, yattn, w_out[:POOL_WIDTH].astype(BF16), w_out[POOL_WIDTH:].astype(BF16), row(xattn_norm),
        xattn_wq.astype(BF16), row(xattn_q_norm), kmem, vmem, xattn_wo.astype(BF16), row(ffn_norm), wr, br,
        tm=tm)

    n_tok = B * S
    n_sorted = n_tok + N_GROUPS * tr
    pos, meta = _route_call(gsel.reshape(n_tok // tm, tm), tr=tr)
    pos = pos.reshape(n_tok)
    sorted_rows = _scatter_rows_call(pos, rows.reshape(n_tok, D + LANES), n_out=n_sorted, tile=tr)
    sorted_out = _expert_call(
        meta.reshape(LANES), sorted_rows, row(ffn_norm), expert_w_gate.astype(BF16), expert_w_up.astype(BF16),
        expert_w_down.astype(BF16), tr=tr)
    out = _gather_rows_call(pos, sorted_out, tile=tr)
    return out.reshape(B, S, D)


def kernel(x, mem, mix_norm, w_in, pool_w, pool_scale, q_norm, k_norm, idx_k_norm, w_out, xattn_norm, mem_norm,
           xattn_wq, xattn_wkv, xattn_q_norm, xattn_k_norm, xattn_wo, ffn_norm, router_group_w, router_group_b,
           router_expert_w, router_expert_b, expert_w_gate, expert_w_up, expert_w_down):
    depth = mix_norm.shape[0]
    for l in range(depth):
        x = _layer(
            x, mem, mix_norm[l], w_in[l], pool_w[l], pool_scale[l], q_norm[l], k_norm[l], idx_k_norm[l], w_out[l],
            xattn_norm[l], mem_norm[l], xattn_wq[l], xattn_wkv[l], xattn_q_norm[l], xattn_k_norm[l], xattn_wo[l],
            ffn_norm[l], router_group_w[l], router_group_b[l], router_expert_w[l], router_expert_b[l],
            expert_w_gate[l], expert_w_up[l], expert_w_down[l], tm=512, tq=512, tr=512)
    return x
```

```python
import functools
import math

import jax
import jax.numpy as jnp
from jax import lax
from jax.experimental import pallas as pl
from jax.experimental.pallas import tpu as pltpu

CHUNK = 64
POOL_WINDOWS = (2, 4, 8, 16)
POOL_GROUP = 128
POOL_WIDTH = POOL_GROUP * len(POOL_WINDOWS)
ATTN_HEADS = 8
HEAD_DIM = 64
ATTN_WIDTH = ATTN_HEADS * HEAD_DIM
IDX_HEADS = 8
IDX_DIM = 64
TOPK_MAX = 256
ROPE_THETA = 10000.0
MEM_HEADS = 4
MEM_HEAD_DIM = 128
MEM_WIDTH = MEM_HEADS * MEM_HEAD_DIM
N_GROUPS = 4
EXPERTS_PER_GROUP = 8
N_EXPERTS = N_GROUPS * EXPERTS_PER_GROUP
EPS = 1e-6
OFF_Q = POOL_WIDTH
OFF_K = OFF_Q + ATTN_WIDTH
OFF_V = OFF_K + ATTN_WIDTH
OFF_QI = OFF_V + ATTN_WIDTH
OFF_KI = OFF_QI + IDX_HEADS * IDX_DIM
OFF_WI = OFF_KI + IDX_DIM
IN_COLS = OFF_WI + IDX_HEADS

LANES = 128
SUBLANES = 8
VMEM_LIMIT_BYTES = 56 * 1024 * 1024

HALO = 16
PACKED_SUBLANES = 2 * SUBLANES
INT_MIN = -(2 ** 31)
HALF_MIN = -(2 ** 15)
NEG_BIG = -1e30
Q_SCALE = HEAD_DIM ** -0.5 * math.log2(math.e)

F32 = jnp.float32
BF16 = jnp.bfloat16
I32 = jnp.int32
I16 = jnp.int16


def _dot(a, b):
    return jnp.dot(a, b, preferred_element_type=F32)


def _dot_nt(a, b):
    return lax.dot_general(a, b, (((1,), (1,)), ((), ())), preferred_element_type=F32)


def _rms_rows(x, g):
    ms = jnp.mean(x * x, axis=-1, keepdims=True)
    return x * lax.rsqrt(ms + EPS) * g


ROW_Q = 0
ROW_K = ROW_Q + ATTN_WIDTH
ROW_V = ROW_K + ATTN_WIDTH
ROW_QI = ROW_V + ATTN_WIDTH
ROW_KI = ROW_QI + IDX_HEADS * IDX_DIM
ROW_WI = ROW_KI + LANES
ROWS_T = ROW_WI + 2 * SUBLANES


def _norm_rope_t(z, gain, cos, sin_signed):
    if gain is not None:
        ms = jnp.mean(z * z, axis=0, keepdims=True)
        z = z * lax.rsqrt(ms + EPS) * gain
    half = z.shape[0] // 2
    swapped = jnp.concatenate([z[half:], z[:half]], axis=0)
    return z * cos + swapped * sin_signed


def _proj_kernel(x_ref, g_ref, wu_ref, wt_ref, poolw_ref, pscale_ref, qg_ref, kg_ref, ig_ref,
                 cos_ref, sin_ref,
                 ypool_ref, qt_ref, k_ref, vt_ref, qit_ref, ki_ref, wit_ref,
                 halo_ref, ext_ref, kt_ref, *, idx_scale):
    tm = x_ref.shape[0]
    h = _rms_rows(x_ref[...], g_ref[...]).astype(BF16)
    cos = cos_ref[...]
    sin = sin_ref[...]

    u = _dot(h, wu_ref[...])
    @pl.when(pl.program_id(1) == 0)
    def _():
        halo_ref[...] = jnp.zeros_like(halo_ref)

    ext_ref[0:HALO, :] = halo_ref[...]
    ext_ref[HALO:HALO + tm, :] = u
    halo_ref[...] = u[tm - HALO:, :]
    pos1 = pl.program_id(1) * tm + lax.broadcasted_iota(I32, (tm, 1), 0) + 1
    for g, w in enumerate(POOL_WINDOWS):
        cols = slice(g * POOL_GROUP, (g + 1) * POOL_GROUP)
        win = u[:, cols]
        for j in range(1, w):
            win = win + ext_ref[HALO - j:HALO - j + tm, cols]
        cnt = jnp.minimum(pos1, w).astype(F32)
        mixed = (win / cnt - u[:, cols]).astype(BF16)
        y = _dot(mixed, poolw_ref[g]) * pscale_ref[:, cols]
        ypool_ref[:, cols] = y.astype(ypool_ref.dtype)

    qg = qg_ref[...]
    kg = kg_ref[...]
    zq = _dot_nt(wt_ref[ROW_Q:ROW_K, :], h)
    for hd in range(ATTN_HEADS):
        rows = slice(hd * HEAD_DIM, (hd + 1) * HEAD_DIM)
        qt_ref[rows, :] = (_norm_rope_t(zq[rows], qg, cos, sin) * Q_SCALE).astype(qt_ref.dtype)
    zk = _dot_nt(wt_ref[ROW_K:ROW_V, :], h)
    for hd in range(ATTN_HEADS):
        rows = slice(hd * HEAD_DIM, (hd + 1) * HEAD_DIM)
        kt_ref[rows, :] = _norm_rope_t(zk[rows], kg, cos, sin)
    k_ref[...] = kt_ref[...].T.astype(k_ref.dtype)
    vt_ref[...] = _dot_nt(wt_ref[ROW_V:ROW_QI, :], h).astype(vt_ref.dtype)
    zqi = _dot_nt(wt_ref[ROW_QI:ROW_KI, :], h)
    for hd in range(IDX_HEADS):
        rows = slice(hd * IDX_DIM, (hd + 1) * IDX_DIM)
        qit_ref[rows, :] = _norm_rope_t(zqi[rows], None, cos, sin).astype(qit_ref.dtype)
    zi = _dot_nt(wt_ref[ROW_KI:ROWS_T, :], h)
    kit = _norm_rope_t(zi[0:IDX_DIM], ig_ref[...], cos, sin)
    kit = jnp.concatenate([kit, jnp.zeros((LANES - IDX_DIM, tm), F32)], axis=0)
    ki_ref[...] = kit.T.astype(ki_ref.dtype)
    wit_ref[...] = zi[LANES:LANES + IDX_HEADS] * idx_scale


def _proj_call(x, g, wu, wt, poolw, pscale, qg, kg, ig, cos_t, sin_t, *, tm):
    B, S, D = x.shape
    nt = S // tm
    kernel = functools.partial(_proj_kernel, idx_scale=(IDX_DIM ** -0.5) * (IDX_HEADS ** -0.5))
    full = lambda shape: pl.BlockSpec(shape, lambda b, i: (0,) * len(shape))
    out_shape = (
        jax.ShapeDtypeStruct((B, S, POOL_WIDTH), BF16),
        jax.ShapeDtypeStruct((B, ATTN_WIDTH, S), BF16),
        jax.ShapeDtypeStruct((B, S, ATTN_WIDTH), BF16),
        jax.ShapeDtypeStruct((B, nt, ATTN_WIDTH, tm), BF16),
        jax.ShapeDtypeStruct((B, IDX_HEADS * IDX_DIM, S), BF16),
        jax.ShapeDtypeStruct((B, S, LANES), BF16),
        jax.ShapeDtypeStruct((B, IDX_HEADS, S), F32),
    )
    return pl.pallas_call(
        kernel,
        grid=(B, nt),
        in_specs=[
            pl.BlockSpec((None, tm, D), lambda b, i: (b, i, 0)),
            full(g.shape), full(wu.shape), full(wt.shape), full(poolw.shape), full(pscale.shape),
            full(qg.shape), full(kg.shape), full(ig.shape),
            pl.BlockSpec((HEAD_DIM, tm), lambda b, i: (0, i)),
            pl.BlockSpec((HEAD_DIM, tm), lambda b, i: (0, i)),
        ],
        out_specs=(
            pl.BlockSpec((None, tm, POOL_WIDTH), lambda b, i: (b, i, 0)),
            pl.BlockSpec((None, ATTN_WIDTH, tm), lambda b, i: (b, 0, i)),
            pl.BlockSpec((None, tm, ATTN_WIDTH), lambda b, i: (b, i, 0)),
            pl.BlockSpec((None, None, ATTN_WIDTH, tm), lambda b, i: (b, i, 0, 0)),
            pl.BlockSpec((None, IDX_HEADS * IDX_DIM, tm), lambda b, i: (b, 0, i)),
            pl.BlockSpec((None, tm, LANES), lambda b, i: (b, i, 0)),
            pl.BlockSpec((None, IDX_HEADS, tm), lambda b, i: (b, 0, i)),
        ),
        out_shape=out_shape,
        scratch_shapes=[
            pltpu.VMEM((HALO, POOL_WIDTH), F32),
            pltpu.VMEM((HALO + tm, POOL_WIDTH), F32),
            pltpu.VMEM((ATTN_WIDTH, tm), F32),
        ],
        compiler_params=pltpu.CompilerParams(
            dimension_semantics=("arbitrary", "arbitrary"), vmem_limit_bytes=VMEM_LIMIT_BYTES),
    )(x, g, wu, wt, poolw, pscale, qg, kg, ig, cos_t, sin_t)


def _sortable_key(score):
    bits = lax.bitcast_convert_type(score, I32)
    key = jnp.where(bits < 0, bits ^ jnp.int32(0x7FFFFFFF), bits)
    return jnp.where(key == -1, 0, key)


def _colsum8(x):
    rows, t = x.shape
    return jnp.sum(x.reshape(rows // SUBLANES, SUBLANES, t), axis=0)


def _dsa_kernel(qit_ref, wit_ref, qt_ref, ki_ref, k_ref, vt_ref, out_ref, sc_ref, half_ref, ot_ref, m_ref, l_ref,
                *, topk):
    tq = qit_ref.shape[1]
    nkt_max, tk, _ = sc_ref.shape
    q0 = pl.program_id(1) * tq
    nkt = (q0 + tq + tk - 1) // tk
    qpos = q0 + lax.broadcasted_iota(I32, (1, tq), 1)
    qend = (qpos // CHUNK + 1) * CHUNK
    krow = lax.broadcasted_iota(I32, (tk, 1), 0)
    zeros_half = jnp.zeros((LANES - IDX_DIM, tq), BF16)

    def score_tile(kt, carry):
        ki_t = ki_ref[pl.ds(pl.multiple_of(kt * tk, tk), tk), :]
        acc = jnp.zeros((tk, tq), F32)
        for hd in range(IDX_HEADS):
            qh = jnp.concatenate([qit_ref[hd * IDX_DIM:(hd + 1) * IDX_DIM, :], zeros_half], axis=0)
            rel = jnp.maximum(_dot(ki_t, qh), 0.0)
            acc = acc + rel * wit_ref[hd:hd + 1, :]
        key = jnp.where(kt * tk + krow < qend, _sortable_key(acc), INT_MIN)
        sc_ref[kt] = key
        half_ref[kt] = lax.shift_right_arithmetic(key, 16).astype(I16)
        return carry

    lax.fori_loop(0, nkt, score_tile, 0)

    def count(pred_fn):
        def body(kt, c8):
            return c8 + _colsum8(jnp.where(pred_fn(sc_ref[kt], kt), 1, 0).astype(I32))
        c8 = lax.fori_loop(0, nkt, body, jnp.zeros((SUBLANES, tq), I32))
        return jnp.sum(c8, axis=0, keepdims=True)

    def count_half(pred_fn):
        def body(kt, c16):
            ones = jnp.where(pred_fn(half_ref[kt]), jnp.int16(1), jnp.int16(0))
            for j in range(tk // PACKED_SUBLANES):
                c16 = c16 + ones[j * PACKED_SUBLANES:(j + 1) * PACKED_SUBLANES]
            return c16
        c16 = lax.fori_loop(0, nkt, body, jnp.zeros((PACKED_SUBLANES, tq), I16))
        return jnp.sum(c16.astype(I32), axis=0, keepdims=True)

    def bisect_half():
        def step(i, t):
            cand = t + lax.shift_left(jnp.int32(1), 15 - i)
            cand16 = cand.astype(I16)
            c = count_half(lambda half: half >= cand16)
            return jnp.where(c >= topk, cand, t)
        return lax.fori_loop(0, 16, step, jnp.full((1, tq), HALF_MIN, I32))

    thr_hi = bisect_half()

    def low_tile(kt, carry):
        hi = half_ref[kt].astype(I32)
        lo = (sc_ref[kt] & 0xFFFF) + HALF_MIN
        lo = jnp.where(hi == thr_hi, lo, jnp.where(hi > thr_hi, -HALF_MIN - 1, HALF_MIN))
        half_ref[kt] = lo.astype(I16)
        return carry

    lax.fori_loop(0, nkt, low_tile, 0)
    thr_lo = bisect_half()
    thr = thr_hi * 65536 + (thr_lo - HALF_MIN)

    thr_adm = jnp.maximum(thr, INT_MIN + 1)
    n_sel = count(lambda key, kt: key >= thr_adm)
    excess_ties = jnp.max(n_sel) > topk

    def store_mask(sel_fn):
        def body(kt, carry):
            sel = sel_fn(sc_ref[kt], kt)
            sc_ref[kt] = lax.bitcast_convert_type(jnp.where(sel, 0.0, NEG_BIG).astype(F32), I32)
            return carry
        lax.fori_loop(0, nkt, body, 0)

    @pl.when(jnp.logical_not(excess_ties))
    def _():
        store_mask(lambda key, kt: key >= thr_adm)

    @pl.when(excess_ties)
    def _():
        need = topk - count(lambda key, kt: key > thr)
        nbits = max(1, (nkt_max * tk - 1).bit_length())

        def tie_step(step, cut):
            cand = cut + lax.shift_left(jnp.int32(1), nbits - 1 - step)
            c = count(lambda key, kt: (key == thr) & (kt * tk + krow < cand))
            return jnp.where(c < need, cand, cut)

        cut = lax.fori_loop(0, nbits, tie_step, jnp.zeros((1, tq), I32))
        cut = jnp.where(thr == INT_MIN, -1, cut)
        store_mask(lambda key, kt: (key > thr) | ((key == thr) & (kt * tk + krow <= cut)))

    zeros_head = jnp.zeros((HEAD_DIM, tq), BF16)
    ones_rows = jnp.ones((PACKED_SUBLANES, tk), BF16)

    def masked_logits(kt, hd):
        qh = qt_ref[hd * HEAD_DIM:(hd + 1) * HEAD_DIM, :]
        qh = jnp.concatenate([qh, zeros_head] if hd % 2 == 0 else [zeros_head, qh], axis=0)
        k2 = k_ref[pl.ds(pl.multiple_of(kt * tk, tk), tk), (hd // 2) * LANES:(hd // 2 + 1) * LANES]
        return _dot(k2, qh) + lax.bitcast_convert_type(sc_ref[kt], F32)

    def weighted_values(kt, hd, p):
        pv = _dot(jnp.concatenate([vt_ref[kt, hd * HEAD_DIM:(hd + 1) * HEAD_DIM, :], ones_rows], axis=0), p)
        return pv[:HEAD_DIM, :], pv[HEAD_DIM:HEAD_DIM + 1, :]

    def attn_tile_plain(kt, carry):
        for hd in range(ATTN_HEADS):
            rows = slice(hd * HEAD_DIM, (hd + 1) * HEAD_DIM)
            o, l = weighted_values(kt, hd, jnp.exp2(masked_logits(kt, hd)).astype(BF16))
            l_ref[hd:hd + 1, :] += l
            ot_ref[rows, :] += o
        return carry

    def attn_tile_online(kt, carry):
        for hd in range(ATTN_HEADS):
            rows = slice(hd * HEAD_DIM, (hd + 1) * HEAD_DIM)
            s = masked_logits(kt, hd)
            m_old = m_ref[hd:hd + 1, :]
            m_new = jnp.maximum(m_old, jnp.max(s, axis=0, keepdims=True))
            alpha = jnp.exp2(m_old - m_new)
            m_ref[hd:hd + 1, :] = m_new
            o, l = weighted_values(kt, hd, jnp.exp2(s - m_new).astype(BF16))
            l_ref[hd:hd + 1, :] = alpha * l_ref[hd:hd + 1, :] + l
            ot_ref[rows, :] = alpha * ot_ref[rows, :] + o
        return carry

    l_ref[...] = jnp.zeros(l_ref.shape, F32)
    ot_ref[...] = jnp.zeros(ot_ref.shape, F32)
    lax.fori_loop(0, nkt, attn_tile_plain, 0)
    l_all = l_ref[...]
    unusable = jnp.where((l_all > 0.0) & (l_all < jnp.finfo(F32).max), 0, 1)

    @pl.when(jnp.max(unusable) > 0)
    def _():
        m_ref[...] = jnp.full(m_ref.shape, NEG_BIG, F32)
        l_ref[...] = jnp.zeros(l_ref.shape, F32)
        ot_ref[...] = jnp.zeros(ot_ref.shape, F32)
        lax.fori_loop(0, nkt, attn_tile_online, 0)

    for hd in range(ATTN_HEADS):
        rows = slice(hd * HEAD_DIM, (hd + 1) * HEAD_DIM)
        ot_ref[rows, :] = ot_ref[rows, :] / l_ref[hd:hd + 1, :]
    out_ref[...] = ot_ref[...].T.astype(out_ref.dtype)


def _dsa_call(qit, wit, qt, ki, k, vt, *, tq, topk):
    B, S, _ = k.shape
    _, nkt, _, tk = vt.shape
    kernel = functools.partial(_dsa_kernel, topk=topk)
    return pl.pallas_call(
        kernel,
        grid=(B, S // tq),
        in_specs=[
            pl.BlockSpec((None, IDX_HEADS * IDX_DIM, tq), lambda b, i: (b, 0, i)),
            pl.BlockSpec((None, IDX_HEADS, tq), lambda b, i: (b, 0, i)),
            pl.BlockSpec((None, ATTN_WIDTH, tq), lambda b, i: (b, 0, i)),
            pl.BlockSpec((None, S, LANES), lambda b, i: (b, 0, 0)),
            pl.BlockSpec((None, S, ATTN_WIDTH), lambda b, i: (b, 0, 0)),
            pl.BlockSpec((None, nkt, ATTN_WIDTH, tk), lambda b, i: (b, 0, 0, 0)),
        ],
        out_specs=pl.BlockSpec((None, tq, ATTN_WIDTH), lambda b, i: (b, i, 0)),
        out_shape=jax.ShapeDtypeStruct((B, S, ATTN_WIDTH), BF16),
        scratch_shapes=[
            pltpu.VMEM((nkt, tk, tq), I32),
            pltpu.VMEM((nkt, tk, tq), I16),
            pltpu.VMEM((ATTN_WIDTH, tq), F32),
            pltpu.VMEM((ATTN_HEADS, tq), F32),
            pltpu.VMEM((ATTN_HEADS, tq), F32),
        ],
        compiler_params=pltpu.CompilerParams(
            dimension_semantics=("arbitrary", "arbitrary"), vmem_limit_bytes=VMEM_LIMIT_BYTES),
    )(qit, wit, qt, ki, k, vt)


def _memkv_kernel(mem_ref, g_ref, wkv_ref, kn_ref, k_ref, v_ref):
    m = _rms_rows(mem_ref[...], g_ref[...]).astype(BF16)
    kv = _dot(m, wkv_ref[...])
    for hd in range(MEM_HEADS):
        cols = slice(hd * MEM_HEAD_DIM, (hd + 1) * MEM_HEAD_DIM)
        k_ref[:, cols] = _rms_rows(kv[:, cols], kn_ref[...]).astype(k_ref.dtype)
    v_ref[...] = kv[:, MEM_WIDTH:].astype(v_ref.dtype)


def _memkv_call(mem, g, wkv, kn):
    B, M, D = mem.shape
    full = lambda shape: pl.BlockSpec(shape, lambda b: (0,) * len(shape))
    return pl.pallas_call(
        _memkv_kernel,
        grid=(B,),
        in_specs=[pl.BlockSpec((None, M, D), lambda b: (b, 0, 0)), full(g.shape), full(wkv.shape), full(kn.shape)],
        out_specs=(pl.BlockSpec((None, M, MEM_WIDTH), lambda b: (b, 0, 0)),
                   pl.BlockSpec((None, M, MEM_WIDTH), lambda b: (b, 0, 0))),
        out_shape=(jax.ShapeDtypeStruct((B, M, MEM_WIDTH), BF16), jax.ShapeDtypeStruct((B, M, MEM_WIDTH), BF16)),
        compiler_params=pltpu.CompilerParams(dimension_semantics=("arbitrary",), vmem_limit_bytes=VMEM_LIMIT_BYTES),
    )(mem, g, wkv, kn)


def _split_bf16(a):
    hi = a.astype(BF16)
    lo = (a - hi.astype(F32)).astype(BF16)
    return hi, lo


def _mid_kernel(x_ref, yp_ref, ya_ref, wo1_ref, wo2_ref, gx_ref, wq_ref, qn_ref, km_ref, vm_ref, wo_ref,
                gf_ref, wr_ref, br_ref, rows_ref, gsel_ref):
    tm, d = x_ref.shape
    x1 = x_ref[...] + _dot(yp_ref[...], wo1_ref[...]) + _dot(ya_ref[...], wo2_ref[...])

    h = _rms_rows(x1, gx_ref[...]).astype(BF16)
    q = _dot(h, wq_ref[...])
    heads = []
    for hd in range(MEM_HEADS):
        cols = slice(hd * MEM_HEAD_DIM, (hd + 1) * MEM_HEAD_DIM)
        qh = (_rms_rows(q[:, cols], qn_ref[...]) * (MEM_HEAD_DIM ** -0.5)).astype(BF16)
        s = _dot_nt(qh, km_ref[:, cols])
        p = jnp.exp(s - jnp.max(s, axis=-1, keepdims=True))
        p = p / jnp.sum(p, axis=-1, keepdims=True)
        heads.append(_dot(p.astype(BF16), vm_ref[:, cols]))
    o = jnp.concatenate(heads, axis=-1).astype(BF16)
    x2 = x1 + _dot(o, wo_ref[...])
    rows_ref[:, :d] = x2

    h_hi, h_lo = _split_bf16(_rms_rows(x2, gf_ref[...]))
    w_hi, w_lo = _split_bf16(wr_ref[...])
    logits = _dot(h_hi, w_hi) + (_dot(h_hi, w_lo) + _dot(h_lo, w_hi)) + br_ref[...]
    lane = lax.broadcasted_iota(I32, (tm, LANES), 1).astype(F32)
    neg_inf = -jnp.inf
    g_logit = jnp.where(lane < N_GROUPS, logits, neg_inf)
    g_max = jnp.max(g_logit, axis=-1, keepdims=True)
    g_sel = jnp.min(jnp.where(g_logit == g_max, lane, LANES), axis=-1, keepdims=True)
    g_w = 1.0 / jnp.sum(jnp.exp(g_logit - g_max), axis=-1, keepdims=True)
    e_lo = N_GROUPS + g_sel * EXPERTS_PER_GROUP
    in_group = (lane >= e_lo) & (lane < e_lo + EXPERTS_PER_GROUP)
    e_logit = jnp.where(in_group, logits, neg_inf)
    v1 = jnp.max(e_logit, axis=-1, keepdims=True)
    i1 = jnp.min(jnp.where(e_logit == v1, lane, LANES), axis=-1, keepdims=True)
    rest = jnp.where(lane == i1, neg_inf, e_logit)
    v2 = jnp.max(rest, axis=-1, keepdims=True)
    i2 = jnp.min(jnp.where(rest == v2, lane, LANES), axis=-1, keepdims=True)
    e2 = jnp.exp(v2 - v1)
    w1 = g_w / (1.0 + e2)
    w2 = g_w * e2 / (1.0 + e2)
    gates = jnp.where(lane == i1 - N_GROUPS, w1, 0.0) + jnp.where(lane == i2 - N_GROUPS, w2, 0.0)
    rows_ref[:, d:] = gates
    gsel_ref[...] = jnp.broadcast_to(g_sel, (tm, LANES)).T[0:1, :]


def _mid_call(x, ypool, yattn, wo1, wo2, gx, wq, qn, kmem, vmem, wo, gf, wr, br, *, tm):
    B, S, D = x.shape
    M = kmem.shape[1]
    nt = S // tm
    full = lambda shape: pl.BlockSpec(shape, lambda b, i: (0,) * len(shape))
    tile = lambda width: pl.BlockSpec((None, tm, width), lambda b, i: (b, i, 0))
    return pl.pallas_call(
        _mid_kernel,
        grid=(B, nt),
        in_specs=[tile(D), tile(POOL_WIDTH), tile(ATTN_WIDTH), full(wo1.shape), full(wo2.shape), full(gx.shape),
                  full(wq.shape), full(qn.shape),
                  pl.BlockSpec((None, M, MEM_WIDTH), lambda b, i: (b, 0, 0)),
                  pl.BlockSpec((None, M, MEM_WIDTH), lambda b, i: (b, 0, 0)),
                  full(wo.shape), full(gf.shape), full(wr.shape), full(br.shape)],
        out_specs=(tile(D + LANES), pl.BlockSpec((None, 1, tm), lambda b, i: (b * nt + i, 0, 0))),
        out_shape=(jax.ShapeDtypeStruct((B, S, D + LANES), F32),
                   jax.ShapeDtypeStruct((B * nt, 1, tm), F32)),
        compiler_params=pltpu.CompilerParams(
            dimension_semantics=("arbitrary", "arbitrary"), vmem_limit_bytes=VMEM_LIMIT_BYTES),
    )(x, ypool, yattn, wo1, wo2, gx, wq, qn, kmem, vmem, wo, gf, wr, br)


META_NUSED = 64
ROW_DMA_UNROLL = 8


def _route_kernel(gsel_ref, pos_ref, meta_ref, *, tr):
    nb, w = gsel_ref.shape
    gsel = gsel_ref[...]
    lane = lax.broadcasted_iota(I32, (1, LANES), 1)
    before = (lax.broadcasted_iota(I32, (w, w), 0) < lax.broadcasted_iota(I32, (w, w), 1)).astype(BF16)
    onehot = [(gsel == g).astype(F32) for g in range(N_GROUPS)]
    cnt = sum(jnp.where(lane == g, jnp.sum(onehot[g], axis=1, keepdims=True), 0.0) for g in range(N_GROUPS))
    run = jnp.zeros((1, LANES), F32)
    carries = []
    for b in range(nb):
        carries.append(run)
        run = run + cnt[b:b + 1]
    carry = jnp.concatenate(carries, axis=0)
    shift = tr.bit_length() - 1
    padded = lax.shift_left(lax.shift_right_logical(run.astype(I32) + (tr - 1), shift), shift)
    size = [jnp.sum(jnp.where(lane == g, padded, 0), axis=1, keepdims=True) for g in range(N_GROUPS)]
    start = [sum(size[:g], jnp.zeros((1, 1), I32)) for g in range(N_GROUPS)]
    total = sum(size, jnp.zeros((1, 1), I32))
    pos = jnp.zeros((nb, w), F32)
    for g in range(N_GROUPS):
        rank = _dot(onehot[g].astype(BF16), before)
        base = jnp.sum(jnp.where(lane == g, carry, 0.0), axis=1, keepdims=True) + start[g].astype(F32)
        pos = pos + onehot[g] * (rank + base)
    pos_ref[...] = pos.astype(I32)
    tile_row = lane * tr
    tile_group = jnp.zeros((1, LANES), I32)
    last_group = jnp.zeros((1, 1), I32)
    for g in range(N_GROUPS):
        tile_group = tile_group + jnp.where((tile_row >= start[g]) & (tile_row < start[g] + size[g]), g, 0)
        last_group = jnp.where(size[g] > 0, g, last_group)
    tile_group = jnp.where(tile_row < total, tile_group, last_group)
    meta_ref[...] = jnp.where(lane == META_NUSED, lax.shift_right_logical(total, shift), tile_group)


def _route_call(gsel, *, tr):
    nb, w = gsel.shape
    return pl.pallas_call(
        functools.partial(_route_kernel, tr=tr),
        out_shape=(jax.ShapeDtypeStruct((nb, w), I32), jax.ShapeDtypeStruct((1, LANES), I32)),
        compiler_params=pltpu.CompilerParams(vmem_limit_bytes=VMEM_LIMIT_BYTES),
    )(gsel)


def _scatter_rows_kernel(pos_ref, src_ref, zeros_ref, dst_ref, sem):
    del zeros_ref
    tile = src_ref.shape[0]
    base = pl.program_id(0) * tile

    def body(i, carry):
        for u in range(ROW_DMA_UNROLL):
            r = i * ROW_DMA_UNROLL + u
            pltpu.make_async_copy(
                src_ref.at[pl.ds(r, 1)], dst_ref.at[pl.ds(pos_ref[base + r], 1)], sem).start(priority=u % 2)
        return carry

    lax.fori_loop(0, tile // ROW_DMA_UNROLL, body, 0)
    pltpu.make_async_copy(src_ref, dst_ref.at[pl.ds(0, tile)], sem).wait()


def _scatter_rows_call(pos, src, *, n_out, tile):
    n, width = src.shape
    any_spec = pl.BlockSpec(memory_space=pl.ANY)
    return pl.pallas_call(
        _scatter_rows_kernel,
        grid_spec=pltpu.PrefetchScalarGridSpec(
            num_scalar_prefetch=1, grid=(n // tile,),
            in_specs=[pl.BlockSpec((tile, width), lambda i, pos: (i, 0)), any_spec], out_specs=any_spec,
            scratch_shapes=[pltpu.SemaphoreType.DMA(())]),
        out_shape=jax.ShapeDtypeStruct((n_out, width), src.dtype),
        input_output_aliases={2: 0},
        compiler_params=pltpu.CompilerParams(dimension_semantics=("arbitrary",), has_side_effects=True),
    )(pos, src, jnp.zeros((n_out, width), src.dtype))


def _gather_rows_kernel(pos_ref, src_ref, out_ref, sem):
    tile = out_ref.shape[0]
    base = pl.program_id(0) * tile

    def body(i, carry):
        for u in range(ROW_DMA_UNROLL):
            r = i * ROW_DMA_UNROLL + u
            pltpu.make_async_copy(
                src_ref.at[pl.ds(pos_ref[base + r], 1)], out_ref.at[pl.ds(r, 1)], sem).start(priority=u % 2)
        return carry

    lax.fori_loop(0, tile // ROW_DMA_UNROLL, body, 0)
    pltpu.make_async_copy(src_ref.at[pl.ds(0, tile)], out_ref, sem).wait()


def _gather_rows_call(pos, src, *, tile):
    n = pos.shape[0]
    width = src.shape[1]
    return pl.pallas_call(
        _gather_rows_kernel,
        grid_spec=pltpu.PrefetchScalarGridSpec(
            num_scalar_prefetch=1, grid=(n // tile,),
            in_specs=[pl.BlockSpec(memory_space=pl.ANY)],
            out_specs=pl.BlockSpec((tile, width), lambda i, pos: (i, 0)),
            scratch_shapes=[pltpu.SemaphoreType.DMA(())]),
        out_shape=jax.ShapeDtypeStruct((n, width), src.dtype),
        compiler_params=pltpu.CompilerParams(dimension_semantics=("arbitrary",)),
    )(pos, src)


def _expert_kernel(meta_ref, rows_ref, gf_ref, wg_ref, wu_ref, wd_ref, out_ref):
    j = pl.program_id(0)
    tr, d = out_ref.shape
    n_exp, _, ff = wg_ref.shape

    @pl.when(j < meta_ref[META_NUSED])
    def _():
        x2 = rows_ref[:, :d]
        gates = rows_ref[:, d:]
        h = _rms_rows(x2, gf_ref[...]).astype(BF16)
        lane = lax.broadcasted_iota(I32, (tr, LANES), 1)
        first = meta_ref[j] * n_exp
        hid = []
        for e in range(n_exp):
            a = _dot(h, wg_ref[e])
            b = _dot(h, wu_ref[e])
            gate = jnp.sum(jnp.where(lane == first + e, gates, 0.0), axis=-1, keepdims=True)
            hid.append((a * jax.nn.sigmoid(a) * b * gate).astype(BF16))
        hid = jnp.concatenate(hid, axis=-1)
        out_ref[...] = x2 + _dot(hid, wd_ref[...].reshape(n_exp * ff, d))

    @pl.when(j >= meta_ref[META_NUSED])
    def _():
        out_ref[...] = jnp.zeros_like(out_ref)


def _expert_call(meta, rows, gf, wg, wu, wd, *, tr):
    n_rows, width = rows.shape
    d = width - LANES
    _, n_exp, _, ff = wg.shape
    by_group = lambda shape: pl.BlockSpec((None,) + shape, lambda j, meta: (meta[j], 0, 0, 0))
    return pl.pallas_call(
        _expert_kernel,
        grid_spec=pltpu.PrefetchScalarGridSpec(
            num_scalar_prefetch=1, grid=(n_rows // tr,),
            in_specs=[pl.BlockSpec((tr, width), lambda j, meta: (j, 0)),
                      pl.BlockSpec(gf.shape, lambda j, meta: (0, 0)),
                      by_group((n_exp, d, ff)), by_group((n_exp, d, ff)), by_group((n_exp, ff, d))],
            out_specs=pl.BlockSpec((tr, d), lambda j, meta: (j, 0))),
        out_shape=jax.ShapeDtypeStruct((n_rows, d), F32),
        compiler_params=pltpu.CompilerParams(dimension_semantics=("arbitrary",), vmem_limit_bytes=VMEM_LIMIT_BYTES),
    )(meta, rows, gf, wg, wu, wd)


def _rope_tables_t(seq_len, dim):
    inv = ROPE_THETA ** (-jnp.arange(0, dim, 2, dtype=F32) / dim)
    ang = jnp.arange(seq_len, dtype=F32)[:, None] * inv[None, :]
    ang = jnp.concatenate([ang, ang], axis=-1)
    sign = jnp.concatenate([-jnp.ones((dim // 2,), F32), jnp.ones((dim // 2,), F32)])
    return jnp.cos(ang).T, (jnp.sin(ang) * sign[None, :]).T


def _layer(x, mem, mix_norm, w_in, pool_w, pool_scale, q_norm, k_norm, idx_k_norm, w_out,
           xattn_norm, mem_norm, xattn_wq, xattn_wkv, xattn_q_norm, xattn_k_norm, xattn_wo,
           ffn_norm, router_group_w, router_group_b, router_expert_w, router_expert_b,
           expert_w_gate, expert_w_up, expert_w_down, *, tm, tq, tr):
    B, S, D = x.shape
    topk = min(TOPK_MAX, S // 4)
    row = lambda v: v.reshape(1, -1).astype(F32)
    col = lambda v: jnp.broadcast_to(v.astype(F32)[:, None], (v.shape[0], tm))

    wu = w_in[:, :POOL_WIDTH].astype(BF16)
    pad = lambda n: jnp.zeros((n, D), F32)
    wt = jnp.concatenate([
        w_in[:, OFF_Q:OFF_KI].T, w_in[:, OFF_KI:OFF_WI].T, pad(LANES - IDX_DIM),
        w_in[:, OFF_WI:IN_COLS].T, pad(2 * SUBLANES - IDX_HEADS)], axis=0).astype(BF16)
    cos_t, sin_t = _rope_tables_t(S, HEAD_DIM)

    ypool, qt, k, vt, qit, ki, wit = _proj_call(
        x, row(mix_norm), wu, wt, pool_w.astype(BF16), row(pool_scale), col(q_norm), col(k_norm),
        col(idx_k_norm), cos_t, sin_t, tm=tm)
    yattn = _dsa_call(qit, wit, qt, ki, k, vt, tq=tq, topk=topk)
    kmem, vmem = _memkv_call(mem, row(mem_norm), xattn_wkv.astype(BF16), row(xattn_k_norm))

    n_logits = N_GROUPS + N_EXPERTS
    wr = jnp.concatenate([router_group_w, router_expert_w, jnp.zeros((D, LANES - n_logits), F32)], axis=1)
    br = jnp.concatenate([router_group_b, router_expert_b, jnp.zeros((LANES - n_logits,), F32)]).reshape(1, LANES)
    rows, gsel = _mid_call(
        x, ypool, yattn, w_out[:POOL_WIDTH].astype(BF16), w_out[POOL_WIDTH:].astype(BF16), row(xattn_norm),
        xattn_wq.astype(BF16), row(xattn_q_norm), kmem, vmem, xattn_wo.astype(BF16), row(ffn_norm), wr, br,
        tm=tm)

    n_tok = B * S
    n_sorted = n_tok + N_GROUPS * tr
    pos, meta = _route_call(gsel.reshape(n_tok // tm, tm), tr=tr)
    pos = pos.reshape(n_tok)
    sorted_rows = _scatter_rows_call(pos, rows.reshape(n_tok, D + LANES), n_out=n_sorted, tile=tr)
    sorted_out = _expert_call(
        meta.reshape(LANES), sorted_rows, row(ffn_norm), expert_w_gate.astype(BF16), expert_w_up.astype(BF16),
        expert_w_down.astype(BF16), tr=tr)
    out = _gather_rows_call(pos, sorted_out, tile=tr)
    return out.reshape(B, S, D)


def kernel(x, mem, mix_norm, w_in, pool_w, pool_scale, q_norm, k_norm, idx_k_norm, w_out, xattn_norm, mem_norm,
           xattn_wq, xattn_wkv, xattn_q_norm, xattn_k_norm, xattn_wo, ffn_norm, router_group_w, router_group_b,
           router_expert_w, router_expert_b, expert_w_gate, expert_w_up, expert_w_down):
    depth = mix_norm.shape[0]
    for l in range(depth):
        x = _layer(
            x, mem, mix_norm[l], w_in[l], pool_w[l], pool_scale[l], q_norm[l], k_norm[l], idx_k_norm[l], w_out[l],
            xattn_norm[l], mem_norm[l], xattn_wq[l], xattn_wkv[l], xattn_q_norm[l], xattn_k_norm[l], xattn_wo[l],
            ffn_norm[l], router_group_w[l], router_group_b[l], router_expert_w[l], router_expert_b[l],
            expert_w_gate[l], expert_w_up[l], expert_w_down[l], tm=512, tq=512, tr=512)
    return x
```

```python
import functools
import math

import jax
import jax.numpy as jnp
from jax import lax
from jax.experimental import pallas as pl
from jax.experimental.pallas import tpu as pltpu

CHUNK = 64
POOL_WINDOWS = (2, 4, 8, 16)
POOL_GROUP = 128
POOL_WIDTH = POOL_GROUP * len(POOL_WINDOWS)
ATTN_HEADS = 8
HEAD_DIM = 64
ATTN_WIDTH = ATTN_HEADS * HEAD_DIM
IDX_HEADS = 8
IDX_DIM = 64
TOPK_MAX = 256
ROPE_THETA = 10000.0
MEM_HEADS = 4
MEM_HEAD_DIM = 128
MEM_WIDTH = MEM_HEADS * MEM_HEAD_DIM
N_GROUPS = 4
EXPERTS_PER_GROUP = 8
N_EXPERTS = N_GROUPS * EXPERTS_PER_GROUP
EPS = 1e-6
OFF_Q = POOL_WIDTH
OFF_K = OFF_Q + ATTN_WIDTH
OFF_V = OFF_K + ATTN_WIDTH
OFF_QI = OFF_V + ATTN_WIDTH
OFF_KI = OFF_QI + IDX_HEADS * IDX_DIM
OFF_WI = OFF_KI + IDX_DIM
IN_COLS = OFF_WI + IDX_HEADS

LANES = 128
SUBLANES = 8
VMEM_LIMIT_BYTES = 56 * 1024 * 1024

HALO = 16
PACKED_SUBLANES = 2 * SUBLANES
INT_MIN = -(2 ** 31)
HALF_MIN = -(2 ** 15)
NEG_BIG = -1e30
Q_SCALE = HEAD_DIM ** -0.5 * math.log2(math.e)

F32 = jnp.float32
BF16 = jnp.bfloat16
I32 = jnp.int32
I16 = jnp.int16


def _dot(a, b):
    return jnp.dot(a, b, preferred_element_type=F32)


def _dot_nt(a, b):
    return lax.dot_general(a, b, (((1,), (1,)), ((), ())), preferred_element_type=F32)


def _rms_rows(x, g):
    ms = jnp.mean(x * x, axis=-1, keepdims=True)
    return x * lax.rsqrt(ms + EPS) * g


ROW_Q = 0
ROW_K = ROW_Q + ATTN_WIDTH
ROW_V = ROW_K + ATTN_WIDTH
ROW_QI = ROW_V + ATTN_WIDTH
ROW_KI = ROW_QI + IDX_HEADS * IDX_DIM
ROW_WI = ROW_KI + LANES
ROWS_T = ROW_WI + 2 * SUBLANES


def _norm_rope_t(z, gain, cos, sin_signed):
    if gain is not None:
        ms = jnp.mean(z * z, axis=0, keepdims=True)
        z = z * lax.rsqrt(ms + EPS) * gain
    half = z.shape[0] // 2
    swapped = jnp.concatenate([z[half:], z[:half]], axis=0)
    return z * cos + swapped * sin_signed


def _proj_kernel(x_ref, g_ref, wu_ref, wt_ref, poolw_ref, pscale_ref, qg_ref, kg_ref, ig_ref,
                 cos_ref, sin_ref,
                 ypool_ref, qt_ref, k_ref, vt_ref, qit_ref, ki_ref, wit_ref,
                 halo_ref, ext_ref, kt_ref, *, idx_scale):
    tm = x_ref.shape[0]
    h = _rms_rows(x_ref[...], g_ref[...]).astype(BF16)
    cos = cos_ref[...]
    sin = sin_ref[...]

    u = _dot(h, wu_ref[...])
    @pl.when(pl.program_id(1) == 0)
    def _():
        halo_ref[...] = jnp.zeros_like(halo_ref)

    ext_ref[0:HALO, :] = halo_ref[...]
    ext_ref[HALO:HALO + tm, :] = u
    halo_ref[...] = u[tm - HALO:, :]
    pos1 = pl.program_id(1) * tm + lax.broadcasted_iota(I32, (tm, 1), 0) + 1
    for g, w in enumerate(POOL_WINDOWS):
        cols = slice(g * POOL_GROUP, (g + 1) * POOL_GROUP)
        win = u[:, cols]
        for j in range(1, w):
            win = win + ext_ref[HALO - j:HALO - j + tm, cols]
        cnt = jnp.minimum(pos1, w).astype(F32)
        mixed = (win / cnt - u[:, cols]).astype(BF16)
        y = _dot(mixed, poolw_ref[g]) * pscale_ref[:, cols]
        ypool_ref[:, cols] = y.astype(ypool_ref.dtype)

    qg = qg_ref[...]
    kg = kg_ref[...]
    zq = _dot_nt(wt_ref[ROW_Q:ROW_K, :], h)
    for hd in range(ATTN_HEADS):
        rows = slice(hd * HEAD_DIM, (hd + 1) * HEAD_DIM)
        qt_ref[rows, :] = (_norm_rope_t(zq[rows], qg, cos, sin) * Q_SCALE).astype(qt_ref.dtype)
    zk = _dot_nt(wt_ref[ROW_K:ROW_V, :], h)
    for hd in range(ATTN_HEADS):
        rows = slice(hd * HEAD_DIM, (hd + 1) * HEAD_DIM)
        kt_ref[rows, :] = _norm_rope_t(zk[rows], kg, cos, sin)
    k_ref[...] = kt_ref[...].T.astype(k_ref.dtype)
    vt_ref[...] = _dot_nt(wt_ref[ROW_V:ROW_QI, :], h).astype(vt_ref.dtype)
    zqi = _dot_nt(wt_ref[ROW_QI:ROW_KI, :], h)
    for hd in range(IDX_HEADS):
        rows = slice(hd * IDX_DIM, (hd + 1) * IDX_DIM)
        qit_ref[rows, :] = _norm_rope_t(zqi[rows], None, cos, sin).astype(qit_ref.dtype)
    zi = _dot_nt(wt_ref[ROW_KI:ROWS_T, :], h)
    kit = _norm_rope_t(zi[0:IDX_DIM], ig_ref[...], cos, sin)
    kit = jnp.concatenate([kit, jnp.zeros((LANES - IDX_DIM, tm), F32)], axis=0)
    ki_ref[...] = kit.T.astype(ki_ref.dtype)
    wit_ref[...] = zi[LANES:LANES + IDX_HEADS] * idx_scale


def _proj_call(x, g, wu, wt, poolw, pscale, qg, kg, ig, cos_t, sin_t, *, tm):
    B, S, D = x.shape
    nt = S // tm
    kernel = functools.partial(_proj_kernel, idx_scale=(IDX_DIM ** -0.5) * (IDX_HEADS ** -0.5))
    full = lambda shape: pl.BlockSpec(shape, lambda b, i: (0,) * len(shape))
    out_shape = (
        jax.ShapeDtypeStruct((B, S, POOL_WIDTH), BF16),
        jax.ShapeDtypeStruct((B, ATTN_WIDTH, S), BF16),
        jax.ShapeDtypeStruct((B, S, ATTN_WIDTH), BF16),
        jax.ShapeDtypeStruct((B, nt, ATTN_WIDTH, tm), BF16),
        jax.ShapeDtypeStruct((B, IDX_HEADS * IDX_DIM, S), BF16),
        jax.ShapeDtypeStruct((B, S, LANES), BF16),
        jax.ShapeDtypeStruct((B, IDX_HEADS, S), F32),
    )
    return pl.pallas_call(
        kernel,
        grid=(B, nt),
        in_specs=[
            pl.BlockSpec((None, tm, D), lambda b, i: (b, i, 0)),
            full(g.shape), full(wu.shape), full(wt.shape), full(poolw.shape), full(pscale.shape),
            full(qg.shape), full(kg.shape), full(ig.shape),
            pl.BlockSpec((HEAD_DIM, tm), lambda b, i: (0, i)),
            pl.BlockSpec((HEAD_DIM, tm), lambda b, i: (0, i)),
        ],
        out_specs=(
            pl.BlockSpec((None, tm, POOL_WIDTH), lambda b, i: (b, i, 0)),
            pl.BlockSpec((None, ATTN_WIDTH, tm), lambda b, i: (b, 0, i)),
            pl.BlockSpec((None, tm, ATTN_WIDTH), lambda b, i: (b, i, 0)),
            pl.BlockSpec((None, None, ATTN_WIDTH, tm), lambda b, i: (b, i, 0, 0)),
            pl.BlockSpec((None, IDX_HEADS * IDX_DIM, tm), lambda b, i: (b, 0, i)),
            pl.BlockSpec((None, tm, LANES), lambda b, i: (b, i, 0)),
            pl.BlockSpec((None, IDX_HEADS, tm), lambda b, i: (b, 0, i)),
        ),
        out_shape=out_shape,
        scratch_shapes=[
            pltpu.VMEM((HALO, POOL_WIDTH), F32),
            pltpu.VMEM((HALO + tm, POOL_WIDTH), F32),
            pltpu.VMEM((ATTN_WIDTH, tm), F32),
        ],
        compiler_params=pltpu.CompilerParams(
            dimension_semantics=("arbitrary", "arbitrary"), vmem_limit_bytes=VMEM_LIMIT_BYTES),
    )(x, g, wu, wt, poolw, pscale, qg, kg, ig, cos_t, sin_t)


def _sortable_key(score):
    bits = lax.bitcast_convert_type(score, I32)
    key = jnp.where(bits < 0, bits ^ jnp.int32(0x7FFFFFFF), bits)
    return jnp.where(key == -1, 0, key)


def _colsum8(x):
    rows, t = x.shape
    return jnp.sum(x.reshape(rows // SUBLANES, SUBLANES, t), axis=0)


def _dsa_kernel(qit_ref, wit_ref, qt_ref, ki_ref, k_ref, vt_ref, out_ref, sc_ref, half_ref, ot_ref, m_ref, l_ref,
                *, topk):
    tq = qit_ref.shape[1]
    nkt_max, tk, _ = sc_ref.shape
    q0 = pl.program_id(1) * tq
    nkt = (q0 + tq + tk - 1) // tk
    qpos = q0 + lax.broadcasted_iota(I32, (1, tq), 1)
    qend = (qpos // CHUNK + 1) * CHUNK
    krow = lax.broadcasted_iota(I32, (tk, 1), 0)
    zeros_half = jnp.zeros((LANES - IDX_DIM, tq), BF16)

    def score_tile(kt, carry):
        ki_t = ki_ref[pl.ds(pl.multiple_of(kt * tk, tk), tk), :]
        acc = jnp.zeros((tk, tq), F32)
        for hd in range(IDX_HEADS):
            qh = jnp.concatenate([qit_ref[hd * IDX_DIM:(hd + 1) * IDX_DIM, :], zeros_half], axis=0)
            rel = jnp.maximum(_dot(ki_t, qh), 0.0)
            acc = acc + rel * wit_ref[hd:hd + 1, :]
        key = jnp.where(kt * tk + krow < qend, _sortable_key(acc), INT_MIN)
        sc_ref[kt] = key
        half_ref[kt] = lax.shift_right_arithmetic(key, 16).astype(I16)
        return carry

    lax.fori_loop(0, nkt, score_tile, 0)

    def count(pred_fn):
        def body(kt, c8):
            return c8 + _colsum8(jnp.where(pred_fn(sc_ref[kt], kt), 1, 0).astype(I32))
        c8 = lax.fori_loop(0, nkt, body, jnp.zeros((SUBLANES, tq), I32))
        return jnp.sum(c8, axis=0, keepdims=True)

    def count_half(pred_fn):
        def body(kt, c16):
            ones = jnp.where(pred_fn(half_ref[kt]), jnp.int16(1), jnp.int16(0))
            for j in range(tk // PACKED_SUBLANES):
                c16 = c16 + ones[j * PACKED_SUBLANES:(j + 1) * PACKED_SUBLANES]
            return c16
        c16 = lax.fori_loop(0, nkt, body, jnp.zeros((PACKED_SUBLANES, tq), I16))
        return jnp.sum(c16.astype(I32), axis=0, keepdims=True)

    def bisect_half():
        def step(i, t):
            cand = t + lax.shift_left(jnp.int32(1), 15 - i)
            cand16 = cand.astype(I16)
            c = count_half(lambda half: half >= cand16)
            return jnp.where(c >= topk, cand, t)
        return lax.fori_loop(0, 16, step, jnp.full((1, tq), HALF_MIN, I32))

    thr_hi = bisect_half()

    def low_tile(kt, carry):
        hi = half_ref[kt].astype(I32)
        lo = (sc_ref[kt] & 0xFFFF) + HALF_MIN
        lo = jnp.where(hi == thr_hi, lo, jnp.where(hi > thr_hi, -HALF_MIN - 1, HALF_MIN))
        half_ref[kt] = lo.astype(I16)
        return carry

    lax.fori_loop(0, nkt, low_tile, 0)
    thr_lo = bisect_half()
    thr = thr_hi * 65536 + (thr_lo - HALF_MIN)

    thr_adm = jnp.maximum(thr, INT_MIN + 1)
    n_sel = count(lambda key, kt: key >= thr_adm)
    excess_ties = jnp.max(n_sel) > topk

    def store_mask(sel_fn):
        def body(kt, carry):
            sel = sel_fn(sc_ref[kt], kt)
            sc_ref[kt] = lax.bitcast_convert_type(jnp.where(sel, 0.0, NEG_BIG).astype(F32), I32)
            return carry
        lax.fori_loop(0, nkt, body, 0)

    @pl.when(jnp.logical_not(excess_ties))
    def _():
        store_mask(lambda key, kt: key >= thr_adm)

    @pl.when(excess_ties)
    def _():
        need = topk - count(lambda key, kt: key > thr)
        nbits = max(1, (nkt_max * tk - 1).bit_length())

        def tie_step(step, cut):
            cand = cut + lax.shift_left(jnp.int32(1), nbits - 1 - step)
            c = count(lambda key, kt: (key == thr) & (kt * tk + krow < cand))
            return jnp.where(c < need, cand, cut)

        cut = lax.fori_loop(0, nbits, tie_step, jnp.zeros((1, tq), I32))
        cut = jnp.where(thr == INT_MIN, -1, cut)
        store_mask(lambda key, kt: (key > thr) | ((key == thr) & (kt * tk + krow <= cut)))

    zeros_head = jnp.zeros((HEAD_DIM, tq), BF16)
    ones_rows = jnp.ones((PACKED_SUBLANES, tk), BF16)

    def masked_logits(kt, hd):
        qh = qt_ref[hd * HEAD_DIM:(hd + 1) * HEAD_DIM, :]
        qh = jnp.concatenate([qh, zeros_head] if hd % 2 == 0 else [zeros_head, qh], axis=0)
        k2 = k_ref[pl.ds(pl.multiple_of(kt * tk, tk), tk), (hd // 2) * LANES:(hd // 2 + 1) * LANES]
        return _dot(k2, qh) + lax.bitcast_convert_type(sc_ref[kt], F32)

    def weighted_values(kt, hd, p):
        pv = _dot(jnp.concatenate([vt_ref[kt, hd * HEAD_DIM:(hd + 1) * HEAD_DIM, :], ones_rows], axis=0), p)
        return pv[:HEAD_DIM, :], pv[HEAD_DIM:HEAD_DIM + 1, :]

    def attn_tile_plain(kt, carry):
        for hd in range(ATTN_HEADS):
            rows = slice(hd * HEAD_DIM, (hd + 1) * HEAD_DIM)
            o, l = weighted_values(kt, hd, jnp.exp2(masked_logits(kt, hd)).astype(BF16))
            l_ref[hd:hd + 1, :] += l
            ot_ref[rows, :] += o
        return carry

    def attn_tile_online(kt, carry):
        for hd in range(ATTN_HEADS):
            rows = slice(hd * HEAD_DIM, (hd + 1) * HEAD_DIM)
            s = masked_logits(kt, hd)
            m_old = m_ref[hd:hd + 1, :]
            m_new = jnp.maximum(m_old, jnp.max(s, axis=0, keepdims=True))
            alpha = jnp.exp2(m_old - m_new)
            m_ref[hd:hd + 1, :] = m_new
            o, l = weighted_values(kt, hd, jnp.exp2(s - m_new).astype(BF16))
            l_ref[hd:hd + 1, :] = alpha * l_ref[hd:hd + 1, :] + l
            ot_ref[rows, :] = alpha * ot_ref[rows, :] + o
        return carry

    l_ref[...] = jnp.zeros(l_ref.shape, F32)
    ot_ref[...] = jnp.zeros(ot_ref.shape, F32)
    lax.fori_loop(0, nkt, attn_tile_plain, 0)
    l_all = l_ref[...]
    unusable = jnp.where((l_all > 0.0) & (l_all < jnp.finfo(F32).max), 0, 1)

    @pl.when(jnp.max(unusable) > 0)
    def _():
        m_ref[...] = jnp.full(m_ref.shape, NEG_BIG, F32)
        l_ref[...] = jnp.zeros(l_ref.shape, F32)
        ot_ref[...] = jnp.zeros(ot_ref.shape, F32)
        lax.fori_loop(0, nkt, attn_tile_online, 0)

    for hd in range(ATTN_HEADS):
        rows = slice(hd * HEAD_DIM, (hd + 1) * HEAD_DIM)
        ot_ref[rows, :] = ot_ref[rows, :] / l_ref[hd:hd + 1, :]
    out_ref[...] = ot_ref[...].T.astype(out_ref.dtype)


def _dsa_call(qit, wit, qt, ki, k, vt, *, tq, topk):
    B, S, _ = k.shape
    _, nkt, _, tk = vt.shape
    kernel = functools.partial(_dsa_kernel, topk=topk)
    return pl.pallas_call(
        kernel,
        grid=(B, S // tq),
        in_specs=[
            pl.BlockSpec((None, IDX_HEADS * IDX_DIM, tq), lambda b, i: (b, 0, i)),
            pl.BlockSpec((None, IDX_HEADS, tq), lambda b, i: (b, 0, i)),
            pl.BlockSpec((None, ATTN_WIDTH, tq), lambda b, i: (b, 0, i)),
            pl.BlockSpec((None, S, LANES), lambda b, i: (b, 0, 0)),
            pl.BlockSpec((None, S, ATTN_WIDTH), lambda b, i: (b, 0, 0)),
            pl.BlockSpec((None, nkt, ATTN_WIDTH, tk), lambda b, i: (b, 0, 0, 0)),
        ],
        out_specs=pl.BlockSpec((None, tq, ATTN_WIDTH), lambda b, i: (b, i, 0)),
        out_shape=jax.ShapeDtypeStruct((B, S, ATTN_WIDTH), BF16),
        scratch_shapes=[
            pltpu.VMEM((nkt, tk, tq), I32),
            pltpu.VMEM((nkt, tk, tq), I16),
            pltpu.VMEM((ATTN_WIDTH, tq), F32),
            pltpu.VMEM((ATTN_HEADS, tq), F32),
            pltpu.VMEM((ATTN_HEADS, tq), F32),
        ],
        compiler_params=pltpu.CompilerParams(
            dimension_semantics=("arbitrary", "arbitrary"), vmem_limit_bytes=VMEM_LIMIT_BYTES),
    )(qit, wit, qt, ki, k, vt)


def _memkv_kernel(mem_ref, g_ref, wkv_ref, kn_ref, k_ref, v_ref):
    m = _rms_rows(mem_ref[...], g_ref[...]).astype(BF16)
    kv = _dot(m, wkv_ref[...])
    for hd in range(MEM_HEADS):
        cols = slice(hd * MEM_HEAD_DIM, (hd + 1) * MEM_HEAD_DIM)
        k_ref[:, cols] = _rms_rows(kv[:, cols], kn_ref[...]).astype(k_ref.dtype)
    v_ref[...] = kv[:, MEM_WIDTH:].astype(v_ref.dtype)


def _memkv_call(mem, g, wkv, kn):
    B, M, D = mem.shape
    full = lambda shape: pl.BlockSpec(shape, lambda b: (0,) * len(shape))
    return pl.pallas_call(
        _memkv_kernel,
        grid=(B,),
        in_specs=[pl.BlockSpec((None, M, D), lambda b: (b, 0, 0)), full(g.shape), full(wkv.shape), full(kn.shape)],
        out_specs=(pl.BlockSpec((None, M, MEM_WIDTH), lambda b: (b, 0, 0)),
                   pl.BlockSpec((None, M, MEM_WIDTH), lambda b: (b, 0, 0))),
        out_shape=(jax.ShapeDtypeStruct((B, M, MEM_WIDTH), BF16), jax.ShapeDtypeStruct((B, M, MEM_WIDTH), BF16)),
        compiler_params=pltpu.CompilerParams(dimension_semantics=("arbitrary",), vmem_limit_bytes=VMEM_LIMIT_BYTES),
    )(mem, g, wkv, kn)


def _split_bf16(a):
    hi = a.astype(BF16)
    lo = (a - hi.astype(F32)).astype(BF16)
    return hi, lo


def _router_logits(x2, gf, wr, br):
    h3 = _rms_rows(x2, gf)
    h_hi, h_lo = _split_bf16(h3)
    w_hi, w_lo = _split_bf16(wr)
    return h3, _dot(h_hi, w_hi) + (_dot(h_hi, w_lo) + _dot(h_lo, w_hi)) + br


def _store_token_tiles(ref, x):
    for c in range(ref.shape[1]):
        ref[:, c, :] = x[:, c * LANES:(c + 1) * LANES]


def _load_token_tiles(ref):
    return jnp.concatenate([ref[:, c, :] for c in range(ref.shape[1])], axis=-1)


def _mid_kernel(x_ref, yp_ref, ya_ref, wo1_ref, wo2_ref, gx_ref, wq_ref, qn_ref, km_ref, vm_ref, wo_ref,
                gf_ref, wr_ref, br_ref, rows_ref, gsel_ref):
    tm, d = x_ref.shape
    x1 = x_ref[...] + _dot(yp_ref[...], wo1_ref[...]) + _dot(ya_ref[...], wo2_ref[...])

    h = _rms_rows(x1, gx_ref[...]).astype(BF16)
    q = _dot(h, wq_ref[...])
    heads = []
    for hd in range(MEM_HEADS):
        cols = slice(hd * MEM_HEAD_DIM, (hd + 1) * MEM_HEAD_DIM)
        qh = (_rms_rows(q[:, cols], qn_ref[...]) * (MEM_HEAD_DIM ** -0.5)).astype(BF16)
        s = _dot_nt(qh, km_ref[:, cols])
        p = jnp.exp(s - jnp.max(s, axis=-1, keepdims=True))
        p = p / jnp.sum(p, axis=-1, keepdims=True)
        heads.append(_dot(p.astype(BF16), vm_ref[:, cols]))
    o = jnp.concatenate(heads, axis=-1).astype(BF16)
    x2 = x1 + _dot(o, wo_ref[...])
    _store_token_tiles(rows_ref, x2)

    _, logits = _router_logits(x2, gf_ref[...], wr_ref[...], br_ref[...])
    lane = lax.broadcasted_iota(I32, (tm, LANES), 1).astype(F32)
    g_logit = jnp.where(lane < N_GROUPS, logits, -jnp.inf)
    g_max = jnp.max(g_logit, axis=-1, keepdims=True)
    g_sel = jnp.min(jnp.where(g_logit == g_max, lane, LANES), axis=-1, keepdims=True)
    gsel_ref[...] = jnp.broadcast_to(g_sel, (tm, LANES)).T[0:1, :]


def _mid_call(x, ypool, yattn, wo1, wo2, gx, wq, qn, kmem, vmem, wo, gf, wr, br, *, tm):
    B, S, D = x.shape
    M = kmem.shape[1]
    nt = S // tm
    full = lambda shape: pl.BlockSpec(shape, lambda b, i: (0,) * len(shape))
    tile = lambda width: pl.BlockSpec((None, tm, width), lambda b, i: (b, i, 0))
    return pl.pallas_call(
        _mid_kernel,
        grid=(B, nt),
        in_specs=[tile(D), tile(POOL_WIDTH), tile(ATTN_WIDTH), full(wo1.shape), full(wo2.shape), full(gx.shape),
                  full(wq.shape), full(qn.shape),
                  pl.BlockSpec((None, M, MEM_WIDTH), lambda b, i: (b, 0, 0)),
                  pl.BlockSpec((None, M, MEM_WIDTH), lambda b, i: (b, 0, 0)),
                  full(wo.shape), full(gf.shape), full(wr.shape), full(br.shape)],
        out_specs=(pl.BlockSpec((None, tm, D // LANES, LANES), lambda b, i: (b, i, 0, 0)),
                   pl.BlockSpec((None, 1, tm), lambda b, i: (b * nt + i, 0, 0))),
        out_shape=(jax.ShapeDtypeStruct((B, S, D // LANES, LANES), F32),
                   jax.ShapeDtypeStruct((B * nt, 1, tm), F32)),
        compiler_params=pltpu.CompilerParams(
            dimension_semantics=("arbitrary", "arbitrary"), vmem_limit_bytes=VMEM_LIMIT_BYTES),
    )(x, ypool, yattn, wo1, wo2, gx, wq, qn, kmem, vmem, wo, gf, wr, br)


META_NUSED = 64
ROW_DMA_UNROLL = 8


def _route_kernel(gsel_ref, pos_ref, meta_ref, *, tr):
    nb, w = gsel_ref.shape
    gsel = gsel_ref[...]
    lane = lax.broadcasted_iota(I32, (1, LANES), 1)
    before = (lax.broadcasted_iota(I32, (w, w), 0) < lax.broadcasted_iota(I32, (w, w), 1)).astype(BF16)
    onehot = [(gsel == g).astype(F32) for g in range(N_GROUPS)]
    cnt = sum(jnp.where(lane == g, jnp.sum(onehot[g], axis=1, keepdims=True), 0.0) for g in range(N_GROUPS))
    run = jnp.zeros((1, LANES), F32)
    carries = []
    for b in range(nb):
        carries.append(run)
        run = run + cnt[b:b + 1]
    carry = jnp.concatenate(carries, axis=0)
    shift = tr.bit_length() - 1
    padded = lax.shift_left(lax.shift_right_logical(run.astype(I32) + (tr - 1), shift), shift)
    size = [jnp.sum(jnp.where(lane == g, padded, 0), axis=1, keepdims=True) for g in range(N_GROUPS)]
    start = [sum(size[:g], jnp.zeros((1, 1), I32)) for g in range(N_GROUPS)]
    total = sum(size, jnp.zeros((1, 1), I32))
    pos = jnp.zeros((nb, w), F32)
    for g in range(N_GROUPS):
        rank = _dot(onehot[g].astype(BF16), before)
        base = jnp.sum(jnp.where(lane == g, carry, 0.0), axis=1, keepdims=True) + start[g].astype(F32)
        pos = pos + onehot[g] * (rank + base)
    pos_ref[...] = pos.astype(I32)
    tile_row = lane * tr
    tile_group = jnp.zeros((1, LANES), I32)
    last_group = jnp.zeros((1, 1), I32)
    for g in range(N_GROUPS):
        tile_group = tile_group + jnp.where((tile_row >= start[g]) & (tile_row < start[g] + size[g]), g, 0)
        last_group = jnp.where(size[g] > 0, g, last_group)
    tile_group = jnp.where(tile_row < total, tile_group, last_group)
    meta_ref[...] = jnp.where(lane == META_NUSED, lax.shift_right_logical(total, shift), tile_group)


def _route_call(gsel, *, tr):
    nb, w = gsel.shape
    return pl.pallas_call(
        functools.partial(_route_kernel, tr=tr),
        out_shape=(jax.ShapeDtypeStruct((nb, w), I32), jax.ShapeDtypeStruct((1, LANES), I32)),
        compiler_params=pltpu.CompilerParams(vmem_limit_bytes=VMEM_LIMIT_BYTES),
    )(gsel)


def _scatter_tokens_kernel(pos_ref, src_ref, zeros_ref, dst_ref, sem):
    del zeros_ref
    tile = src_ref.shape[0]
    base = pl.program_id(0) * tile

    def body(i, carry):
        for u in range(ROW_DMA_UNROLL):
            r = i * ROW_DMA_UNROLL + u
            pltpu.make_async_copy(src_ref.at[r], dst_ref.at[pos_ref[base + r]], sem).start(priority=u % 2)
        return carry

    lax.fori_loop(0, tile // ROW_DMA_UNROLL, body, 0)
    pltpu.make_async_copy(src_ref, dst_ref.at[pl.ds(0, tile)], sem).wait()


def _scatter_tokens_call(pos, src, *, n_out, tile):
    n = src.shape[0]
    token = src.shape[1:]
    any_spec = pl.BlockSpec(memory_space=pl.ANY)
    return pl.pallas_call(
        _scatter_tokens_kernel,
        grid_spec=pltpu.PrefetchScalarGridSpec(
            num_scalar_prefetch=1, grid=(n // tile,),
            in_specs=[pl.BlockSpec((tile,) + token, lambda i, pos: (i, 0, 0)), any_spec], out_specs=any_spec,
            scratch_shapes=[pltpu.SemaphoreType.DMA(())]),
        out_shape=jax.ShapeDtypeStruct((n_out,) + token, src.dtype),
        input_output_aliases={2: 0},
        compiler_params=pltpu.CompilerParams(dimension_semantics=("arbitrary",), has_side_effects=True),
    )(pos, src, jnp.zeros((n_out,) + token, src.dtype))


def _gather_tokens_kernel(pos_ref, src_ref, out_ref, buf_ref, sem):
    tile = out_ref.shape[0]
    base = pl.program_id(0) * tile

    def body(i, carry):
        for u in range(ROW_DMA_UNROLL):
            r = i * ROW_DMA_UNROLL + u
            pltpu.make_async_copy(src_ref.at[pos_ref[base + r]], buf_ref.at[r], sem).start(priority=u % 2)
        return carry

    lax.fori_loop(0, tile // ROW_DMA_UNROLL, body, 0)
    pltpu.make_async_copy(src_ref.at[pl.ds(0, tile)], buf_ref, sem).wait()
    out_ref[...] = _load_token_tiles(buf_ref)


def _gather_tokens_call(pos, src, *, tile):
    n = pos.shape[0]
    token = src.shape[1:]
    width = token[0] * token[1]
    return pl.pallas_call(
        _gather_tokens_kernel,
        grid_spec=pltpu.PrefetchScalarGridSpec(
            num_scalar_prefetch=1, grid=(n // tile,),
            in_specs=[pl.BlockSpec(memory_space=pl.ANY)],
            out_specs=pl.BlockSpec((tile, width), lambda i, pos: (i, 0)),
            scratch_shapes=[pltpu.VMEM((tile,) + token, src.dtype), pltpu.SemaphoreType.DMA(())]),
        out_shape=jax.ShapeDtypeStruct((n, width), src.dtype),
        compiler_params=pltpu.CompilerParams(dimension_semantics=("arbitrary",)),
    )(pos, src)


def _expert_kernel(meta_ref, rows_ref, gf_ref, wr_ref, br_ref, wg_ref, wu_ref, wd_ref, out_ref):
    j = pl.program_id(0)
    tr = out_ref.shape[0]
    n_exp, d, ff = wg_ref.shape

    @pl.when(j < meta_ref[META_NUSED])
    def _():
        x2 = _load_token_tiles(rows_ref)
        h3, logits = _router_logits(x2, gf_ref[...], wr_ref[...], br_ref[...])
        lane = lax.broadcasted_iota(I32, (tr, LANES), 1).astype(F32)
        neg_inf = -jnp.inf
        group = meta_ref[j].astype(F32)
        g_logit = jnp.where(lane < N_GROUPS, logits, neg_inf)
        g_max = jnp.max(g_logit, axis=-1, keepdims=True)
        g_own = jnp.max(jnp.where(lane == group, logits, neg_inf), axis=-1, keepdims=True)
        g_w = jnp.exp(g_own - g_max) / jnp.sum(jnp.exp(g_logit - g_max), axis=-1, keepdims=True)
        e_lo = N_GROUPS + group * n_exp
        in_group = (lane >= e_lo) & (lane < e_lo + n_exp)
        e_logit = jnp.where(in_group, logits, neg_inf)
        v1 = jnp.max(e_logit, axis=-1, keepdims=True)
        i1 = jnp.min(jnp.where(e_logit == v1, lane, LANES), axis=-1, keepdims=True)
        rest = jnp.where(lane == i1, neg_inf, e_logit)
        v2 = jnp.max(rest, axis=-1, keepdims=True)
        i2 = jnp.min(jnp.where(rest == v2, lane, LANES), axis=-1, keepdims=True)
        e2 = jnp.exp(v2 - v1)
        w1 = g_w / (1.0 + e2)
        w2 = g_w * e2 / (1.0 + e2)

        h = h3.astype(BF16)
        hid = []
        for e in range(n_exp):
            a = _dot(h, wg_ref[e])
            b = _dot(h, wu_ref[e])
            gate = jnp.where(i1 == e_lo + e, w1, 0.0) + jnp.where(i2 == e_lo + e, w2, 0.0)
            hid.append((a * jax.nn.sigmoid(a) * b * gate).astype(BF16))
        hid = jnp.concatenate(hid, axis=-1)
        _store_token_tiles(out_ref, x2 + _dot(hid, wd_ref[...].reshape(n_exp * ff, d)))

    @pl.when(j >= meta_ref[META_NUSED])
    def _():
        out_ref[...] = jnp.zeros_like(out_ref)


def _expert_call(meta, rows, gf, wr, br, wg, wu, wd, *, tr):
    n_rows = rows.shape[0]
    token = rows.shape[1:]
    _, n_exp, d, ff = wg.shape
    full = lambda a: pl.BlockSpec(a.shape, lambda j, meta: (0,) * a.ndim)
    by_group = lambda shape: pl.BlockSpec((None,) + shape, lambda j, meta: (meta[j], 0, 0, 0))
    return pl.pallas_call(
        _expert_kernel,
        grid_spec=pltpu.PrefetchScalarGridSpec(
            num_scalar_prefetch=1, grid=(n_rows // tr,),
            in_specs=[pl.BlockSpec((tr,) + token, lambda j, meta: (j, 0, 0)), full(gf), full(wr), full(br),
                      by_group((n_exp, d, ff)), by_group((n_exp, d, ff)), by_group((n_exp, ff, d))],
            out_specs=pl.BlockSpec((tr,) + token, lambda j, meta: (j, 0, 0))),
        out_shape=jax.ShapeDtypeStruct((n_rows,) + token, F32),
        compiler_params=pltpu.CompilerParams(dimension_semantics=("arbitrary",), vmem_limit_bytes=VMEM_LIMIT_BYTES),
    )(meta, rows, gf, wr, br, wg, wu, wd)


def _rope_tables_t(seq_len, dim):
    inv = ROPE_THETA ** (-jnp.arange(0, dim, 2, dtype=F32) / dim)
    ang = jnp.arange(seq_len, dtype=F32)[:, None] * inv[None, :]
    ang = jnp.concatenate([ang, ang], axis=-1)
    sign = jnp.concatenate([-jnp.ones((dim // 2,), F32), jnp.ones((dim // 2,), F32)])
    return jnp.cos(ang).T, (jnp.sin(ang) * sign[None, :]).T


def _layer(x, mem, mix_norm, w_in, pool_w, pool_scale, q_norm, k_norm, idx_k_norm, w_out,
           xattn_norm, mem_norm, xattn_wq, xattn_wkv, xattn_q_norm, xattn_k_norm, xattn_wo,
           ffn_norm, router_group_w, router_group_b, router_expert_w, router_expert_b,
           expert_w_gate, expert_w_up, expert_w_down, *, tm, tq, tr):
    B, S, D = x.shape
    topk = min(TOPK_MAX, S // 4)
    row = lambda v: v.reshape(1, -1).astype(F32)
    col = lambda v: jnp.broadcast_to(v.astype(F32)[:, None], (v.shape[0], tm))

    wu = w_in[:, :POOL_WIDTH].astype(BF16)
    pad = lambda n: jnp.zeros((n, D), F32)
    wt = jnp.concatenate([
        w_in[:, OFF_Q:OFF_KI].T, w_in[:, OFF_KI:OFF_WI].T, pad(LANES - IDX_DIM),
        w_in[:, OFF_WI:IN_COLS].T, pad(2 * SUBLANES - IDX_HEADS)], axis=0).astype(BF16)
    cos_t, sin_t = _rope_tables_t(S, HEAD_DIM)

    ypool, qt, k, vt, qit, ki, wit = _proj_call(
        x, row(mix_norm), wu, wt, pool_w.astype(BF16), row(pool_scale), col(q_norm), col(k_norm),
        col(idx_k_norm), cos_t, sin_t, tm=tm)
    yattn = _dsa_call(qit, wit, qt, ki, k, vt, tq=tq, topk=topk)
    kmem, vmem = _memkv_call(mem, row(mem_norm), xattn_wkv.astype(BF16), row(xattn_k_norm))

    n_logits = N_GROUPS + N_EXPERTS
    wr = jnp.concatenate([router_group_w, router_expert_w, jnp.zeros((D, LANES - n_logits), F32)], axis=1)
    br = jnp.concatenate([router_group_b, router_expert_b, jnp.zeros((LANES - n_logits,), F32)]).reshape(1, LANES)
    rows, gsel = _mid_call(
        x, ypool, yattn, w_out[:POOL_WIDTH].astype(BF16), w_out[POOL_WIDTH:].astype(BF16), row(xattn_norm),
        xattn_wq.astype(BF16), row(xattn_q_norm), kmem, vmem, xattn_wo.astype(BF16), row(ffn_norm), wr, br,
        tm=tm)

    n_tok = B * S
    n_sorted = n_tok + N_GROUPS * tr
    pos, meta = _route_call(gsel.reshape(n_tok // tm, tm), tr=tr)
    pos = pos.reshape(n_tok)
    sorted_rows = _scatter_tokens_call(pos, rows.reshape(n_tok, D // LANES, LANES), n_out=n_sorted, tile=tr)
    sorted_out = _expert_call(
        meta.reshape(LANES), sorted_rows, row(ffn_norm), wr, br, expert_w_gate.astype(BF16),
        expert_w_up.astype(BF16), expert_w_down.astype(BF16), tr=tr)
    out = _gather_tokens_call(pos, sorted_out, tile=tr)
    return out.reshape(B, S, D)


def kernel(x, mem, mix_norm, w_in, pool_w, pool_scale, q_norm, k_norm, idx_k_norm, w_out, xattn_norm, mem_norm,
           xattn_wq, xattn_wkv, xattn_q_norm, xattn_k_norm, xattn_wo, ffn_norm, router_group_w, router_group_b,
           router_expert_w, router_expert_b, expert_w_gate, expert_w_up, expert_w_down):
    depth = mix_norm.shape[0]
    for l in range(depth):
        x = _layer(
            x, mem, mix_norm[l], w_in[l], pool_w[l], pool_scale[l], q_norm[l], k_norm[l], idx_k_norm[l], w_out[l],
            xattn_norm[l], mem_norm[l], xattn_wq[l], xattn_wkv[l], xattn_q_norm[l], xattn_k_norm[l], xattn_wo[l],
            ffn_norm[l], router_group_w[l], router_group_b[l], router_expert_w[l], router_expert_b[l],
            expert_w_gate[l], expert_w_up[l], expert_w_down[l], tm=512, tq=512, tr=512)
    return x
```

```python
import functools
import math

import jax
import jax.numpy as jnp
from jax import lax
from jax.experimental import pallas as pl
from jax.experimental.pallas import tpu as pltpu

CHUNK = 64
POOL_WINDOWS = (2, 4, 8, 16)
POOL_GROUP = 128
POOL_WIDTH = POOL_GROUP * len(POOL_WINDOWS)
ATTN_HEADS = 8
HEAD_DIM = 64
ATTN_WIDTH = ATTN_HEADS * HEAD_DIM
IDX_HEADS = 8
IDX_DIM = 64
TOPK_MAX = 256
ROPE_THETA = 10000.0
MEM_HEADS = 4
MEM_HEAD_DIM = 128
MEM_WIDTH = MEM_HEADS * MEM_HEAD_DIM
N_GROUPS = 4
EXPERTS_PER_GROUP = 8
N_EXPERTS = N_GROUPS * EXPERTS_PER_GROUP
EPS = 1e-6
OFF_Q = POOL_WIDTH
OFF_K = OFF_Q + ATTN_WIDTH
OFF_V = OFF_K + ATTN_WIDTH
OFF_QI = OFF_V + ATTN_WIDTH
OFF_KI = OFF_QI + IDX_HEADS * IDX_DIM
OFF_WI = OFF_KI + IDX_DIM
IN_COLS = OFF_WI + IDX_HEADS

LANES = 128
SUBLANES = 8
VMEM_LIMIT_BYTES = 56 * 1024 * 1024

HALO = 16
PACKED_SUBLANES = 2 * SUBLANES
INT_MIN = -(2 ** 31)
NEG_BIG = -1e30
Q_SCALE = HEAD_DIM ** -0.5 * math.log2(math.e)

F32 = jnp.float32
BF16 = jnp.bfloat16
I32 = jnp.int32


def _dot(a, b):
    return jnp.dot(a, b, preferred_element_type=F32)


def _dot_nt(a, b):
    return lax.dot_general(a, b, (((1,), (1,)), ((), ())), preferred_element_type=F32)


def _rms_rows(x, g):
    ms = jnp.mean(x * x, axis=-1, keepdims=True)
    return x * lax.rsqrt(ms + EPS) * g


ROW_Q = 0
ROW_K = ROW_Q + ATTN_WIDTH
ROW_V = ROW_K + ATTN_WIDTH
ROW_QI = ROW_V + ATTN_WIDTH
ROW_KI = ROW_QI + IDX_HEADS * IDX_DIM
ROW_WI = ROW_KI + LANES
ROWS_T = ROW_WI + 2 * SUBLANES


def _norm_rope_t(z, gain, cos, sin_signed):
    if gain is not None:
        ms = jnp.mean(z * z, axis=0, keepdims=True)
        z = z * lax.rsqrt(ms + EPS) * gain
    half = z.shape[0] // 2
    swapped = jnp.concatenate([z[half:], z[:half]], axis=0)
    return z * cos + swapped * sin_signed


def _proj_kernel(x_ref, g_ref, wu_ref, wt_ref, poolw_ref, pscale_ref, qg_ref, kg_ref, ig_ref,
                 cos_ref, sin_ref,
                 ypool_ref, qt_ref, k_ref, vt_ref, qit_ref, ki_ref, wit_ref,
                 halo_ref, ext_ref, kt_ref, *, idx_scale):
    tm = x_ref.shape[0]
    h = _rms_rows(x_ref[...], g_ref[...]).astype(BF16)
    cos = cos_ref[...]
    sin = sin_ref[...]

    u = _dot(h, wu_ref[...])
    @pl.when(pl.program_id(1) == 0)
    def _():
        halo_ref[...] = jnp.zeros_like(halo_ref)

    ext_ref[0:HALO, :] = halo_ref[...]
    ext_ref[HALO:HALO + tm, :] = u
    halo_ref[...] = u[tm - HALO:, :]
    pos1 = pl.program_id(1) * tm + lax.broadcasted_iota(I32, (tm, 1), 0) + 1
    for g, w in enumerate(POOL_WINDOWS):
        cols = slice(g * POOL_GROUP, (g + 1) * POOL_GROUP)
        win = u[:, cols]
        for j in range(1, w):
            win = win + ext_ref[HALO - j:HALO - j + tm, cols]
        cnt = jnp.minimum(pos1, w).astype(F32)
        mixed = (win / cnt - u[:, cols]).astype(BF16)
        y = _dot(mixed, poolw_ref[g]) * pscale_ref[:, cols]
        ypool_ref[:, cols] = y.astype(ypool_ref.dtype)

    qg = qg_ref[...]
    kg = kg_ref[...]
    zq = _dot_nt(wt_ref[ROW_Q:ROW_K, :], h)
    for hd in range(ATTN_HEADS):
        rows = slice(hd * HEAD_DIM, (hd + 1) * HEAD_DIM)
        qt_ref[rows, :] = (_norm_rope_t(zq[rows], qg, cos, sin) * Q_SCALE).astype(qt_ref.dtype)
    zk = _dot_nt(wt_ref[ROW_K:ROW_V, :], h)
    for hd in range(ATTN_HEADS):
        rows = slice(hd * HEAD_DIM, (hd + 1) * HEAD_DIM)
        kt_ref[rows, :] = _norm_rope_t(zk[rows], kg, cos, sin)
    k_ref[...] = kt_ref[...].T.astype(k_ref.dtype)
    vt_ref[...] = _dot_nt(wt_ref[ROW_V:ROW_QI, :], h).astype(vt_ref.dtype)
    zqi = _dot_nt(wt_ref[ROW_QI:ROW_KI, :], h)
    for hd in range(IDX_HEADS):
        rows = slice(hd * IDX_DIM, (hd + 1) * IDX_DIM)
        qit_ref[rows, :] = _norm_rope_t(zqi[rows], None, cos, sin).astype(qit_ref.dtype)
    zi = _dot_nt(wt_ref[ROW_KI:ROWS_T, :], h)
    kit = _norm_rope_t(zi[0:IDX_DIM], ig_ref[...], cos, sin)
    kit = jnp.concatenate([kit, jnp.zeros((LANES - IDX_DIM, tm), F32)], axis=0)
    ki_ref[...] = kit.T.astype(ki_ref.dtype)
    wit_ref[...] = zi[LANES:LANES + IDX_HEADS] * idx_scale


def _proj_call(x, g, wu, wt, poolw, pscale, qg, kg, ig, cos_t, sin_t, *, tm):
    B, S, D = x.shape
    nt = S // tm
    kernel = functools.partial(_proj_kernel, idx_scale=(IDX_DIM ** -0.5) * (IDX_HEADS ** -0.5))
    full = lambda shape: pl.BlockSpec(shape, lambda b, i: (0,) * len(shape))
    out_shape = (
        jax.ShapeDtypeStruct((B, S, POOL_WIDTH), BF16),
        jax.ShapeDtypeStruct((B, ATTN_WIDTH, S), BF16),
        jax.ShapeDtypeStruct((B, S, ATTN_WIDTH), BF16),
        jax.ShapeDtypeStruct((B, nt, ATTN_WIDTH, tm), BF16),
        jax.ShapeDtypeStruct((B, IDX_HEADS * IDX_DIM, S), BF16),
        jax.ShapeDtypeStruct((B, S, LANES), BF16),
        jax.ShapeDtypeStruct((B, IDX_HEADS, S), F32),
    )
    return pl.pallas_call(
        kernel,
        grid=(B, nt),
        in_specs=[
            pl.BlockSpec((None, tm, D), lambda b, i: (b, i, 0)),
            full(g.shape), full(wu.shape), full(wt.shape), full(poolw.shape), full(pscale.shape),
            full(qg.shape), full(kg.shape), full(ig.shape),
            pl.BlockSpec((HEAD_DIM, tm), lambda b, i: (0, i)),
            pl.BlockSpec((HEAD_DIM, tm), lambda b, i: (0, i)),
        ],
        out_specs=(
            pl.BlockSpec((None, tm, POOL_WIDTH), lambda b, i: (b, i, 0)),
            pl.BlockSpec((None, ATTN_WIDTH, tm), lambda b, i: (b, 0, i)),
            pl.BlockSpec((None, tm, ATTN_WIDTH), lambda b, i: (b, i, 0)),
            pl.BlockSpec((None, None, ATTN_WIDTH, tm), lambda b, i: (b, i, 0, 0)),
            pl.BlockSpec((None, IDX_HEADS * IDX_DIM, tm), lambda b, i: (b, 0, i)),
            pl.BlockSpec((None, tm, LANES), lambda b, i: (b, i, 0)),
            pl.BlockSpec((None, IDX_HEADS, tm), lambda b, i: (b, 0, i)),
        ),
        out_shape=out_shape,
        scratch_shapes=[
            pltpu.VMEM((HALO, POOL_WIDTH), F32),
            pltpu.VMEM((HALO + tm, POOL_WIDTH), F32),
            pltpu.VMEM((ATTN_WIDTH, tm), F32),
        ],
        compiler_params=pltpu.CompilerParams(
            dimension_semantics=("arbitrary", "arbitrary"), vmem_limit_bytes=VMEM_LIMIT_BYTES),
    )(x, g, wu, wt, poolw, pscale, qg, kg, ig, cos_t, sin_t)


KEY_NEG_INF = -0x7F800000


def _key_to_float(key):
    key = jnp.maximum(key, KEY_NEG_INF)
    bits = jnp.where(key < 0, (key - 1) ^ jnp.int32(0x7FFFFFFF), key)
    return lax.bitcast_convert_type(bits, F32)


def _colsum8(x):
    rows, t = x.shape
    return jnp.sum(x.reshape(rows // SUBLANES, SUBLANES, t), axis=0)


def _dsa_kernel(qit_ref, wit_ref, qt_ref, ki_ref, k_ref, vt_ref, out_ref, sc_ref, ot_ref, m_ref, l_ref, *, topk):
    tq = qit_ref.shape[1]
    nkt_max, tk, _ = sc_ref.shape
    q0 = pl.program_id(1) * tq
    nkt = (q0 + tq + tk - 1) // tk
    qpos = q0 + lax.broadcasted_iota(I32, (1, tq), 1)
    qend = (qpos // CHUNK + 1) * CHUNK
    krow = lax.broadcasted_iota(I32, (tk, 1), 0)
    zeros_half = jnp.zeros((LANES - IDX_DIM, tq), BF16)

    def score_tile(kt, carry):
        ki_t = ki_ref[pl.ds(pl.multiple_of(kt * tk, tk), tk), :]
        acc = jnp.zeros((tk, tq), F32)
        for hd in range(IDX_HEADS):
            qh = jnp.concatenate([qit_ref[hd * IDX_DIM:(hd + 1) * IDX_DIM, :], zeros_half], axis=0)
            rel = jnp.maximum(_dot(ki_t, qh), 0.0)
            acc = acc + rel * wit_ref[hd:hd + 1, :]
        sc_ref[kt] = jnp.where(kt * tk + krow < qend, acc, -jnp.inf)
        return carry

    lax.fori_loop(0, nkt, score_tile, 0)

    def count(pred_fn):
        def body(kt, c8):
            return c8 + _colsum8(jnp.where(pred_fn(sc_ref[kt], kt), 1, 0).astype(I32))
        c8 = lax.fori_loop(0, nkt, body, jnp.zeros((SUBLANES, tq), I32))
        return jnp.sum(c8, axis=0, keepdims=True)

    def bit_step(i, carry):
        t, n = carry
        cand = t + lax.shift_left(jnp.int32(1), 31 - i)
        cand_f = _key_to_float(cand)
        c = count(lambda s, kt: s >= cand_f)
        return jnp.where(c >= topk, cand, t), jnp.where(c >= topk, c, n)

    thr_key, n_sel = lax.fori_loop(
        0, 32, bit_step, (jnp.full((1, tq), INT_MIN, I32), jnp.zeros((1, tq), I32)))
    thr = _key_to_float(thr_key)
    few = thr == -jnp.inf
    thr_adm = jnp.where(few, jnp.finfo(F32).min, thr)
    excess_ties = jnp.max(jnp.where(few, 0, n_sel)) > topk

    def store_mask(sel_fn):
        def body(kt, carry):
            sc_ref[kt] = jnp.where(sel_fn(sc_ref[kt], kt), 0.0, NEG_BIG).astype(F32)
            return carry
        lax.fori_loop(0, nkt, body, 0)

    @pl.when(jnp.logical_not(excess_ties))
    def _():
        store_mask(lambda s, kt: s >= thr_adm)

    @pl.when(excess_ties)
    def _():
        need = topk - count(lambda s, kt: s > thr)
        nbits = max(1, (nkt_max * tk - 1).bit_length())

        def tie_step(step, cut):
            cand = cut + lax.shift_left(jnp.int32(1), nbits - 1 - step)
            c = count(lambda s, kt: (s == thr) & (kt * tk + krow < cand))
            return jnp.where(c < need, cand, cut)

        cut = lax.fori_loop(0, nbits, tie_step, jnp.zeros((1, tq), I32))
        cut = jnp.where(few, -1, cut)
        store_mask(lambda s, kt: (s > thr_adm) | ((s == thr) & (kt * tk + krow <= cut)))

    zeros_head = jnp.zeros((HEAD_DIM, tq), BF16)
    ones_rows = jnp.ones((PACKED_SUBLANES, tk), BF16)

    def masked_logits(kt, hd):
        qh = qt_ref[hd * HEAD_DIM:(hd + 1) * HEAD_DIM, :]
        qh = jnp.concatenate([qh, zeros_head] if hd % 2 == 0 else [zeros_head, qh], axis=0)
        k2 = k_ref[pl.ds(pl.multiple_of(kt * tk, tk), tk), (hd // 2) * LANES:(hd // 2 + 1) * LANES]
        return _dot(k2, qh) + sc_ref[kt]

    def weighted_values(kt, hd, p):
        pv = _dot(jnp.concatenate([vt_ref[kt, hd * HEAD_DIM:(hd + 1) * HEAD_DIM, :], ones_rows], axis=0), p)
        return pv[:HEAD_DIM, :], pv[HEAD_DIM:HEAD_DIM + 1, :]

    def attn_tile_plain(kt, carry):
        for hd in range(ATTN_HEADS):
            rows = slice(hd * HEAD_DIM, (hd + 1) * HEAD_DIM)
            o, l = weighted_values(kt, hd, jnp.exp2(masked_logits(kt, hd)).astype(BF16))
            l_ref[hd:hd + 1, :] += l
            ot_ref[rows, :] += o
        return carry

    def attn_tile_online(kt, carry):
        for hd in range(ATTN_HEADS):
            rows = slice(hd * HEAD_DIM, (hd + 1) * HEAD_DIM)
            s = masked_logits(kt, hd)
            m_old = m_ref[hd:hd + 1, :]
            m_new = jnp.maximum(m_old, jnp.max(s, axis=0, keepdims=True))
            alpha = jnp.exp2(m_old - m_new)
            m_ref[hd:hd + 1, :] = m_new
            o, l = weighted_values(kt, hd, jnp.exp2(s - m_new).astype(BF16))
            l_ref[hd:hd + 1, :] = alpha * l_ref[hd:hd + 1, :] + l
            ot_ref[rows, :] = alpha * ot_ref[rows, :] + o
        return carry

    l_ref[...] = jnp.zeros(l_ref.shape, F32)
    ot_ref[...] = jnp.zeros(ot_ref.shape, F32)
    lax.fori_loop(0, nkt, attn_tile_plain, 0)
    l_all = l_ref[...]
    unusable = jnp.where((l_all > 0.0) & (l_all < jnp.finfo(F32).max), 0, 1)

    @pl.when(jnp.max(unusable) > 0)
    def _():
        m_ref[...] = jnp.full(m_ref.shape, NEG_BIG, F32)
        l_ref[...] = jnp.zeros(l_ref.shape, F32)
        ot_ref[...] = jnp.zeros(ot_ref.shape, F32)
        lax.fori_loop(0, nkt, attn_tile_online, 0)

    for hd in range(ATTN_HEADS):
        rows = slice(hd * HEAD_DIM, (hd + 1) * HEAD_DIM)
        ot_ref[rows, :] = ot_ref[rows, :] / l_ref[hd:hd + 1, :]
    out_ref[...] = ot_ref[...].T.astype(out_ref.dtype)


def _dsa_call(qit, wit, qt, ki, k, vt, *, tq, topk):
    B, S, _ = k.shape
    _, nkt, _, tk = vt.shape
    kernel = functools.partial(_dsa_kernel, topk=topk)
    return pl.pallas_call(
        kernel,
        grid=(B, S // tq),
        in_specs=[
            pl.BlockSpec((None, IDX_HEADS * IDX_DIM, tq), lambda b, i: (b, 0, i)),
            pl.BlockSpec((None, IDX_HEADS, tq), lambda b, i: (b, 0, i)),
            pl.BlockSpec((None, ATTN_WIDTH, tq), lambda b, i: (b, 0, i)),
            pl.BlockSpec((None, S, LANES), lambda b, i: (b, 0, 0)),
            pl.BlockSpec((None, S, ATTN_WIDTH), lambda b, i: (b, 0, 0)),
            pl.BlockSpec((None, nkt, ATTN_WIDTH, tk), lambda b, i: (b, 0, 0, 0)),
        ],
        out_specs=pl.BlockSpec((None, tq, ATTN_WIDTH), lambda b, i: (b, i, 0)),
        out_shape=jax.ShapeDtypeStruct((B, S, ATTN_WIDTH), BF16),
        scratch_shapes=[
            pltpu.VMEM((nkt, tk, tq), F32),
            pltpu.VMEM((ATTN_WIDTH, tq), F32),
            pltpu.VMEM((ATTN_HEADS, tq), F32),
            pltpu.VMEM((ATTN_HEADS, tq), F32),
        ],
        compiler_params=pltpu.CompilerParams(
            dimension_semantics=("arbitrary", "arbitrary"), vmem_limit_bytes=VMEM_LIMIT_BYTES),
    )(qit, wit, qt, ki, k, vt)


def _memkv_kernel(mem_ref, g_ref, wkv_ref, kn_ref, k_ref, v_ref):
    m = _rms_rows(mem_ref[...], g_ref[...]).astype(BF16)
    kv = _dot(m, wkv_ref[...])
    for hd in range(MEM_HEADS):
        cols = slice(hd * MEM_HEAD_DIM, (hd + 1) * MEM_HEAD_DIM)
        k_ref[:, cols] = _rms_rows(kv[:, cols], kn_ref[...]).astype(k_ref.dtype)
    v_ref[...] = kv[:, MEM_WIDTH:].astype(v_ref.dtype)


def _memkv_call(mem, g, wkv, kn):
    B, M, D = mem.shape
    full = lambda shape: pl.BlockSpec(shape, lambda b: (0,) * len(shape))
    return pl.pallas_call(
        _memkv_kernel,
        grid=(B,),
        in_specs=[pl.BlockSpec((None, M, D), lambda b: (b, 0, 0)), full(g.shape), full(wkv.shape), full(kn.shape)],
        out_specs=(pl.BlockSpec((None, M, MEM_WIDTH), lambda b: (b, 0, 0)),
                   pl.BlockSpec((None, M, MEM_WIDTH), lambda b: (b, 0, 0))),
        out_shape=(jax.ShapeDtypeStruct((B, M, MEM_WIDTH), BF16), jax.ShapeDtypeStruct((B, M, MEM_WIDTH), BF16)),
        compiler_params=pltpu.CompilerParams(dimension_semantics=("arbitrary",), vmem_limit_bytes=VMEM_LIMIT_BYTES),
    )(mem, g, wkv, kn)


def _split_bf16(a):
    hi = a.astype(BF16)
    lo = (a - hi.astype(F32)).astype(BF16)
    return hi, lo


def _mid_kernel(x_ref, yp_ref, ya_ref, wo1_ref, wo2_ref, gx_ref, wq_ref, qn_ref, km_ref, vm_ref, wo_ref,
                gf_ref, wr_ref, br_ref, rows_ref, gsel_ref):
    tm, d = x_ref.shape
    x1 = x_ref[...] + _dot(yp_ref[...], wo1_ref[...]) + _dot(ya_ref[...], wo2_ref[...])

    h = _rms_rows(x1, gx_ref[...]).astype(BF16)
    q = _dot(h, wq_ref[...])
    heads = []
    for hd in range(MEM_HEADS):
        cols = slice(hd * MEM_HEAD_DIM, (hd + 1) * MEM_HEAD_DIM)
        qh = (_rms_rows(q[:, cols], qn_ref[...]) * (MEM_HEAD_DIM ** -0.5)).astype(BF16)
        s = _dot_nt(qh, km_ref[:, cols])
        p = jnp.exp(s - jnp.max(s, axis=-1, keepdims=True))
        p = p / jnp.sum(p, axis=-1, keepdims=True)
        heads.append(_dot(p.astype(BF16), vm_ref[:, cols]))
    o = jnp.concatenate(heads, axis=-1).astype(BF16)
    x2 = x1 + _dot(o, wo_ref[...])
    rows_ref[:, :d] = x2

    h_hi, h_lo = _split_bf16(_rms_rows(x2, gf_ref[...]))
    w_hi, w_lo = _split_bf16(wr_ref[...])
    hw = _dot(h_hi, jnp.concatenate([w_hi, w_lo], axis=1))
    logits = hw[:, :LANES] + (hw[:, LANES:] + _dot(h_lo, w_hi)) + br_ref[...]
    lane = lax.broadcasted_iota(I32, (tm, LANES), 1).astype(F32)
    neg_inf = -jnp.inf
    g_logit = jnp.where(lane < N_GROUPS, logits, neg_inf)
    g_max = jnp.max(g_logit, axis=-1, keepdims=True)
    g_sel = jnp.min(jnp.where(g_logit == g_max, lane, LANES), axis=-1, keepdims=True)
    g_w = 1.0 / jnp.sum(jnp.exp(g_logit - g_max), axis=-1, keepdims=True)
    e_lo = N_GROUPS + g_sel * EXPERTS_PER_GROUP
    in_group = (lane >= e_lo) & (lane < e_lo + EXPERTS_PER_GROUP)
    e_logit = jnp.where(in_group, logits, neg_inf)
    v1 = jnp.max(e_logit, axis=-1, keepdims=True)
    i1 = jnp.min(jnp.where(e_logit == v1, lane, LANES), axis=-1, keepdims=True)
    rest = jnp.where(lane == i1, neg_inf, e_logit)
    v2 = jnp.max(rest, axis=-1, keepdims=True)
    i2 = jnp.min(jnp.where(rest == v2, lane, LANES), axis=-1, keepdims=True)
    e2 = jnp.exp(v2 - v1)
    w1 = g_w / (1.0 + e2)
    w2 = g_w * e2 / (1.0 + e2)
    gates = jnp.where(lane == i1 - N_GROUPS, w1, 0.0) + jnp.where(lane == i2 - N_GROUPS, w2, 0.0)
    rows_ref[:, d:] = gates
    gsel_ref[...] = jnp.broadcast_to(g_sel, (tm, LANES)).T[0:1, :]


def _mid_call(x, ypool, yattn, wo1, wo2, gx, wq, qn, kmem, vmem, wo, gf, wr, br, *, tm):
    B, S, D = x.shape
    M = kmem.shape[1]
    nt = S // tm
    full = lambda shape: pl.BlockSpec(shape, lambda b, i: (0,) * len(shape))
    tile = lambda width: pl.BlockSpec((None, tm, width), lambda b, i: (b, i, 0))
    return pl.pallas_call(
        _mid_kernel,
        grid=(B, nt),
        in_specs=[tile(D), tile(POOL_WIDTH), tile(ATTN_WIDTH), full(wo1.shape), full(wo2.shape), full(gx.shape),
                  full(wq.shape), full(qn.shape),
                  pl.BlockSpec((None, M, MEM_WIDTH), lambda b, i: (b, 0, 0)),
                  pl.BlockSpec((None, M, MEM_WIDTH), lambda b, i: (b, 0, 0)),
                  full(wo.shape), full(gf.shape), full(wr.shape), full(br.shape)],
        out_specs=(tile(D + LANES), pl.BlockSpec((None, 1, tm), lambda b, i: (b * nt + i, 0, 0))),
        out_shape=(jax.ShapeDtypeStruct((B, S, D + LANES), F32),
                   jax.ShapeDtypeStruct((B * nt, 1, tm), F32)),
        compiler_params=pltpu.CompilerParams(
            dimension_semantics=("arbitrary", "arbitrary"), vmem_limit_bytes=VMEM_LIMIT_BYTES),
    )(x, ypool, yattn, wo1, wo2, gx, wq, qn, kmem, vmem, wo, gf, wr, br)


META_NUSED = 64
ROW_DMA_UNROLL = 8


def _route_kernel(gsel_ref, pos_ref, meta_ref, *, tr):
    nb, w = gsel_ref.shape
    gsel = gsel_ref[...]
    lane = lax.broadcasted_iota(I32, (1, LANES), 1)
    before = (lax.broadcasted_iota(I32, (w, w), 0) < lax.broadcasted_iota(I32, (w, w), 1)).astype(BF16)
    onehot = [(gsel == g).astype(F32) for g in range(N_GROUPS)]
    cnt = sum(jnp.where(lane == g, jnp.sum(onehot[g], axis=1, keepdims=True), 0.0) for g in range(N_GROUPS))
    run = jnp.zeros((1, LANES), F32)
    carries = []
    for b in range(nb):
        carries.append(run)
        run = run + cnt[b:b + 1]
    carry = jnp.concatenate(carries, axis=0)
    padded = jnp.floor((run + (tr - 1)) * (1.0 / tr)) * tr
    size = [jnp.sum(jnp.where(lane == g, padded, 0.0), axis=1, keepdims=True) for g in range(N_GROUPS)]
    start = [sum(size[:g], jnp.zeros((1, 1), F32)) for g in range(N_GROUPS)]
    total = sum(size, jnp.zeros((1, 1), F32))
    pos = jnp.zeros((nb, w), F32)
    for g in range(N_GROUPS):
        rank = _dot(onehot[g].astype(BF16), before)
        base = jnp.sum(jnp.where(lane == g, carry, 0.0), axis=1, keepdims=True) + start[g]
        pos = pos + onehot[g] * (rank + base)
    pos_ref[...] = pos.astype(I32)
    tile_row = (lane * tr).astype(F32)
    tile_group = jnp.zeros((1, LANES), I32)
    last_group = jnp.zeros((1, 1), I32)
    for g in range(N_GROUPS):
        tile_group = tile_group + jnp.where((tile_row >= start[g]) & (tile_row < start[g] + size[g]), g, 0)
        last_group = jnp.where(size[g] > 0, g, last_group)
    tile_group = jnp.where(tile_row < total, tile_group, last_group)
    meta_ref[...] = jnp.where(lane == META_NUSED, (total * (1.0 / tr)).astype(I32), tile_group)


def _route_call(gsel, *, tr):
    nb, w = gsel.shape
    return pl.pallas_call(
        functools.partial(_route_kernel, tr=tr),
        out_shape=(jax.ShapeDtypeStruct((nb, w), I32), jax.ShapeDtypeStruct((1, LANES), I32)),
        compiler_params=pltpu.CompilerParams(vmem_limit_bytes=VMEM_LIMIT_BYTES),
    )(gsel)


def _scatter_rows_kernel(pos_ref, src_ref, zeros_ref, dst_ref, sem):
    del zeros_ref
    tile = src_ref.shape[0]
    base = pl.program_id(0) * tile

    def body(i, carry):
        for u in range(ROW_DMA_UNROLL):
            r = i * ROW_DMA_UNROLL + u
            pltpu.make_async_copy(
                src_ref.at[pl.ds(r, 1)], dst_ref.at[pl.ds(pos_ref[base + r], 1)], sem).start(priority=u % 2)
        return carry

    lax.fori_loop(0, tile // ROW_DMA_UNROLL, body, 0)
    pltpu.make_async_copy(src_ref, dst_ref.at[pl.ds(0, tile)], sem).wait()


def _scatter_rows_call(pos, src, *, n_out, tile):
    n, width = src.shape
    any_spec = pl.BlockSpec(memory_space=pl.ANY)
    return pl.pallas_call(
        _scatter_rows_kernel,
        grid_spec=pltpu.PrefetchScalarGridSpec(
            num_scalar_prefetch=1, grid=(n // tile,),
            in_specs=[pl.BlockSpec((tile, width), lambda i, pos: (i, 0)), any_spec], out_specs=any_spec,
            scratch_shapes=[pltpu.SemaphoreType.DMA(())]),
        out_shape=jax.ShapeDtypeStruct((n_out, width), src.dtype),
        input_output_aliases={2: 0},
        compiler_params=pltpu.CompilerParams(dimension_semantics=("arbitrary",), has_side_effects=True),
    )(pos, src, jnp.zeros((n_out, width), src.dtype))


def _gather_rows_kernel(pos_ref, src_ref, out_ref, sem):
    tile = out_ref.shape[0]
    base = pl.program_id(0) * tile

    def body(i, carry):
        for u in range(ROW_DMA_UNROLL):
            r = i * ROW_DMA_UNROLL + u
            pltpu.make_async_copy(
                src_ref.at[pl.ds(pos_ref[base + r], 1)], out_ref.at[pl.ds(r, 1)], sem).start(priority=u % 2)
        return carry

    lax.fori_loop(0, tile // ROW_DMA_UNROLL, body, 0)
    pltpu.make_async_copy(src_ref.at[pl.ds(0, tile)], out_ref, sem).wait()


def _gather_rows_call(pos, src, *, tile):
    n = pos.shape[0]
    width = src.shape[1]
    return pl.pallas_call(
        _gather_rows_kernel,
        grid_spec=pltpu.PrefetchScalarGridSpec(
            num_scalar_prefetch=1, grid=(n // tile,),
            in_specs=[pl.BlockSpec(memory_space=pl.ANY)],
            out_specs=pl.BlockSpec((tile, width), lambda i, pos: (i, 0)),
            scratch_shapes=[pltpu.SemaphoreType.DMA(())]),
        out_shape=jax.ShapeDtypeStruct((n, width), src.dtype),
        compiler_params=pltpu.CompilerParams(dimension_semantics=("arbitrary",)),
    )(pos, src)


def _expert_kernel(meta_ref, rows_ref, gf_ref, wg_ref, wu_ref, wd_ref, out_ref):
    j = pl.program_id(0)
    tr, d = out_ref.shape
    n_exp, _, ff = wg_ref.shape

    @pl.when(j < meta_ref[META_NUSED])
    def _():
        x2 = rows_ref[:, :d]
        gates = rows_ref[:, d:]
        h = _rms_rows(x2, gf_ref[...]).astype(BF16)
        lane = lax.broadcasted_iota(I32, (tr, LANES), 1)
        first = meta_ref[j] * n_exp
        hid = []
        for e in range(n_exp):
            a = _dot(h, wg_ref[e])
            b = _dot(h, wu_ref[e])
            gate = jnp.sum(jnp.where(lane == first + e, gates, 0.0), axis=-1, keepdims=True)
            hid.append((a * jax.nn.sigmoid(a) * b * gate).astype(BF16))
        hid = jnp.concatenate(hid, axis=-1)
        out_ref[...] = x2 + _dot(hid, wd_ref[...].reshape(n_exp * ff, d))

    @pl.when(j >= meta_ref[META_NUSED])
    def _():
        out_ref[...] = jnp.zeros_like(out_ref)


def _expert_call(meta, rows, gf, wg, wu, wd, *, tr):
    n_rows, width = rows.shape
    d = width - LANES
    _, n_exp, _, ff = wg.shape
    by_group = lambda shape: pl.BlockSpec((None,) + shape, lambda j, meta: (meta[j], 0, 0, 0))
    return pl.pallas_call(
        _expert_kernel,
        grid_spec=pltpu.PrefetchScalarGridSpec(
            num_scalar_prefetch=1, grid=(n_rows // tr,),
            in_specs=[pl.BlockSpec((tr, width), lambda j, meta: (j, 0)),
                      pl.BlockSpec(gf.shape, lambda j, meta: (0, 0)),
                      by_group((n_exp, d, ff)), by_group((n_exp, d, ff)), by_group((n_exp, ff, d))],
            out_specs=pl.BlockSpec((tr, d), lambda j, meta: (j, 0))),
        out_shape=jax.ShapeDtypeStruct((n_rows, d), F32),
        compiler_params=pltpu.CompilerParams(dimension_semantics=("arbitrary",), vmem_limit_bytes=VMEM_LIMIT_BYTES),
    )(meta, rows, gf, wg, wu, wd)


def _rope_tables_t(seq_len, dim):
    inv = ROPE_THETA ** (-jnp.arange(0, dim, 2, dtype=F32) / dim)
    ang = jnp.arange(seq_len, dtype=F32)[:, None] * inv[None, :]
    ang = jnp.concatenate([ang, ang], axis=-1)
    sign = jnp.concatenate([-jnp.ones((dim // 2,), F32), jnp.ones((dim // 2,), F32)])
    return jnp.cos(ang).T, (jnp.sin(ang) * sign[None, :]).T


def _layer(x, mem, mix_norm, w_in, pool_w, pool_scale, q_norm, k_norm, idx_k_norm, w_out,
           xattn_norm, mem_norm, xattn_wq, xattn_wkv, xattn_q_norm, xattn_k_norm, xattn_wo,
           ffn_norm, router_group_w, router_group_b, router_expert_w, router_expert_b,
           expert_w_gate, expert_w_up, expert_w_down, *, tm, tq, tr):
    B, S, D = x.shape
    topk = min(TOPK_MAX, S // 4)
    row = lambda v: v.reshape(1, -1).astype(F32)
    col = lambda v: jnp.broadcast_to(v.astype(F32)[:, None], (v.shape[0], tm))

    wu = w_in[:, :POOL_WIDTH].astype(BF16)
    pad = lambda n: jnp.zeros((n, D), F32)
    wt = jnp.concatenate([
        w_in[:, OFF_Q:OFF_KI].T, w_in[:, OFF_KI:OFF_WI].T, pad(LANES - IDX_DIM),
        w_in[:, OFF_WI:IN_COLS].T, pad(2 * SUBLANES - IDX_HEADS)], axis=0).astype(BF16)
    cos_t, sin_t = _rope_tables_t(S, HEAD_DIM)

    ypool, qt, k, vt, qit, ki, wit = _proj_call(
        x, row(mix_norm), wu, wt, pool_w.astype(BF16), row(pool_scale), col(q_norm), col(k_norm),
        col(idx_k_norm), cos_t, sin_t, tm=tm)
    yattn = _dsa_call(qit, wit, qt, ki, k, vt, tq=tq, topk=topk)
    kmem, vmem = _memkv_call(mem, row(mem_norm), xattn_wkv.astype(BF16), row(xattn_k_norm))

    n_logits = N_GROUPS + N_EXPERTS
    wr = jnp.concatenate([router_group_w, router_expert_w, jnp.zeros((D, LANES - n_logits), F32)], axis=1)
    br = jnp.concatenate([router_group_b, router_expert_b, jnp.zeros((LANES - n_logits,), F32)]).reshape(1, LANES)
    rows, gsel = _mid_call(
        x, ypool, yattn, w_out[:POOL_WIDTH].astype(BF16), w_out[POOL_WIDTH:].astype(BF16), row(xattn_norm),
        xattn_wq.astype(BF16), row(xattn_q_norm), kmem, vmem, xattn_wo.astype(BF16), row(ffn_norm), wr, br,
        tm=tm)

    n_tok = B * S
    n_sorted = n_tok + N_GROUPS * tr
    pos, meta = _route_call(gsel.reshape(n_tok // tm, tm), tr=tr)
    pos = pos.reshape(n_tok)
    sorted_rows = _scatter_rows_call(pos, rows.reshape(n_tok, D + LANES), n_out=n_sorted, tile=tr)
    sorted_out = _expert_call(
        meta.reshape(LANES), sorted_rows, row(ffn_norm), expert_w_gate.astype(BF16), expert_w_up.astype(BF16),
        expert_w_down.astype(BF16), tr=tr)
    out = _gather_rows_call(pos, sorted_out, tile=tr)
    return out.reshape(B, S, D)


def kernel(x, mem, mix_norm, w_in, pool_w, pool_scale, q_norm, k_norm, idx_k_norm, w_out, xattn_norm, mem_norm,
           xattn_wq, xattn_wkv, xattn_q_norm, xattn_k_norm, xattn_wo, ffn_norm, router_group_w, router_group_b,
           router_expert_w, router_expert_b, expert_w_gate, expert_w_up, expert_w_down):
    depth = mix_norm.shape[0]
    for l in range(depth):
        x = _layer(
            x, mem, mix_norm[l], w_in[l], pool_w[l], pool_scale[l], q_norm[l], k_norm[l], idx_k_norm[l], w_out[l],
            xattn_norm[l], mem_norm[l], xattn_wq[l], xattn_wkv[l], xattn_q_norm[l], xattn_k_norm[l], xattn_wo[l],
            ffn_norm[l], router_group_w[l], router_group_b[l], router_expert_w[l], router_expert_b[l],
            expert_w_gate[l], expert_w_up[l], expert_w_down[l], tm=512, tq=512, tr=512)
    return x
```

```python
import functools
import math

import jax
import jax.numpy as jnp
from jax import lax
from jax.experimental import pallas as pl
from jax.experimental.pallas import tpu as pltpu

CHUNK = 64
POOL_WINDOWS = (2, 4, 8, 16)
POOL_GROUP = 128
POOL_WIDTH = POOL_GROUP * len(POOL_WINDOWS)
ATTN_HEADS = 8
HEAD_DIM = 64
ATTN_WIDTH = ATTN_HEADS * HEAD_DIM
IDX_HEADS = 8
IDX_DIM = 64
TOPK_MAX = 256
ROPE_THETA = 10000.0
MEM_HEADS = 4
MEM_HEAD_DIM = 128
MEM_WIDTH = MEM_HEADS * MEM_HEAD_DIM
N_GROUPS = 4
EXPERTS_PER_GROUP = 8
N_EXPERTS = N_GROUPS * EXPERTS_PER_GROUP
EPS = 1e-6
OFF_Q = POOL_WIDTH
OFF_K = OFF_Q + ATTN_WIDTH
OFF_V = OFF_K + ATTN_WIDTH
OFF_QI = OFF_V + ATTN_WIDTH
OFF_KI = OFF_QI + IDX_HEADS * IDX_DIM
OFF_WI = OFF_KI + IDX_DIM
IN_COLS = OFF_WI + IDX_HEADS

LANES = 128
SUBLANES = 8
VMEM_LIMIT_BYTES = 56 * 1024 * 1024

HALO = 16
PACKED_SUBLANES = 2 * SUBLANES
INT_MIN = -(2 ** 31)
NEG_BIG = -1e30
Q_SCALE = HEAD_DIM ** -0.5 * math.log2(math.e)

F32 = jnp.float32
BF16 = jnp.bfloat16
I32 = jnp.int32


def _dot(a, b):
    return jnp.dot(a, b, preferred_element_type=F32)


def _dot_nt(a, b):
    return lax.dot_general(a, b, (((1,), (1,)), ((), ())), preferred_element_type=F32)


def _rms_rows(x, g):
    ms = jnp.mean(x * x, axis=-1, keepdims=True)
    return x * lax.rsqrt(ms + EPS) * g


ROW_Q = 0
ROW_K = ROW_Q + ATTN_WIDTH
ROW_V = ROW_K + ATTN_WIDTH
ROW_QI = ROW_V + ATTN_WIDTH
ROW_KI = ROW_QI + IDX_HEADS * IDX_DIM
ROW_WI = ROW_KI + LANES
ROWS_T = ROW_WI + 2 * SUBLANES


def _norm_rope_t(z, gain, cos, sin_signed):
    if gain is not None:
        ms = jnp.mean(z * z, axis=0, keepdims=True)
        z = z * lax.rsqrt(ms + EPS) * gain
    half = z.shape[0] // 2
    swapped = jnp.concatenate([z[half:], z[:half]], axis=0)
    return z * cos + swapped * sin_signed


def _proj_kernel(x_ref, g_ref, wu_ref, wt_ref, poolw_ref, pscale_ref, qg_ref, kg_ref, ig_ref,
                 cos_ref, sin_ref,
                 ypool_ref, qt_ref, k_ref, vt_ref, qit_ref, ki_ref, wit_ref,
                 halo_ref, ext_ref, kt_ref, *, idx_scale):
    tm = x_ref.shape[0]
    h = _rms_rows(x_ref[...], g_ref[...]).astype(BF16)
    cos = cos_ref[...]
    sin = sin_ref[...]

    u = _dot(h, wu_ref[...])
    @pl.when(pl.program_id(1) == 0)
    def _():
        halo_ref[...] = jnp.zeros_like(halo_ref)

    ext_ref[0:HALO, :] = halo_ref[...]
    ext_ref[HALO:HALO + tm, :] = u
    halo_ref[...] = u[tm - HALO:, :]
    pos1 = pl.program_id(1) * tm + lax.broadcasted_iota(I32, (tm, 1), 0) + 1
    for g, w in enumerate(POOL_WINDOWS):
        cols = slice(g * POOL_GROUP, (g + 1) * POOL_GROUP)
        win = u[:, cols]
        for j in range(1, w):
            win = win + ext_ref[HALO - j:HALO - j + tm, cols]
        cnt = jnp.minimum(pos1, w).astype(F32)
        mixed = (win / cnt - u[:, cols]).astype(BF16)
        y = _dot(mixed, poolw_ref[g]) * pscale_ref[:, cols]
        ypool_ref[:, cols] = y.astype(ypool_ref.dtype)

    qg = qg_ref[...]
    kg = kg_ref[...]
    zq = _dot_nt(wt_ref[ROW_Q:ROW_K, :], h)
    for hd in range(ATTN_HEADS):
        rows = slice(hd * HEAD_DIM, (hd + 1) * HEAD_DIM)
        qt_ref[rows, :] = (_norm_rope_t(zq[rows], qg, cos, sin) * Q_SCALE).astype(qt_ref.dtype)
    zk = _dot_nt(wt_ref[ROW_K:ROW_V, :], h)
    for hd in range(ATTN_HEADS):
        rows = slice(hd * HEAD_DIM, (hd + 1) * HEAD_DIM)
        kt_ref[rows, :] = _norm_rope_t(zk[rows], kg, cos, sin)
    k_ref[...] = kt_ref[...].T.astype(k_ref.dtype)
    vt_ref[...] = _dot_nt(wt_ref[ROW_V:ROW_QI, :], h).astype(vt_ref.dtype)
    zqi = _dot_nt(wt_ref[ROW_QI:ROW_KI, :], h)
    for hd in range(IDX_HEADS):
        rows = slice(hd * IDX_DIM, (hd + 1) * IDX_DIM)
        qit_ref[rows, :] = _norm_rope_t(zqi[rows], None, cos, sin).astype(qit_ref.dtype)
    zi = _dot_nt(wt_ref[ROW_KI:ROWS_T, :], h)
    kit = _norm_rope_t(zi[0:IDX_DIM], ig_ref[...], cos, sin)
    kit = jnp.concatenate([kit, jnp.zeros((LANES - IDX_DIM, tm), F32)], axis=0)
    ki_ref[...] = kit.T.astype(ki_ref.dtype)
    wit_ref[...] = zi[LANES:LANES + IDX_HEADS] * idx_scale


def _proj_call(x, g, wu, wt, poolw, pscale, qg, kg, ig, cos_t, sin_t, *, tm):
    B, S, D = x.shape
    nt = S // tm
    kernel = functools.partial(_proj_kernel, idx_scale=(IDX_DIM ** -0.5) * (IDX_HEADS ** -0.5))
    full = lambda shape: pl.BlockSpec(shape, lambda b, i: (0,) * len(shape))
    out_shape = (
        jax.ShapeDtypeStruct((B, S, POOL_WIDTH), BF16),
        jax.ShapeDtypeStruct((B, ATTN_WIDTH, S), BF16),
        jax.ShapeDtypeStruct((B, S, ATTN_WIDTH), BF16),
        jax.ShapeDtypeStruct((B, nt, ATTN_WIDTH, tm), BF16),
        jax.ShapeDtypeStruct((B, IDX_HEADS * IDX_DIM, S), BF16),
        jax.ShapeDtypeStruct((B, S, LANES), BF16),
        jax.ShapeDtypeStruct((B, IDX_HEADS, S), F32),
    )
    return pl.pallas_call(
        kernel,
        grid=(B, nt),
        in_specs=[
            pl.BlockSpec((None, tm, D), lambda b, i: (b, i, 0)),
            full(g.shape), full(wu.shape), full(wt.shape), full(poolw.shape), full(pscale.shape),
            full(qg.shape), full(kg.shape), full(ig.shape),
            pl.BlockSpec((HEAD_DIM, tm), lambda b, i: (0, i)),
            pl.BlockSpec((HEAD_DIM, tm), lambda b, i: (0, i)),
        ],
        out_specs=(
            pl.BlockSpec((None, tm, POOL_WIDTH), lambda b, i: (b, i, 0)),
            pl.BlockSpec((None, ATTN_WIDTH, tm), lambda b, i: (b, 0, i)),
            pl.BlockSpec((None, tm, ATTN_WIDTH), lambda b, i: (b, i, 0)),
            pl.BlockSpec((None, None, ATTN_WIDTH, tm), lambda b, i: (b, i, 0, 0)),
            pl.BlockSpec((None, IDX_HEADS * IDX_DIM, tm), lambda b, i: (b, 0, i)),
            pl.BlockSpec((None, tm, LANES), lambda b, i: (b, i, 0)),
            pl.BlockSpec((None, IDX_HEADS, tm), lambda b, i: (b, 0, i)),
        ),
        out_shape=out_shape,
        scratch_shapes=[
            pltpu.VMEM((HALO, POOL_WIDTH), F32),
            pltpu.VMEM((HALO + tm, POOL_WIDTH), F32),
            pltpu.VMEM((ATTN_WIDTH, tm), F32),
        ],
        compiler_params=pltpu.CompilerParams(
            dimension_semantics=("arbitrary", "arbitrary"), vmem_limit_bytes=VMEM_LIMIT_BYTES),
    )(x, g, wu, wt, poolw, pscale, qg, kg, ig, cos_t, sin_t)


KEY_NEG_INF = -0x7F800000


def _key_to_float(key):
    key = jnp.maximum(key, KEY_NEG_INF)
    bits = jnp.where(key < 0, (key - 1) ^ jnp.int32(0x7FFFFFFF), key)
    return lax.bitcast_convert_type(bits, F32)


def _colsum8(x):
    rows, t = x.shape
    return jnp.sum(x.reshape(rows // SUBLANES, SUBLANES, t), axis=0)


def _dsa_kernel(qit_ref, wit_ref, qt_ref, ki_ref, k_ref, vt_ref, out_ref, sc_ref, coarse_ref, ot_ref, m_ref, l_ref,
                *, topk):
    tq = qit_ref.shape[1]
    nkt_max, tk, _ = sc_ref.shape
    q0 = pl.program_id(1) * tq
    nkt = (q0 + tq + tk - 1) // tk
    qpos = q0 + lax.broadcasted_iota(I32, (1, tq), 1)
    qend = (qpos // CHUNK + 1) * CHUNK
    krow = lax.broadcasted_iota(I32, (tk, 1), 0)
    zeros_half = jnp.zeros((LANES - IDX_DIM, tq), BF16)

    def score_tile(kt, carry):
        ki_t = ki_ref[pl.ds(pl.multiple_of(kt * tk, tk), tk), :]
        acc = jnp.zeros((tk, tq), F32)
        for hd in range(IDX_HEADS):
            qh = jnp.concatenate([qit_ref[hd * IDX_DIM:(hd + 1) * IDX_DIM, :], zeros_half], axis=0)
            rel = jnp.maximum(_dot(ki_t, qh), 0.0)
            acc = acc + rel * wit_ref[hd:hd + 1, :]
        score = jnp.where(kt * tk + krow < qend, acc, -jnp.inf)
        sc_ref[kt] = score
        coarse_ref[kt] = score.astype(BF16)
        return carry

    lax.fori_loop(0, nkt, score_tile, 0)

    def count(pred_fn):
        def body(kt, c8):
            return c8 + _colsum8(jnp.where(pred_fn(sc_ref[kt], kt), 1, 0).astype(I32))
        c8 = lax.fori_loop(0, nkt, body, jnp.zeros((SUBLANES, tq), I32))
        return jnp.sum(c8, axis=0, keepdims=True)

    def count_coarse(cand):
        def body(kt, c16):
            ones = jnp.where(coarse_ref[kt] >= cand, jnp.int16(1), jnp.int16(0))
            for j in range(tk // PACKED_SUBLANES):
                c16 = c16 + ones[j * PACKED_SUBLANES:(j + 1) * PACKED_SUBLANES]
            return c16
        c16 = lax.fori_loop(0, nkt, body, jnp.zeros((PACKED_SUBLANES, tq), jnp.int16))
        return jnp.sum(c16.astype(I32), axis=0, keepdims=True)

    def coarse_step(i, k):
        cand = k + lax.shift_left(jnp.int32(1), 15 - i)
        c = count_coarse(_key_to_float(cand * 65536).astype(BF16))
        return jnp.where(c >= topk, cand, k)

    k_coarse = lax.fori_loop(0, 16, coarse_step, jnp.full((1, tq), -(2 ** 15), I32))

    def fine_step(i, carry):
        t, n = carry
        cand = t + lax.shift_left(jnp.int32(1), 16 - i)
        cand_f = _key_to_float(cand)
        c = count(lambda s, kt: s >= cand_f)
        return jnp.where(c >= topk, cand, t), jnp.where(c >= topk, c, n)

    thr_key, n_sel = lax.fori_loop(0, 17, fine_step, ((k_coarse - 1) * 65536, jnp.zeros((1, tq), I32)))
    thr = _key_to_float(thr_key)
    few = thr == -jnp.inf
    thr_adm = jnp.where(few, jnp.finfo(F32).min, thr)
    excess_ties = jnp.max(jnp.where(few, 0, n_sel)) > topk

    def store_mask(sel_fn):
        def body(kt, carry):
            sc_ref[kt] = jnp.where(sel_fn(sc_ref[kt], kt), 0.0, NEG_BIG).astype(F32)
            return carry
        lax.fori_loop(0, nkt, body, 0)

    @pl.when(jnp.logical_not(excess_ties))
    def _():
        store_mask(lambda s, kt: s >= thr_adm)

    @pl.when(excess_ties)
    def _():
        need = topk - count(lambda s, kt: s > thr)
        nbits = max(1, (nkt_max * tk - 1).bit_length())

        def tie_step(step, cut):
            cand = cut + lax.shift_left(jnp.int32(1), nbits - 1 - step)
            c = count(lambda s, kt: (s == thr) & (kt * tk + krow < cand))
            return jnp.where(c < need, cand, cut)

        cut = lax.fori_loop(0, nbits, tie_step, jnp.zeros((1, tq), I32))
        cut = jnp.where(few, -1, cut)
        store_mask(lambda s, kt: (s > thr_adm) | ((s == thr) & (kt * tk + krow <= cut)))

    zeros_head = jnp.zeros((HEAD_DIM, tq), BF16)
    ones_rows = jnp.ones((PACKED_SUBLANES, tk), BF16)

    def masked_logits(kt, hd):
        qh = qt_ref[hd * HEAD_DIM:(hd + 1) * HEAD_DIM, :]
        qh = jnp.concatenate([qh, zeros_head] if hd % 2 == 0 else [zeros_head, qh], axis=0)
        k2 = k_ref[pl.ds(pl.multiple_of(kt * tk, tk), tk), (hd // 2) * LANES:(hd // 2 + 1) * LANES]
        return _dot(k2, qh) + sc_ref[kt]

    def weighted_values(kt, hd, p):
        pv = _dot(jnp.concatenate([vt_ref[kt, hd * HEAD_DIM:(hd + 1) * HEAD_DIM, :], ones_rows], axis=0), p)
        return pv[:HEAD_DIM, :], pv[HEAD_DIM:HEAD_DIM + 1, :]

    def attn_tile_plain(kt, carry):
        for hd in range(ATTN_HEADS):
            rows = slice(hd * HEAD_DIM, (hd + 1) * HEAD_DIM)
            o, l = weighted_values(kt, hd, jnp.exp2(masked_logits(kt, hd)).astype(BF16))
            l_ref[hd:hd + 1, :] += l
            ot_ref[rows, :] += o
        return carry

    def attn_tile_online(kt, carry):
        for hd in range(ATTN_HEADS):
            rows = slice(hd * HEAD_DIM, (hd + 1) * HEAD_DIM)
            s = masked_logits(kt, hd)
            m_old = m_ref[hd:hd + 1, :]
            m_new = jnp.maximum(m_old, jnp.max(s, axis=0, keepdims=True))
            alpha = jnp.exp2(m_old - m_new)
            m_ref[hd:hd + 1, :] = m_new
            o, l = weighted_values(kt, hd, jnp.exp2(s - m_new).astype(BF16))
            l_ref[hd:hd + 1, :] = alpha * l_ref[hd:hd + 1, :] + l
            ot_ref[rows, :] = alpha * ot_ref[rows, :] + o
        return carry

    l_ref[...] = jnp.zeros(l_ref.shape, F32)
    ot_ref[...] = jnp.zeros(ot_ref.shape, F32)
    lax.fori_loop(0, nkt, attn_tile_plain, 0)
    l_all = l_ref[...]
    unusable = jnp.where((l_all > 0.0) & (l_all < jnp.finfo(F32).max), 0, 1)

    @pl.when(jnp.max(unusable) > 0)
    def _():
        m_ref[...] = jnp.full(m_ref.shape, NEG_BIG, F32)
        l_ref[...] = jnp.zeros(l_ref.shape, F32)
        ot_ref[...] = jnp.zeros(ot_ref.shape, F32)
        lax.fori_loop(0, nkt, attn_tile_online, 0)

    for hd in range(ATTN_HEADS):
        rows = slice(hd * HEAD_DIM, (hd + 1) * HEAD_DIM)
        ot_ref[rows, :] = ot_ref[rows, :] / l_ref[hd:hd + 1, :]
    out_ref[...] = ot_ref[...].T.astype(out_ref.dtype)


def _dsa_call(qit, wit, qt, ki, k, vt, *, tq, topk):
    B, S, _ = k.shape
    _, nkt, _, tk = vt.shape
    kernel = functools.partial(_dsa_kernel, topk=topk)
    return pl.pallas_call(
        kernel,
        grid=(B, S // tq),
        in_specs=[
            pl.BlockSpec((None, IDX_HEADS * IDX_DIM, tq), lambda b, i: (b, 0, i)),
            pl.BlockSpec((None, IDX_HEADS, tq), lambda b, i: (b, 0, i)),
            pl.BlockSpec((None, ATTN_WIDTH, tq), lambda b, i: (b, 0, i)),
            pl.BlockSpec((None, S, LANES), lambda b, i: (b, 0, 0)),
            pl.BlockSpec((None, S, ATTN_WIDTH), lambda b, i: (b, 0, 0)),
            pl.BlockSpec((None, nkt, ATTN_WIDTH, tk), lambda b, i: (b, 0, 0, 0)),
        ],
        out_specs=pl.BlockSpec((None, tq, ATTN_WIDTH), lambda b, i: (b, i, 0)),
        out_shape=jax.ShapeDtypeStruct((B, S, ATTN_WIDTH), BF16),
        scratch_shapes=[
            pltpu.VMEM((nkt, tk, tq), F32),
            pltpu.VMEM((nkt, tk, tq), BF16),
            pltpu.VMEM((ATTN_WIDTH, tq), F32),
            pltpu.VMEM((ATTN_HEADS, tq), F32),
            pltpu.VMEM((ATTN_HEADS, tq), F32),
        ],
        compiler_params=pltpu.CompilerParams(
            dimension_semantics=("arbitrary", "arbitrary"), vmem_limit_bytes=VMEM_LIMIT_BYTES),
    )(qit, wit, qt, ki, k, vt)


def _memkv_kernel(mem_ref, g_ref, wkv_ref, kn_ref, k_ref, v_ref):
    m = _rms_rows(mem_ref[...], g_ref[...]).astype(BF16)
    kv = _dot(m, wkv_ref[...])
    for hd in range(MEM_HEADS):
        cols = slice(hd * MEM_HEAD_DIM, (hd + 1) * MEM_HEAD_DIM)
        k_ref[:, cols] = _rms_rows(kv[:, cols], kn_ref[...]).astype(k_ref.dtype)
    v_ref[...] = kv[:, MEM_WIDTH:].astype(v_ref.dtype)


def _memkv_call(mem, g, wkv, kn):
    B, M, D = mem.shape
    full = lambda shape: pl.BlockSpec(shape, lambda b: (0,) * len(shape))
    return pl.pallas_call(
        _memkv_kernel,
        grid=(B,),
        in_specs=[pl.BlockSpec((None, M, D), lambda b: (b, 0, 0)), full(g.shape), full(wkv.shape), full(kn.shape)],
        out_specs=(pl.BlockSpec((None, M, MEM_WIDTH), lambda b: (b, 0, 0)),
                   pl.BlockSpec((None, M, MEM_WIDTH), lambda b: (b, 0, 0))),
        out_shape=(jax.ShapeDtypeStruct((B, M, MEM_WIDTH), BF16), jax.ShapeDtypeStruct((B, M, MEM_WIDTH), BF16)),
        compiler_params=pltpu.CompilerParams(dimension_semantics=("arbitrary",), vmem_limit_bytes=VMEM_LIMIT_BYTES),
    )(mem, g, wkv, kn)


def _split_bf16(a):
    hi = a.astype(BF16)
    lo = (a - hi.astype(F32)).astype(BF16)
    return hi, lo


def _mid_kernel(x_ref, yp_ref, ya_ref, wo1_ref, wo2_ref, gx_ref, wq_ref, qn_ref, km_ref, vm_ref, wo_ref,
                gf_ref, wr_ref, br_ref, rows_ref, gsel_ref):
    tm, d = x_ref.shape
    x1 = x_ref[...] + _dot(yp_ref[...], wo1_ref[...]) + _dot(ya_ref[...], wo2_ref[...])

    h = _rms_rows(x1, gx_ref[...]).astype(BF16)
    q = _dot(h, wq_ref[...])
    heads = []
    for hd in range(MEM_HEADS):
        cols = slice(hd * MEM_HEAD_DIM, (hd + 1) * MEM_HEAD_DIM)
        qh = (_rms_rows(q[:, cols], qn_ref[...]) * (MEM_HEAD_DIM ** -0.5)).astype(BF16)
        s = _dot_nt(qh, km_ref[:, cols])
        p = jnp.exp(s - jnp.max(s, axis=-1, keepdims=True))
        p = p / jnp.sum(p, axis=-1, keepdims=True)
        heads.append(_dot(p.astype(BF16), vm_ref[:, cols]))
    o = jnp.concatenate(heads, axis=-1).astype(BF16)
    x2 = x1 + _dot(o, wo_ref[...])
    rows_ref[:, :d] = x2

    h_hi, h_lo = _split_bf16(_rms_rows(x2, gf_ref[...]))
    w_hi, w_lo = _split_bf16(wr_ref[...])
    hw = _dot(h_hi, jnp.concatenate([w_hi, w_lo], axis=1))
    logits = hw[:, :LANES] + (hw[:, LANES:] + _dot(h_lo, w_hi)) + br_ref[...]
    lane = lax.broadcasted_iota(I32, (tm, LANES), 1).astype(F32)
    neg_inf = -jnp.inf
    g_logit = jnp.where(lane < N_GROUPS, logits, neg_inf)
    g_max = jnp.max(g_logit, axis=-1, keepdims=True)
    g_sel = jnp.min(jnp.where(g_logit == g_max, lane, LANES), axis=-1, keepdims=True)
    g_w = 1.0 / jnp.sum(jnp.exp(g_logit - g_max), axis=-1, keepdims=True)
    e_lo = N_GROUPS + g_sel * EXPERTS_PER_GROUP
    in_group = (lane >= e_lo) & (lane < e_lo + EXPERTS_PER_GROUP)
    e_logit = jnp.where(in_group, logits, neg_inf)
    v1 = jnp.max(e_logit, axis=-1, keepdims=True)
    i1 = jnp.min(jnp.where(e_logit == v1, lane, LANES), axis=-1, keepdims=True)
    rest = jnp.where(lane == i1, neg_inf, e_logit)
    v2 = jnp.max(rest, axis=-1, keepdims=True)
    i2 = jnp.min(jnp.where(rest == v2, lane, LANES), axis=-1, keepdims=True)
    e2 = jnp.exp(v2 - v1)
    w1 = g_w / (1.0 + e2)
    w2 = g_w * e2 / (1.0 + e2)
    gates = jnp.where(lane == i1 - N_GROUPS, w1, 0.0) + jnp.where(lane == i2 - N_GROUPS, w2, 0.0)
    rows_ref[:, d:] = gates
    gsel_ref[...] = jnp.broadcast_to(g_sel, (tm, LANES)).T[0:1, :]


def _mid_call(x, ypool, yattn, wo1, wo2, gx, wq, qn, kmem, vmem, wo, gf, wr, br, *, tm):
    B, S, D = x.shape
    M = kmem.shape[1]
    nt = S // tm
    full = lambda shape: pl.BlockSpec(shape, lambda b, i: (0,) * len(shape))
    tile = lambda width: pl.BlockSpec((None, tm, width), lambda b, i: (b, i, 0))
    return pl.pallas_call(
        _mid_kernel,
        grid=(B, nt),
        in_specs=[tile(D), tile(POOL_WIDTH), tile(ATTN_WIDTH), full(wo1.shape), full(wo2.shape), full(gx.shape),
                  full(wq.shape), full(qn.shape),
                  pl.BlockSpec((None, M, MEM_WIDTH), lambda b, i: (b, 0, 0)),
                  pl.BlockSpec((None, M, MEM_WIDTH), lambda b, i: (b, 0, 0)),
                  full(wo.shape), full(gf.shape), full(wr.shape), full(br.shape)],
        out_specs=(tile(D + LANES), pl.BlockSpec((None, 1, tm), lambda b, i: (b * nt + i, 0, 0))),
        out_shape=(jax.ShapeDtypeStruct((B, S, D + LANES), F32),
                   jax.ShapeDtypeStruct((B * nt, 1, tm), F32)),
        compiler_params=pltpu.CompilerParams(
            dimension_semantics=("arbitrary", "arbitrary"), vmem_limit_bytes=VMEM_LIMIT_BYTES),
    )(x, ypool, yattn, wo1, wo2, gx, wq, qn, kmem, vmem, wo, gf, wr, br)


META_NUSED = 64
ROW_DMA_UNROLL = 8


def _route_kernel(gsel_ref, pos_ref, meta_ref, *, tr):
    nb, w = gsel_ref.shape
    gsel = gsel_ref[...]
    lane = lax.broadcasted_iota(I32, (1, LANES), 1)
    before = (lax.broadcasted_iota(I32, (w, w), 0) < lax.broadcasted_iota(I32, (w, w), 1)).astype(BF16)
    onehot = [(gsel == g).astype(F32) for g in range(N_GROUPS)]
    cnt = sum(jnp.where(lane == g, jnp.sum(onehot[g], axis=1, keepdims=True), 0.0) for g in range(N_GROUPS))
    run = jnp.zeros((1, LANES), F32)
    carries = []
    for b in range(nb):
        carries.append(run)
        run = run + cnt[b:b + 1]
    carry = jnp.concatenate(carries, axis=0)
    padded = jnp.floor((run + (tr - 1)) * (1.0 / tr)) * tr
    size = [jnp.sum(jnp.where(lane == g, padded, 0.0), axis=1, keepdims=True) for g in range(N_GROUPS)]
    start = [sum(size[:g], jnp.zeros((1, 1), F32)) for g in range(N_GROUPS)]
    total = sum(size, jnp.zeros((1, 1), F32))
    pos = jnp.zeros((nb, w), F32)
    for g in range(N_GROUPS):
        rank = _dot(onehot[g].astype(BF16), before)
        base = jnp.sum(jnp.where(lane == g, carry, 0.0), axis=1, keepdims=True) + start[g]
        pos = pos + onehot[g] * (rank + base)
    pos_ref[...] = pos.astype(I32)
    tile_row = (lane * tr).astype(F32)
    tile_group = jnp.zeros((1, LANES), I32)
    last_group = jnp.zeros((1, 1), I32)
    for g in range(N_GROUPS):
        tile_group = tile_group + jnp.where((tile_row >= start[g]) & (tile_row < start[g] + size[g]), g, 0)
        last_group = jnp.where(size[g] > 0, g, last_group)
    tile_group = jnp.where(tile_row < total, tile_group, last_group)
    meta_ref[...] = jnp.where(lane == META_NUSED, (total * (1.0 / tr)).astype(I32), tile_group)


def _route_call(gsel, *, tr):
    nb, w = gsel.shape
    return pl.pallas_call(
        functools.partial(_route_kernel, tr=tr),
        out_shape=(jax.ShapeDtypeStruct((nb, w), I32), jax.ShapeDtypeStruct((1, LANES), I32)),
        compiler_params=pltpu.CompilerParams(vmem_limit_bytes=VMEM_LIMIT_BYTES),
    )(gsel)


def _scatter_rows_kernel(pos_ref, src_ref, zeros_ref, dst_ref, sem):
    del zeros_ref
    tile = src_ref.shape[0]
    base = pl.program_id(0) * tile

    def body(i, carry):
        for u in range(ROW_DMA_UNROLL):
            r = i * ROW_DMA_UNROLL + u
            pltpu.make_async_copy(
                src_ref.at[pl.ds(r, 1)], dst_ref.at[pl.ds(pos_ref[base + r], 1)], sem).start(priority=u % 2)
        return carry

    lax.fori_loop(0, tile // ROW_DMA_UNROLL, body, 0)
    pltpu.make_async_copy(src_ref, dst_ref.at[pl.ds(0, tile)], sem).wait()


def _scatter_rows_call(pos, src, *, n_out, tile):
    n, width = src.shape
    any_spec = pl.BlockSpec(memory_space=pl.ANY)
    return pl.pallas_call(
        _scatter_rows_kernel,
        grid_spec=pltpu.PrefetchScalarGridSpec(
            num_scalar_prefetch=1, grid=(n // tile,),
            in_specs=[pl.BlockSpec((tile, width), lambda i, pos: (i, 0)), any_spec], out_specs=any_spec,
            scratch_shapes=[pltpu.SemaphoreType.DMA(())]),
        out_shape=jax.ShapeDtypeStruct((n_out, width), src.dtype),
        input_output_aliases={2: 0},
        compiler_params=pltpu.CompilerParams(dimension_semantics=("arbitrary",), has_side_effects=True),
    )(pos, src, jnp.zeros((n_out, width), src.dtype))


def _gather_rows_kernel(pos_ref, src_ref, out_ref, sem):
    tile = out_ref.shape[0]
    base = pl.program_id(0) * tile

    def body(i, carry):
        for u in range(ROW_DMA_UNROLL):
            r = i * ROW_DMA_UNROLL + u
            pltpu.make_async_copy(
                src_ref.at[pl.ds(pos_ref[base + r], 1)], out_ref.at[pl.ds(r, 1)], sem).start(priority=u % 2)
        return carry

    lax.fori_loop(0, tile // ROW_DMA_UNROLL, body, 0)
    pltpu.make_async_copy(src_ref.at[pl.ds(0, tile)], out_ref, sem).wait()


def _gather_rows_call(pos, src, *, tile):
    n = pos.shape[0]
    width = src.shape[1]
    return pl.pallas_call(
        _gather_rows_kernel,
        grid_spec=pltpu.PrefetchScalarGridSpec(
            num_scalar_prefetch=1, grid=(n // tile,),
            in_specs=[pl.BlockSpec(memory_space=pl.ANY)],
            out_specs=pl.BlockSpec((tile, width), lambda i, pos: (i, 0)),
            scratch_shapes=[pltpu.SemaphoreType.DMA(())]),
        out_shape=jax.ShapeDtypeStruct((n, width), src.dtype),
        compiler_params=pltpu.CompilerParams(dimension_semantics=("arbitrary",)),
    )(pos, src)


def _expert_kernel(meta_ref, rows_ref, gf_ref, wg_ref, wu_ref, wd_ref, out_ref):
    j = pl.program_id(0)
    tr, d = out_ref.shape
    n_exp, _, ff = wg_ref.shape

    @pl.when(j < meta_ref[META_NUSED])
    def _():
        x2 = rows_ref[:, :d]
        gates = rows_ref[:, d:]
        h = _rms_rows(x2, gf_ref[...]).astype(BF16)
        lane = lax.broadcasted_iota(I32, (tr, LANES), 1)
        first = meta_ref[j] * n_exp
        hid = []
        for e in range(n_exp):
            a = _dot(h, wg_ref[e])
            b = _dot(h, wu_ref[e])
            gate = jnp.sum(jnp.where(lane == first + e, gates, 0.0), axis=-1, keepdims=True)
            hid.append((a * jax.nn.sigmoid(a) * b * gate).astype(BF16))
        hid = jnp.concatenate(hid, axis=-1)
        out_ref[...] = x2 + _dot(hid, wd_ref[...].reshape(n_exp * ff, d))

    @pl.when(j >= meta_ref[META_NUSED])
    def _():
        out_ref[...] = jnp.zeros_like(out_ref)


def _expert_call(meta, rows, gf, wg, wu, wd, *, tr):
    n_rows, width = rows.shape
    d = width - LANES
    _, n_exp, _, ff = wg.shape
    by_group = lambda shape: pl.BlockSpec((None,) + shape, lambda j, meta: (meta[j], 0, 0, 0))
    return pl.pallas_call(
        _expert_kernel,
        grid_spec=pltpu.PrefetchScalarGridSpec(
            num_scalar_prefetch=1, grid=(n_rows // tr,),
            in_specs=[pl.BlockSpec((tr, width), lambda j, meta: (j, 0)),
                      pl.BlockSpec(gf.shape, lambda j, meta: (0, 0)),
                      by_group((n_exp, d, ff)), by_group((n_exp, d, ff)), by_group((n_exp, ff, d))],
            out_specs=pl.BlockSpec((tr, d), lambda j, meta: (j, 0))),
        out_shape=jax.ShapeDtypeStruct((n_rows, d), F32),
        compiler_params=pltpu.CompilerParams(dimension_semantics=("arbitrary",), vmem_limit_bytes=VMEM_LIMIT_BYTES),
    )(meta, rows, gf, wg, wu, wd)


def _rope_tables_t(seq_len, dim):
    inv = ROPE_THETA ** (-jnp.arange(0, dim, 2, dtype=F32) / dim)
    ang = jnp.arange(seq_len, dtype=F32)[:, None] * inv[None, :]
    ang = jnp.concatenate([ang, ang], axis=-1)
    sign = jnp.concatenate([-jnp.ones((dim // 2,), F32), jnp.ones((dim // 2,), F32)])
    return jnp.cos(ang).T, (jnp.sin(ang) * sign[None, :]).T


def _layer(x, mem, mix_norm, w_in, pool_w, pool_scale, q_norm, k_norm, idx_k_norm, w_out,
           xattn_norm, mem_norm, xattn_wq, xattn_wkv, xattn_q_norm, xattn_k_norm, xattn_wo,
           ffn_norm, router_group_w, router_group_b, router_expert_w, router_expert_b,
           expert_w_gate, expert_w_up, expert_w_down, *, tm, tq, tr):
    B, S, D = x.shape
    topk = min(TOPK_MAX, S // 4)
    row = lambda v: v.reshape(1, -1).astype(F32)
    col = lambda v: jnp.broadcast_to(v.astype(F32)[:, None], (v.shape[0], tm))

    wu = w_in[:, :POOL_WIDTH].astype(BF16)
    pad = lambda n: jnp.zeros((n, D), F32)
    wt = jnp.concatenate([
        w_in[:, OFF_Q:OFF_KI].T, w_in[:, OFF_KI:OFF_WI].T, pad(LANES - IDX_DIM),
        w_in[:, OFF_WI:IN_COLS].T, pad(2 * SUBLANES - IDX_HEADS)], axis=0).astype(BF16)
    cos_t, sin_t = _rope_tables_t(S, HEAD_DIM)

    ypool, qt, k, vt, qit, ki, wit = _proj_call(
        x, row(mix_norm), wu, wt, pool_w.astype(BF16), row(pool_scale), col(q_norm), col(k_norm),
        col(idx_k_norm), cos_t, sin_t, tm=tm)
    yattn = _dsa_call(qit, wit, qt, ki, k, vt, tq=tq, topk=topk)
    kmem, vmem = _memkv_call(mem, row(mem_norm), xattn_wkv.astype(BF16), row(xattn_k_norm))

    n_logits = N_GROUPS + N_EXPERTS
    wr = jnp.concatenate([router_group_w, router_expert_w, jnp.zeros((D, LANES - n_logits), F32)], axis=1)
    br = jnp.concatenate([router_group_b, router_expert_b, jnp.zeros((LANES - n_logits,), F32)]).reshape(1, LANES)
    rows, gsel = _mid_call(
        x, ypool, yattn, w_out[:POOL_WIDTH].astype(BF16), w_out[POOL_WIDTH:].astype(BF16), row(xattn_norm),
        xattn_wq.astype(BF16), row(xattn_q_norm), kmem, vmem, xattn_wo.astype(BF16), row(ffn_norm), wr, br,
        tm=tm)

    n_tok = B * S
    n_sorted = n_tok + N_GROUPS * tr
    pos, meta = _route_call(gsel.reshape(n_tok // tm, tm), tr=tr)
    pos = pos.reshape(n_tok)
    sorted_rows = _scatter_rows_call(pos, rows.reshape(n_tok, D + LANES), n_out=n_sorted, tile=tr)
    sorted_out = _expert_call(
        meta.reshape(LANES), sorted_rows, row(ffn_norm), expert_w_gate.astype(BF16), expert_w_up.astype(BF16),
        expert_w_down.astype(BF16), tr=tr)
    out = _gather_rows_call(pos, sorted_out, tile=tr)
    return out.reshape(B, S, D)


def kernel(x, mem, mix_norm, w_in, pool_w, pool_scale, q_norm, k_norm, idx_k_norm, w_out, xattn_norm, mem_norm,
           xattn_wq, xattn_wkv, xattn_q_norm, xattn_k_norm, xattn_wo, ffn_norm, router_group_w, router_group_b,
           router_expert_w, router_expert_b, expert_w_gate, expert_w_up, expert_w_down):
    depth = mix_norm.shape[0]
    for l in range(depth):
        x = _layer(
            x, mem, mix_norm[l], w_in[l], pool_w[l], pool_scale[l], q_norm[l], k_norm[l], idx_k_norm[l], w_out[l],
            xattn_norm[l], mem_norm[l], xattn_wq[l], xattn_wkv[l], xattn_q_norm[l], xattn_k_norm[l], xattn_wo[l],
            ffn_norm[l], router_group_w[l], router_group_b[l], router_expert_w[l], router_expert_b[l],
            expert_w_gate[l], expert_w_up[l], expert_w_down[l], tm=512, tq=512, tr=512)
    return x
```

```python
import functools
import math

import jax
import jax.numpy as jnp
from jax import lax
from jax.experimental import pallas as pl
from jax.experimental.pallas import tpu as pltpu

CHUNK = 64
POOL_WINDOWS = (2, 4, 8, 16)
POOL_GROUP = 128
POOL_WIDTH = POOL_GROUP * len(POOL_WINDOWS)
ATTN_HEADS = 8
HEAD_DIM = 64
ATTN_WIDTH = ATTN_HEADS * HEAD_DIM
IDX_HEADS = 8
IDX_DIM = 64
TOPK_MAX = 256
ROPE_THETA = 10000.0
MEM_HEADS = 4
MEM_HEAD_DIM = 128
MEM_WIDTH = MEM_HEADS * MEM_HEAD_DIM
N_GROUPS = 4
EXPERTS_PER_GROUP = 8
N_EXPERTS = N_GROUPS * EXPERTS_PER_GROUP
EPS = 1e-6
OFF_Q = POOL_WIDTH
OFF_K = OFF_Q + ATTN_WIDTH
OFF_V = OFF_K + ATTN_WIDTH
OFF_QI = OFF_V + ATTN_WIDTH
OFF_KI = OFF_QI + IDX_HEADS * IDX_DIM
OFF_WI = OFF_KI + IDX_DIM
IN_COLS = OFF_WI + IDX_HEADS

LANES = 128
SUBLANES = 8
VMEM_LIMIT_BYTES = 56 * 1024 * 1024

HALO = 16
PACKED_SUBLANES = 2 * SUBLANES
INT_MIN = -(2 ** 31)
NEG_BIG = -1e30
Q_SCALE = HEAD_DIM ** -0.5 * math.log2(math.e)

F32 = jnp.float32
BF16 = jnp.bfloat16
I32 = jnp.int32


def _dot(a, b):
    return jnp.dot(a, b, preferred_element_type=F32)


def _dot_nt(a, b):
    return lax.dot_general(a, b, (((1,), (1,)), ((), ())), preferred_element_type=F32)


def _rms_rows(x, g):
    ms = jnp.mean(x * x, axis=-1, keepdims=True)
    return x * lax.rsqrt(ms + EPS) * g


ROW_Q = 0
ROW_K = ROW_Q + ATTN_WIDTH
ROW_V = ROW_K + ATTN_WIDTH
ROW_QI = ROW_V + ATTN_WIDTH
ROW_KI = ROW_QI + IDX_HEADS * IDX_DIM
ROW_WI = ROW_KI + LANES
ROWS_T = ROW_WI + 2 * SUBLANES


def _norm_rope_t(z, gain, cos, sin_signed):
    if gain is not None:
        ms = jnp.mean(z * z, axis=0, keepdims=True)
        z = z * lax.rsqrt(ms + EPS) * gain
    half = z.shape[0] // 2
    swapped = jnp.concatenate([z[half:], z[:half]], axis=0)
    return z * cos + swapped * sin_signed


def _proj_kernel(x_ref, g_ref, wu_ref, wt_ref, poolw_ref, pscale_ref, qg_ref, kg_ref, ig_ref,
                 cos_ref, sin_ref,
                 ypool_ref, qt_ref, k_ref, vt_ref, qit_ref, ki_ref, wit_ref,
                 halo_ref, ext_ref, kt_ref, *, idx_scale):
    tm = x_ref.shape[0]
    h = _rms_rows(x_ref[...], g_ref[...]).astype(BF16)
    cos = cos_ref[...]
    sin = sin_ref[...]

    u = _dot(h, wu_ref[...])
    @pl.when(pl.program_id(1) == 0)
    def _():
        halo_ref[...] = jnp.zeros_like(halo_ref)

    ext_ref[0:HALO, :] = halo_ref[...]
    ext_ref[HALO:HALO + tm, :] = u
    halo_ref[...] = u[tm - HALO:, :]
    pos1 = pl.program_id(1) * tm + lax.broadcasted_iota(I32, (tm, 1), 0) + 1
    for g, w in enumerate(POOL_WINDOWS):
        cols = slice(g * POOL_GROUP, (g + 1) * POOL_GROUP)
        win = u[:, cols]
        for j in range(1, w):
            win = win + ext_ref[HALO - j:HALO - j + tm, cols]
        cnt = jnp.minimum(pos1, w).astype(F32)
        mixed = (win / cnt - u[:, cols]).astype(BF16)
        y = _dot(mixed, poolw_ref[g]) * pscale_ref[:, cols]
        ypool_ref[:, cols] = y.astype(ypool_ref.dtype)

    qg = qg_ref[...]
    kg = kg_ref[...]
    zq = _dot_nt(wt_ref[ROW_Q:ROW_K, :], h)
    for hd in range(ATTN_HEADS):
        rows = slice(hd * HEAD_DIM, (hd + 1) * HEAD_DIM)
        qt_ref[rows, :] = (_norm_rope_t(zq[rows], qg, cos, sin) * Q_SCALE).astype(qt_ref.dtype)
    zk = _dot_nt(wt_ref[ROW_K:ROW_V, :], h)
    for hd in range(ATTN_HEADS):
        rows = slice(hd * HEAD_DIM, (hd + 1) * HEAD_DIM)
        kt_ref[rows, :] = _norm_rope_t(zk[rows], kg, cos, sin)
    k_ref[...] = kt_ref[...].T.astype(k_ref.dtype)
    vt = _dot_nt(wt_ref[ROW_V:ROW_QI, :], h).astype(vt_ref.dtype)
    tk = vt_ref.shape[-1]
    for c in range(vt_ref.shape[0]):
        vt_ref[c] = vt[:, c * tk:(c + 1) * tk]
    zqi = _dot_nt(wt_ref[ROW_QI:ROW_KI, :], h)
    for hd in range(IDX_HEADS):
        rows = slice(hd * IDX_DIM, (hd + 1) * IDX_DIM)
        qit_ref[rows, :] = _norm_rope_t(zqi[rows], None, cos, sin).astype(qit_ref.dtype)
    zi = _dot_nt(wt_ref[ROW_KI:ROWS_T, :], h)
    kit = _norm_rope_t(zi[0:IDX_DIM], ig_ref[...], cos, sin)
    kit = jnp.concatenate([kit, jnp.zeros((LANES - IDX_DIM, tm), F32)], axis=0)
    ki_ref[...] = kit.T.astype(ki_ref.dtype)
    wit_ref[...] = zi[LANES:LANES + IDX_HEADS] * idx_scale


def _proj_call(x, g, wu, wt, poolw, pscale, qg, kg, ig, cos_t, sin_t, *, tm, tk):
    B, S, D = x.shape
    nt = S // tm
    kernel = functools.partial(_proj_kernel, idx_scale=(IDX_DIM ** -0.5) * (IDX_HEADS ** -0.5))
    full = lambda shape: pl.BlockSpec(shape, lambda b, i: (0,) * len(shape))
    out_shape = (
        jax.ShapeDtypeStruct((B, S, POOL_WIDTH), BF16),
        jax.ShapeDtypeStruct((B, ATTN_WIDTH, S), BF16),
        jax.ShapeDtypeStruct((B, S, ATTN_WIDTH), BF16),
        jax.ShapeDtypeStruct((B, S // tk, ATTN_WIDTH, tk), BF16),
        jax.ShapeDtypeStruct((B, IDX_HEADS * IDX_DIM, S), BF16),
        jax.ShapeDtypeStruct((B, S, LANES), BF16),
        jax.ShapeDtypeStruct((B, IDX_HEADS, S), F32),
    )
    return pl.pallas_call(
        kernel,
        grid=(B, nt),
        in_specs=[
            pl.BlockSpec((None, tm, D), lambda b, i: (b, i, 0)),
            full(g.shape), full(wu.shape), full(wt.shape), full(poolw.shape), full(pscale.shape),
            full(qg.shape), full(kg.shape), full(ig.shape),
            pl.BlockSpec((HEAD_DIM, tm), lambda b, i: (0, i)),
            pl.BlockSpec((HEAD_DIM, tm), lambda b, i: (0, i)),
        ],
        out_specs=(
            pl.BlockSpec((None, tm, POOL_WIDTH), lambda b, i: (b, i, 0)),
            pl.BlockSpec((None, ATTN_WIDTH, tm), lambda b, i: (b, 0, i)),
            pl.BlockSpec((None, tm, ATTN_WIDTH), lambda b, i: (b, i, 0)),
            pl.BlockSpec((None, tm // tk, ATTN_WIDTH, tk), lambda b, i: (b, i, 0, 0)),
            pl.BlockSpec((None, IDX_HEADS * IDX_DIM, tm), lambda b, i: (b, 0, i)),
            pl.BlockSpec((None, tm, LANES), lambda b, i: (b, i, 0)),
            pl.BlockSpec((None, IDX_HEADS, tm), lambda b, i: (b, 0, i)),
        ),
        out_shape=out_shape,
        scratch_shapes=[
            pltpu.VMEM((HALO, POOL_WIDTH), F32),
            pltpu.VMEM((HALO + tm, POOL_WIDTH), F32),
            pltpu.VMEM((ATTN_WIDTH, tm), F32),
        ],
        compiler_params=pltpu.CompilerParams(
            dimension_semantics=("arbitrary", "arbitrary"), vmem_limit_bytes=VMEM_LIMIT_BYTES),
    )(x, g, wu, wt, poolw, pscale, qg, kg, ig, cos_t, sin_t)


KEY_NEG_INF = -0x7F800000


def _key_to_float(key):
    key = jnp.maximum(key, KEY_NEG_INF)
    bits = jnp.where(key < 0, (key - 1) ^ jnp.int32(0x7FFFFFFF), key)
    return lax.bitcast_convert_type(bits, F32)


def _colsum8(x):
    rows, t = x.shape
    return jnp.sum(x.reshape(rows // SUBLANES, SUBLANES, t), axis=0)


def _dsa_kernel(qit_ref, wit_ref, qt_ref, ki_ref, k_ref, vt_ref, out_ref, sc_ref, coarse_ref, ot_ref, m_ref, l_ref,
                *, topk):
    tq = qit_ref.shape[1]
    nkt_max, tk, _ = sc_ref.shape
    q0 = pl.program_id(1) * tq
    nkt = (q0 + tq + tk - 1) // tk
    qpos = q0 + lax.broadcasted_iota(I32, (1, tq), 1)
    qend = (qpos // CHUNK + 1) * CHUNK
    krow = lax.broadcasted_iota(I32, (tk, 1), 0)
    zeros_half = jnp.zeros((LANES - IDX_DIM, tq), BF16)

    def score_tile(kt, carry):
        ki_t = ki_ref[pl.ds(pl.multiple_of(kt * tk, tk), tk), :]
        acc = jnp.zeros((tk, tq), F32)
        for hd in range(IDX_HEADS):
            qh = jnp.concatenate([qit_ref[hd * IDX_DIM:(hd + 1) * IDX_DIM, :], zeros_half], axis=0)
            rel = jnp.maximum(_dot(ki_t, qh), 0.0)
            acc = acc + rel * wit_ref[hd:hd + 1, :]
        score = jnp.where(kt * tk + krow < qend, acc, -jnp.inf)
        sc_ref[kt] = score
        coarse_ref[kt] = score.astype(BF16)
        return carry

    lax.fori_loop(0, nkt, score_tile, 0)

    def count(pred_fn):
        def body(kt, c8):
            return c8 + _colsum8(jnp.where(pred_fn(sc_ref[kt], kt), 1, 0).astype(I32))
        c8 = lax.fori_loop(0, nkt, body, jnp.zeros((SUBLANES, tq), I32))
        return jnp.sum(c8, axis=0, keepdims=True)

    def count_coarse(cand):
        def body(kt, c16):
            ones = jnp.where(coarse_ref[kt] >= cand, jnp.int16(1), jnp.int16(0))
            for j in range(tk // PACKED_SUBLANES):
                c16 = c16 + ones[j * PACKED_SUBLANES:(j + 1) * PACKED_SUBLANES]
            return c16
        c16 = lax.fori_loop(0, nkt, body, jnp.zeros((PACKED_SUBLANES, tq), jnp.int16))
        return jnp.sum(c16.astype(I32), axis=0, keepdims=True)

    def coarse_step(i, k):
        cand = k + lax.shift_left(jnp.int32(1), 15 - i)
        c = count_coarse(_key_to_float(cand * 65536).astype(BF16))
        return jnp.where(c >= topk, cand, k)

    k_coarse = lax.fori_loop(0, 16, coarse_step, jnp.full((1, tq), -(2 ** 15), I32))

    def fine_step(i, carry):
        t, n = carry
        cand = t + lax.shift_left(jnp.int32(1), 16 - i)
        cand_f = _key_to_float(cand)
        c = count(lambda s, kt: s >= cand_f)
        return jnp.where(c >= topk, cand, t), jnp.where(c >= topk, c, n)

    thr_key, n_sel = lax.fori_loop(0, 17, fine_step, ((k_coarse - 1) * 65536, jnp.zeros((1, tq), I32)))
    thr = _key_to_float(thr_key)
    few = thr == -jnp.inf
    thr_adm = jnp.where(few, jnp.finfo(F32).min, thr)
    excess_ties = jnp.max(jnp.where(few, 0, n_sel)) > topk

    def store_mask(sel_fn):
        def body(kt, carry):
            sc_ref[kt] = jnp.where(sel_fn(sc_ref[kt], kt), 0.0, NEG_BIG).astype(F32)
            return carry
        lax.fori_loop(0, nkt, body, 0)

    @pl.when(jnp.logical_not(excess_ties))
    def _():
        store_mask(lambda s, kt: s >= thr_adm)

    @pl.when(excess_ties)
    def _():
        need = topk - count(lambda s, kt: s > thr)
        nbits = max(1, (nkt_max * tk - 1).bit_length())

        def tie_step(step, cut):
            cand = cut + lax.shift_left(jnp.int32(1), nbits - 1 - step)
            c = count(lambda s, kt: (s == thr) & (kt * tk + krow < cand))
            return jnp.where(c < need, cand, cut)

        cut = lax.fori_loop(0, nbits, tie_step, jnp.zeros((1, tq), I32))
        cut = jnp.where(few, -1, cut)
        store_mask(lambda s, kt: (s > thr_adm) | ((s == thr) & (kt * tk + krow <= cut)))

    zeros_head = jnp.zeros((HEAD_DIM, tq), BF16)
    ones_rows = jnp.ones((PACKED_SUBLANES, tk), BF16)

    def masked_logits(kt, hd):
        qh = qt_ref[hd * HEAD_DIM:(hd + 1) * HEAD_DIM, :]
        qh = jnp.concatenate([qh, zeros_head] if hd % 2 == 0 else [zeros_head, qh], axis=0)
        k2 = k_ref[pl.ds(pl.multiple_of(kt * tk, tk), tk), (hd // 2) * LANES:(hd // 2 + 1) * LANES]
        return _dot(k2, qh) + sc_ref[kt]

    def weighted_values(kt, hd, p):
        pv = _dot(jnp.concatenate([vt_ref[kt, hd * HEAD_DIM:(hd + 1) * HEAD_DIM, :], ones_rows], axis=0), p)
        return pv[:HEAD_DIM, :], pv[HEAD_DIM:HEAD_DIM + 1, :]

    def attn_tile_plain(kt, carry):
        for hd in range(ATTN_HEADS):
            rows = slice(hd * HEAD_DIM, (hd + 1) * HEAD_DIM)
            o, l = weighted_values(kt, hd, jnp.exp2(masked_logits(kt, hd)).astype(BF16))
            l_ref[hd:hd + 1, :] += l
            ot_ref[rows, :] += o
        return carry

    def attn_tile_online(kt, carry):
        for hd in range(ATTN_HEADS):
            rows = slice(hd * HEAD_DIM, (hd + 1) * HEAD_DIM)
            s = masked_logits(kt, hd)
            m_old = m_ref[hd:hd + 1, :]
            m_new = jnp.maximum(m_old, jnp.max(s, axis=0, keepdims=True))
            alpha = jnp.exp2(m_old - m_new)
            m_ref[hd:hd + 1, :] = m_new
            o, l = weighted_values(kt, hd, jnp.exp2(s - m_new).astype(BF16))
            l_ref[hd:hd + 1, :] = alpha * l_ref[hd:hd + 1, :] + l
            ot_ref[rows, :] = alpha * ot_ref[rows, :] + o
        return carry

    l_ref[...] = jnp.zeros(l_ref.shape, F32)
    ot_ref[...] = jnp.zeros(ot_ref.shape, F32)
    lax.fori_loop(0, nkt, attn_tile_plain, 0)
    l_all = l_ref[...]
    unusable = jnp.where((l_all > 0.0) & (l_all < jnp.finfo(F32).max), 0, 1)

    @pl.when(jnp.max(unusable) > 0)
    def _():
        m_ref[...] = jnp.full(m_ref.shape, NEG_BIG, F32)
        l_ref[...] = jnp.zeros(l_ref.shape, F32)
        ot_ref[...] = jnp.zeros(ot_ref.shape, F32)
        lax.fori_loop(0, nkt, attn_tile_online, 0)

    for hd in range(ATTN_HEADS):
        rows = slice(hd * HEAD_DIM, (hd + 1) * HEAD_DIM)
        ot_ref[rows, :] = ot_ref[rows, :] / l_ref[hd:hd + 1, :]
    out_ref[...] = ot_ref[...].T.astype(out_ref.dtype)


def _dsa_call(qit, wit, qt, ki, k, vt, *, tq, topk):
    B, S, _ = k.shape
    _, nkt, _, tk = vt.shape
    kernel = functools.partial(_dsa_kernel, topk=topk)
    return pl.pallas_call(
        kernel,
        grid=(B, S // tq),
        in_specs=[
            pl.BlockSpec((None, IDX_HEADS * IDX_DIM, tq), lambda b, i: (b, 0, i)),
            pl.BlockSpec((None, IDX_HEADS, tq), lambda b, i: (b, 0, i)),
            pl.BlockSpec((None, ATTN_WIDTH, tq), lambda b, i: (b, 0, i)),
            pl.BlockSpec((None, S, LANES), lambda b, i: (b, 0, 0)),
            pl.BlockSpec((None, S, ATTN_WIDTH), lambda b, i: (b, 0, 0)),
            pl.BlockSpec((None, nkt, ATTN_WIDTH, tk), lambda b, i: (b, 0, 0, 0)),
        ],
        out_specs=pl.BlockSpec((None, tq, ATTN_WIDTH), lambda b, i: (b, i, 0)),
        out_shape=jax.ShapeDtypeStruct((B, S, ATTN_WIDTH), BF16),
        scratch_shapes=[
            pltpu.VMEM((nkt, tk, tq), F32),
            pltpu.VMEM((nkt, tk, tq), BF16),
            pltpu.VMEM((ATTN_WIDTH, tq), F32),
            pltpu.VMEM((ATTN_HEADS, tq), F32),
            pltpu.VMEM((ATTN_HEADS, tq), F32),
        ],
        compiler_params=pltpu.CompilerParams(
            dimension_semantics=("arbitrary", "arbitrary"), vmem_limit_bytes=VMEM_LIMIT_BYTES),
    )(qit, wit, qt, ki, k, vt)


def _memkv_kernel(mem_ref, g_ref, wkv_ref, kn_ref, k_ref, v_ref):
    m = _rms_rows(mem_ref[...], g_ref[...]).astype(BF16)
    kv = _dot(m, wkv_ref[...])
    for hd in range(MEM_HEADS):
        cols = slice(hd * MEM_HEAD_DIM, (hd + 1) * MEM_HEAD_DIM)
        k_ref[:, cols] = _rms_rows(kv[:, cols], kn_ref[...]).astype(k_ref.dtype)
    v_ref[...] = kv[:, MEM_WIDTH:].astype(v_ref.dtype)


def _memkv_call(mem, g, wkv, kn):
    B, M, D = mem.shape
    full = lambda shape: pl.BlockSpec(shape, lambda b: (0,) * len(shape))
    return pl.pallas_call(
        _memkv_kernel,
        grid=(B,),
        in_specs=[pl.BlockSpec((None, M, D), lambda b: (b, 0, 0)), full(g.shape), full(wkv.shape), full(kn.shape)],
        out_specs=(pl.BlockSpec((None, M, MEM_WIDTH), lambda b: (b, 0, 0)),
                   pl.BlockSpec((None, M, MEM_WIDTH), lambda b: (b, 0, 0))),
        out_shape=(jax.ShapeDtypeStruct((B, M, MEM_WIDTH), BF16), jax.ShapeDtypeStruct((B, M, MEM_WIDTH), BF16)),
        compiler_params=pltpu.CompilerParams(dimension_semantics=("arbitrary",), vmem_limit_bytes=VMEM_LIMIT_BYTES),
    )(mem, g, wkv, kn)


def _split_bf16(a):
    hi = a.astype(BF16)
    lo = (a - hi.astype(F32)).astype(BF16)
    return hi, lo


def _mid_kernel(x_ref, yp_ref, ya_ref, wo1_ref, wo2_ref, gx_ref, wq_ref, qn_ref, km_ref, vm_ref, wo_ref,
                gf_ref, wr_ref, br_ref, rows_ref, gsel_ref):
    tm, d = x_ref.shape
    x1 = x_ref[...] + _dot(yp_ref[...], wo1_ref[...]) + _dot(ya_ref[...], wo2_ref[...])

    h = _rms_rows(x1, gx_ref[...]).astype(BF16)
    q = _dot(h, wq_ref[...])
    heads = []
    for hd in range(MEM_HEADS):
        cols = slice(hd * MEM_HEAD_DIM, (hd + 1) * MEM_HEAD_DIM)
        qh = (_rms_rows(q[:, cols], qn_ref[...]) * (MEM_HEAD_DIM ** -0.5)).astype(BF16)
        s = _dot_nt(qh, km_ref[:, cols])
        p = jnp.exp(s - jnp.max(s, axis=-1, keepdims=True))
        p = p / jnp.sum(p, axis=-1, keepdims=True)
        heads.append(_dot(p.astype(BF16), vm_ref[:, cols]))
    o = jnp.concatenate(heads, axis=-1).astype(BF16)
    x2 = x1 + _dot(o, wo_ref[...])
    rows_ref[:, :d] = x2

    h_hi, h_lo = _split_bf16(_rms_rows(x2, gf_ref[...]))
    w_hi, w_lo = _split_bf16(wr_ref[...])
    hw = _dot(h_hi, jnp.concatenate([w_hi, w_lo], axis=1))
    logits = hw[:, :LANES] + (hw[:, LANES:] + _dot(h_lo, w_hi)) + br_ref[...]
    lane = lax.broadcasted_iota(I32, (tm, LANES), 1).astype(F32)
    neg_inf = -jnp.inf
    g_logit = jnp.where(lane < N_GROUPS, logits, neg_inf)
    g_max = jnp.max(g_logit, axis=-1, keepdims=True)
    g_sel = jnp.min(jnp.where(g_logit == g_max, lane, LANES), axis=-1, keepdims=True)
    g_w = 1.0 / jnp.sum(jnp.exp(g_logit - g_max), axis=-1, keepdims=True)
    e_lo = N_GROUPS + g_sel * EXPERTS_PER_GROUP
    in_group = (lane >= e_lo) & (lane < e_lo + EXPERTS_PER_GROUP)
    e_logit = jnp.where(in_group, logits, neg_inf)
    v1 = jnp.max(e_logit, axis=-1, keepdims=True)
    i1 = jnp.min(jnp.where(e_logit == v1, lane, LANES), axis=-1, keepdims=True)
    rest = jnp.where(lane == i1, neg_inf, e_logit)
    v2 = jnp.max(rest, axis=-1, keepdims=True)
    i2 = jnp.min(jnp.where(rest == v2, lane, LANES), axis=-1, keepdims=True)
    e2 = jnp.exp(v2 - v1)
    w1 = g_w / (1.0 + e2)
    w2 = g_w * e2 / (1.0 + e2)
    gates = jnp.where(lane == i1 - N_GROUPS, w1, 0.0) + jnp.where(lane == i2 - N_GROUPS, w2, 0.0)
    rows_ref[:, d:] = gates
    gsel_ref[...] = jnp.broadcast_to(g_sel, (tm, LANES)).T[0:1, :]


def _mid_call(x, ypool, yattn, wo1, wo2, gx, wq, qn, kmem, vmem, wo, gf, wr, br, *, tm):
    B, S, D = x.shape
    M = kmem.shape[1]
    nt = S // tm
    full = lambda shape: pl.BlockSpec(shape, lambda b, i: (0,) * len(shape))
    tile = lambda width: pl.BlockSpec((None, tm, width), lambda b, i: (b, i, 0))
    return pl.pallas_call(
        _mid_kernel,
        grid=(B, nt),
        in_specs=[tile(D), tile(POOL_WIDTH), tile(ATTN_WIDTH), full(wo1.shape), full(wo2.shape), full(gx.shape),
                  full(wq.shape), full(qn.shape),
                  pl.BlockSpec((None, M, MEM_WIDTH), lambda b, i: (b, 0, 0)),
                  pl.BlockSpec((None, M, MEM_WIDTH), lambda b, i: (b, 0, 0)),
                  full(wo.shape), full(gf.shape), full(wr.shape), full(br.shape)],
        out_specs=(tile(D + LANES), pl.BlockSpec((None, 1, tm), lambda b, i: (b * nt + i, 0, 0))),
        out_shape=(jax.ShapeDtypeStruct((B, S, D + LANES), F32),
                   jax.ShapeDtypeStruct((B * nt, 1, tm), F32)),
        compiler_params=pltpu.CompilerParams(
            dimension_semantics=("arbitrary", "arbitrary"), vmem_limit_bytes=VMEM_LIMIT_BYTES),
    )(x, ypool, yattn, wo1, wo2, gx, wq, qn, kmem, vmem, wo, gf, wr, br)


META_NUSED = 64
ROW_DMA_UNROLL = 8


def _route_kernel(gsel_ref, pos_ref, meta_ref, *, tr):
    nb, w = gsel_ref.shape
    gsel = gsel_ref[...]
    lane = lax.broadcasted_iota(I32, (1, LANES), 1)
    before = (lax.broadcasted_iota(I32, (w, w), 0) < lax.broadcasted_iota(I32, (w, w), 1)).astype(BF16)
    onehot = [(gsel == g).astype(F32) for g in range(N_GROUPS)]
    cnt = sum(jnp.where(lane == g, jnp.sum(onehot[g], axis=1, keepdims=True), 0.0) for g in range(N_GROUPS))
    run = jnp.zeros((1, LANES), F32)
    carries = []
    for b in range(nb):
        carries.append(run)
        run = run + cnt[b:b + 1]
    carry = jnp.concatenate(carries, axis=0)
    padded = jnp.floor((run + (tr - 1)) * (1.0 / tr)) * tr
    size = [jnp.sum(jnp.where(lane == g, padded, 0.0), axis=1, keepdims=True) for g in range(N_GROUPS)]
    start = [sum(size[:g], jnp.zeros((1, 1), F32)) for g in range(N_GROUPS)]
    total = sum(size, jnp.zeros((1, 1), F32))
    pos = jnp.zeros((nb, w), F32)
    for g in range(N_GROUPS):
        rank = _dot(onehot[g].astype(BF16), before)
        base = jnp.sum(jnp.where(lane == g, carry, 0.0), axis=1, keepdims=True) + start[g]
        pos = pos + onehot[g] * (rank + base)
    pos_ref[...] = pos.astype(I32)
    tile_row = (lane * tr).astype(F32)
    tile_group = jnp.zeros((1, LANES), I32)
    last_group = jnp.zeros((1, 1), I32)
    for g in range(N_GROUPS):
        tile_group = tile_group + jnp.where((tile_row >= start[g]) & (tile_row < start[g] + size[g]), g, 0)
        last_group = jnp.where(size[g] > 0, g, last_group)
    tile_group = jnp.where(tile_row < total, tile_group, last_group)
    meta_ref[...] = jnp.where(lane == META_NUSED, (total * (1.0 / tr)).astype(I32), tile_group)


def _route_call(gsel, *, tr):
    nb, w = gsel.shape
    return pl.pallas_call(
        functools.partial(_route_kernel, tr=tr),
        out_shape=(jax.ShapeDtypeStruct((nb, w), I32), jax.ShapeDtypeStruct((1, LANES), I32)),
        compiler_params=pltpu.CompilerParams(vmem_limit_bytes=VMEM_LIMIT_BYTES),
    )(gsel)


def _scatter_rows_kernel(pos_ref, src_ref, zeros_ref, dst_ref, sem):
    del zeros_ref
    tile = src_ref.shape[0]
    base = pl.program_id(0) * tile

    def body(i, carry):
        for u in range(ROW_DMA_UNROLL):
            r = i * ROW_DMA_UNROLL + u
            pltpu.make_async_copy(
                src_ref.at[pl.ds(r, 1)], dst_ref.at[pl.ds(pos_ref[base + r], 1)], sem).start(priority=u % 2)
        return carry

    lax.fori_loop(0, tile // ROW_DMA_UNROLL, body, 0)
    pltpu.make_async_copy(src_ref, dst_ref.at[pl.ds(0, tile)], sem).wait()


def _scatter_rows_call(pos, src, *, n_out, tile):
    n, width = src.shape
    any_spec = pl.BlockSpec(memory_space=pl.ANY)
    return pl.pallas_call(
        _scatter_rows_kernel,
        grid_spec=pltpu.PrefetchScalarGridSpec(
            num_scalar_prefetch=1, grid=(n // tile,),
            in_specs=[pl.BlockSpec((tile, width), lambda i, pos: (i, 0)), any_spec], out_specs=any_spec,
            scratch_shapes=[pltpu.SemaphoreType.DMA(())]),
        out_shape=jax.ShapeDtypeStruct((n_out, width), src.dtype),
        input_output_aliases={2: 0},
        compiler_params=pltpu.CompilerParams(dimension_semantics=("arbitrary",), has_side_effects=True),
    )(pos, src, jnp.zeros((n_out, width), src.dtype))


def _gather_rows_kernel(pos_ref, src_ref, out_ref, sem):
    tile = out_ref.shape[0]
    base = pl.program_id(0) * tile

    def body(i, carry):
        for u in range(ROW_DMA_UNROLL):
            r = i * ROW_DMA_UNROLL + u
            pltpu.make_async_copy(
                src_ref.at[pl.ds(pos_ref[base + r], 1)], out_ref.at[pl.ds(r, 1)], sem).start(priority=u % 2)
        return carry

    lax.fori_loop(0, tile // ROW_DMA_UNROLL, body, 0)
    pltpu.make_async_copy(src_ref.at[pl.ds(0, tile)], out_ref, sem).wait()


def _gather_rows_call(pos, src, *, tile):
    n = pos.shape[0]
    width = src.shape[1]
    return pl.pallas_call(
        _gather_rows_kernel,
        grid_spec=pltpu.PrefetchScalarGridSpec(
            num_scalar_prefetch=1, grid=(n // tile,),
            in_specs=[pl.BlockSpec(memory_space=pl.ANY)],
            out_specs=pl.BlockSpec((tile, width), lambda i, pos: (i, 0)),
            scratch_shapes=[pltpu.SemaphoreType.DMA(())]),
        out_shape=jax.ShapeDtypeStruct((n, width), src.dtype),
        compiler_params=pltpu.CompilerParams(dimension_semantics=("arbitrary",)),
    )(pos, src)


def _expert_kernel(meta_ref, rows_ref, gf_ref, wg_ref, wu_ref, wd_ref, out_ref):
    j = pl.program_id(0)
    tr, d = out_ref.shape
    n_exp, _, ff = wg_ref.shape

    @pl.when(j < meta_ref[META_NUSED])
    def _():
        x2 = rows_ref[:, :d]
        gates = rows_ref[:, d:]
        h = _rms_rows(x2, gf_ref[...]).astype(BF16)
        lane = lax.broadcasted_iota(I32, (tr, LANES), 1)
        first = meta_ref[j] * n_exp
        hid = []
        for e in range(n_exp):
            a = _dot(h, wg_ref[e])
            b = _dot(h, wu_ref[e])
            gate = jnp.sum(jnp.where(lane == first + e, gates, 0.0), axis=-1, keepdims=True)
            hid.append((a * jax.nn.sigmoid(a) * b * gate).astype(BF16))
        hid = jnp.concatenate(hid, axis=-1)
        out_ref[...] = x2 + _dot(hid, wd_ref[...].reshape(n_exp * ff, d))

    @pl.when(j >= meta_ref[META_NUSED])
    def _():
        out_ref[...] = jnp.zeros_like(out_ref)


def _expert_call(meta, rows, gf, wg, wu, wd, *, tr):
    n_rows, width = rows.shape
    d = width - LANES
    _, n_exp, _, ff = wg.shape
    by_group = lambda shape: pl.BlockSpec((None,) + shape, lambda j, meta: (meta[j], 0, 0, 0))
    return pl.pallas_call(
        _expert_kernel,
        grid_spec=pltpu.PrefetchScalarGridSpec(
            num_scalar_prefetch=1, grid=(n_rows // tr,),
            in_specs=[pl.BlockSpec((tr, width), lambda j, meta: (j, 0)),
                      pl.BlockSpec(gf.shape, lambda j, meta: (0, 0)),
                      by_group((n_exp, d, ff)), by_group((n_exp, d, ff)), by_group((n_exp, ff, d))],
            out_specs=pl.BlockSpec((tr, d), lambda j, meta: (j, 0))),
        out_shape=jax.ShapeDtypeStruct((n_rows, d), F32),
        compiler_params=pltpu.CompilerParams(dimension_semantics=("arbitrary",), vmem_limit_bytes=VMEM_LIMIT_BYTES),
    )(meta, rows, gf, wg, wu, wd)


def _rope_tables_t(seq_len, dim):
    inv = ROPE_THETA ** (-jnp.arange(0, dim, 2, dtype=F32) / dim)
    ang = jnp.arange(seq_len, dtype=F32)[:, None] * inv[None, :]
    ang = jnp.concatenate([ang, ang], axis=-1)
    sign = jnp.concatenate([-jnp.ones((dim // 2,), F32), jnp.ones((dim // 2,), F32)])
    return jnp.cos(ang).T, (jnp.sin(ang) * sign[None, :]).T


def _layer(x, mem, mix_norm, w_in, pool_w, pool_scale, q_norm, k_norm, idx_k_norm, w_out,
           xattn_norm, mem_norm, xattn_wq, xattn_wkv, xattn_q_norm, xattn_k_norm, xattn_wo,
           ffn_norm, router_group_w, router_group_b, router_expert_w, router_expert_b,
           expert_w_gate, expert_w_up, expert_w_down, *, tm, tm_mid, tq, tk, tr):
    B, S, D = x.shape
    topk = min(TOPK_MAX, S // 4)
    row = lambda v: v.reshape(1, -1).astype(F32)
    col = lambda v: jnp.broadcast_to(v.astype(F32)[:, None], (v.shape[0], tm))

    wu = w_in[:, :POOL_WIDTH].astype(BF16)
    pad = lambda n: jnp.zeros((n, D), F32)
    wt = jnp.concatenate([
        w_in[:, OFF_Q:OFF_KI].T, w_in[:, OFF_KI:OFF_WI].T, pad(LANES - IDX_DIM),
        w_in[:, OFF_WI:IN_COLS].T, pad(2 * SUBLANES - IDX_HEADS)], axis=0).astype(BF16)
    cos_t, sin_t = _rope_tables_t(S, HEAD_DIM)

    ypool, qt, k, vt, qit, ki, wit = _proj_call(
        x, row(mix_norm), wu, wt, pool_w.astype(BF16), row(pool_scale), col(q_norm), col(k_norm),
        col(idx_k_norm), cos_t, sin_t, tm=tm, tk=tk)
    yattn = _dsa_call(qit, wit, qt, ki, k, vt, tq=tq, topk=topk)
    kmem, vmem = _memkv_call(mem, row(mem_norm), xattn_wkv.astype(BF16), row(xattn_k_norm))

    n_logits = N_GROUPS + N_EXPERTS
    wr = jnp.concatenate([router_group_w, router_expert_w, jnp.zeros((D, LANES - n_logits), F32)], axis=1)
    br = jnp.concatenate([router_group_b, router_expert_b, jnp.zeros((LANES - n_logits,), F32)]).reshape(1, LANES)
    rows, gsel = _mid_call(
        x, ypool, yattn, w_out[:POOL_WIDTH].astype(BF16), w_out[POOL_WIDTH:].astype(BF16), row(xattn_norm),
        xattn_wq.astype(BF16), row(xattn_q_norm), kmem, vmem, xattn_wo.astype(BF16), row(ffn_norm), wr, br,
        tm=tm_mid)

    n_tok = B * S
    n_sorted = n_tok + N_GROUPS * tr
    pos, meta = _route_call(gsel.reshape(n_tok // tm_mid, tm_mid), tr=tr)
    pos = pos.reshape(n_tok)
    sorted_rows = _scatter_rows_call(pos, rows.reshape(n_tok, D + LANES), n_out=n_sorted, tile=tm_mid)
    sorted_out = _expert_call(
        meta.reshape(LANES), sorted_rows, row(ffn_norm), expert_w_gate.astype(BF16), expert_w_up.astype(BF16),
        expert_w_down.astype(BF16), tr=tr)
    out = _gather_rows_call(pos, sorted_out, tile=tm_mid)
    return out.reshape(B, S, D)


def kernel(x, mem, mix_norm, w_in, pool_w, pool_scale, q_norm, k_norm, idx_k_norm, w_out, xattn_norm, mem_norm,
           xattn_wq, xattn_wkv, xattn_q_norm, xattn_k_norm, xattn_wo, ffn_norm, router_group_w, router_group_b,
           router_expert_w, router_expert_b, expert_w_gate, expert_w_up, expert_w_down):
    depth = mix_norm.shape[0]
    for l in range(depth):
        x = _layer(
            x, mem, mix_norm[l], w_in[l], pool_w[l], pool_scale[l], q_norm[l], k_norm[l], idx_k_norm[l], w_out[l],
            xattn_norm[l], mem_norm[l], xattn_wq[l], xattn_wkv[l], xattn_q_norm[l], xattn_k_norm[l], xattn_wo[l],
            ffn_norm[l], router_group_w[l], router_group_b[l], router_expert_w[l], router_expert_b[l],
            expert_w_gate[l], expert_w_up[l], expert_w_down[l], tm=1024, tm_mid=1024, tq=512, tk=512, tr=512)
    return x
```

```python
import functools
import math

import jax
import jax.numpy as jnp
from jax import lax
from jax.experimental import pallas as pl
from jax.experimental.pallas import tpu as pltpu

CHUNK = 64
POOL_WINDOWS = (2, 4, 8, 16)
POOL_GROUP = 128
POOL_WIDTH = POOL_GROUP * len(POOL_WINDOWS)
ATTN_HEADS = 8
HEAD_DIM = 64
ATTN_WIDTH = ATTN_HEADS * HEAD_DIM
IDX_HEADS = 8
IDX_DIM = 64
TOPK_MAX = 256
ROPE_THETA = 10000.0
MEM_HEADS = 4
MEM_HEAD_DIM = 128
MEM_WIDTH = MEM_HEADS * MEM_HEAD_DIM
N_GROUPS = 4
EXPERTS_PER_GROUP = 8
N_EXPERTS = N_GROUPS * EXPERTS_PER_GROUP
EPS = 1e-6
OFF_Q = POOL_WIDTH
OFF_K = OFF_Q + ATTN_WIDTH
OFF_V = OFF_K + ATTN_WIDTH
OFF_QI = OFF_V + ATTN_WIDTH
OFF_KI = OFF_QI + IDX_HEADS * IDX_DIM
OFF_WI = OFF_KI + IDX_DIM
IN_COLS = OFF_WI + IDX_HEADS

LANES = 128
SUBLANES = 8
VMEM_LIMIT_BYTES = 56 * 1024 * 1024

HALO = 16
PACKED_SUBLANES = 2 * SUBLANES
INT_MIN = -(2 ** 31)
NEG_BIG = -1e30
Q_SCALE = HEAD_DIM ** -0.5 * math.log2(math.e)

F32 = jnp.float32
BF16 = jnp.bfloat16
I32 = jnp.int32


def _dot(a, b):
    return jnp.dot(a, b, preferred_element_type=F32)


def _dot_nt(a, b):
    return lax.dot_general(a, b, (((1,), (1,)), ((), ())), preferred_element_type=F32)


def _rms_rows(x, g):
    ms = jnp.mean(x * x, axis=-1, keepdims=True)
    return x * lax.rsqrt(ms + EPS) * g


ROW_Q = 0
ROW_K = ROW_Q + ATTN_WIDTH
ROW_V = ROW_K + ATTN_WIDTH
ROW_QI = ROW_V + ATTN_WIDTH
ROW_KI = ROW_QI + IDX_HEADS * IDX_DIM
ROW_WI = ROW_KI + LANES
ROWS_T = ROW_WI + 2 * SUBLANES


def _norm_rope_t(z, gain, cos, sin_signed):
    if gain is not None:
        ms = jnp.mean(z * z, axis=0, keepdims=True)
        z = z * lax.rsqrt(ms + EPS) * gain
    half = z.shape[0] // 2
    swapped = jnp.concatenate([z[half:], z[:half]], axis=0)
    return z * cos + swapped * sin_signed


def _proj_kernel(x_ref, g_ref, wu_ref, wt_ref, poolw_ref, pscale_ref, qg_ref, kg_ref, ig_ref,
                 cos_ref, sin_ref,
                 ypool_ref, qt_ref, k_ref, vt_ref, qit_ref, ki_ref, wit_ref,
                 halo_ref, ext_ref, kt_ref, *, idx_scale):
    tm = x_ref.shape[0]
    h = _rms_rows(x_ref[...], g_ref[...]).astype(BF16)
    cos = cos_ref[...]
    sin = sin_ref[...]

    u = _dot(h, wu_ref[...])
    @pl.when(pl.program_id(1) == 0)
    def _():
        halo_ref[...] = jnp.zeros_like(halo_ref)

    ext_ref[0:HALO, :] = halo_ref[...]
    ext_ref[HALO:HALO + tm, :] = u
    halo_ref[...] = u[tm - HALO:, :]
    pos1 = pl.program_id(1) * tm + lax.broadcasted_iota(I32, (tm, 1), 0) + 1
    for g, w in enumerate(POOL_WINDOWS):
        cols = slice(g * POOL_GROUP, (g + 1) * POOL_GROUP)
        win = u[:, cols]
        for j in range(1, w):
            win = win + ext_ref[HALO - j:HALO - j + tm, cols]
        cnt = jnp.minimum(pos1, w).astype(F32)
        mixed = (win / cnt - u[:, cols]).astype(BF16)
        y = _dot(mixed, poolw_ref[g]) * pscale_ref[:, cols]
        ypool_ref[:, cols] = y.astype(ypool_ref.dtype)

    qg = qg_ref[...]
    kg = kg_ref[...]
    zq = _dot_nt(wt_ref[ROW_Q:ROW_K, :], h)
    for hd in range(ATTN_HEADS):
        rows = slice(hd * HEAD_DIM, (hd + 1) * HEAD_DIM)
        qt_ref[rows, :] = (_norm_rope_t(zq[rows], qg, cos, sin) * Q_SCALE).astype(qt_ref.dtype)
    zk = _dot_nt(wt_ref[ROW_K:ROW_V, :], h)
    for hd in range(ATTN_HEADS):
        rows = slice(hd * HEAD_DIM, (hd + 1) * HEAD_DIM)
        kt_ref[rows, :] = _norm_rope_t(zk[rows], kg, cos, sin)
    k_ref[...] = kt_ref[...].T.astype(k_ref.dtype)
    vt = _dot_nt(wt_ref[ROW_V:ROW_QI, :], h).astype(vt_ref.dtype)
    tk = vt_ref.shape[-1]
    for c in range(vt_ref.shape[0]):
        vt_ref[c] = vt[:, c * tk:(c + 1) * tk]
    zqi = _dot_nt(wt_ref[ROW_QI:ROW_KI, :], h)
    for hd in range(IDX_HEADS):
        rows = slice(hd * IDX_DIM, (hd + 1) * IDX_DIM)
        qit_ref[rows, :] = _norm_rope_t(zqi[rows], None, cos, sin).astype(qit_ref.dtype)
    zi = _dot_nt(wt_ref[ROW_KI:ROWS_T, :], h)
    kit = _norm_rope_t(zi[0:IDX_DIM], ig_ref[...], cos, sin)
    kit = jnp.concatenate([kit, jnp.zeros((LANES - IDX_DIM, tm), F32)], axis=0)
    ki_ref[...] = kit.T.astype(ki_ref.dtype)
    wit_ref[...] = zi[LANES:LANES + IDX_HEADS] * idx_scale


def _proj_call(x, g, wu, wt, poolw, pscale, qg, kg, ig, cos_t, sin_t, *, tm, tk):
    B, S, D = x.shape
    nt = S // tm
    kernel = functools.partial(_proj_kernel, idx_scale=(IDX_DIM ** -0.5) * (IDX_HEADS ** -0.5))
    full = lambda shape: pl.BlockSpec(shape, lambda b, i: (0,) * len(shape))
    out_shape = (
        jax.ShapeDtypeStruct((B, S, POOL_WIDTH), BF16),
        jax.ShapeDtypeStruct((B, ATTN_WIDTH, S), BF16),
        jax.ShapeDtypeStruct((B, S, ATTN_WIDTH), BF16),
        jax.ShapeDtypeStruct((B, S // tk, ATTN_WIDTH, tk), BF16),
        jax.ShapeDtypeStruct((B, IDX_HEADS * IDX_DIM, S), BF16),
        jax.ShapeDtypeStruct((B, S, LANES), BF16),
        jax.ShapeDtypeStruct((B, IDX_HEADS, S), F32),
    )
    return pl.pallas_call(
        kernel,
        grid=(B, nt),
        in_specs=[
            pl.BlockSpec((None, tm, D), lambda b, i: (b, i, 0)),
            full(g.shape), full(wu.shape), full(wt.shape), full(poolw.shape), full(pscale.shape),
            full(qg.shape), full(kg.shape), full(ig.shape),
            pl.BlockSpec((HEAD_DIM, tm), lambda b, i: (0, i)),
            pl.BlockSpec((HEAD_DIM, tm), lambda b, i: (0, i)),
        ],
        out_specs=(
            pl.BlockSpec((None, tm, POOL_WIDTH), lambda b, i: (b, i, 0)),
            pl.BlockSpec((None, ATTN_WIDTH, tm), lambda b, i: (b, 0, i)),
            pl.BlockSpec((None, tm, ATTN_WIDTH), lambda b, i: (b, i, 0)),
            pl.BlockSpec((None, tm // tk, ATTN_WIDTH, tk), lambda b, i: (b, i, 0, 0)),
            pl.BlockSpec((None, IDX_HEADS * IDX_DIM, tm), lambda b, i: (b, 0, i)),
            pl.BlockSpec((None, tm, LANES), lambda b, i: (b, i, 0)),
            pl.BlockSpec((None, IDX_HEADS, tm), lambda b, i: (b, 0, i)),
        ),
        out_shape=out_shape,
        scratch_shapes=[
            pltpu.VMEM((HALO, POOL_WIDTH), F32),
            pltpu.VMEM((HALO + tm, POOL_WIDTH), F32),
            pltpu.VMEM((ATTN_WIDTH, tm), F32),
        ],
        compiler_params=pltpu.CompilerParams(
            dimension_semantics=("arbitrary", "arbitrary"), vmem_limit_bytes=VMEM_LIMIT_BYTES),
    )(x, g, wu, wt, poolw, pscale, qg, kg, ig, cos_t, sin_t)


KEY_NEG_INF = -0x7F800000


def _key_to_float(key):
    key = jnp.maximum(key, KEY_NEG_INF)
    bits = jnp.where(key < 0, (key - 1) ^ jnp.int32(0x7FFFFFFF), key)
    return lax.bitcast_convert_type(bits, F32)


def _colsum8(x):
    rows, t = x.shape
    return jnp.sum(x.reshape(rows // SUBLANES, SUBLANES, t), axis=0)


def _dsa_kernel(qit_ref, wit_ref, qt_ref, ki_ref, k_ref, vt_ref, out_ref, sc_ref, coarse_ref, ot_ref, m_ref, l_ref,
                *, topk):
    tq = qit_ref.shape[1]
    nkt_max, tk, _ = sc_ref.shape
    q0 = pl.program_id(1) * tq
    nkt = (q0 + tq + tk - 1) // tk
    qpos = q0 + lax.broadcasted_iota(I32, (1, tq), 1)
    qend = (qpos // CHUNK + 1) * CHUNK
    krow = lax.broadcasted_iota(I32, (tk, 1), 0)
    zeros_half = jnp.zeros((LANES - IDX_DIM, tq), BF16)

    def score_tile(kt, carry):
        ki_t = ki_ref[pl.ds(pl.multiple_of(kt * tk, tk), tk), :]
        acc = jnp.zeros((tk, tq), F32)
        for hd in range(IDX_HEADS):
            qh = jnp.concatenate([qit_ref[hd * IDX_DIM:(hd + 1) * IDX_DIM, :], zeros_half], axis=0)
            rel = jnp.maximum(_dot(ki_t, qh), 0.0)
            acc = acc + rel * wit_ref[hd:hd + 1, :]
        score = jnp.where(kt * tk + krow < qend, acc, -jnp.inf)
        sc_ref[kt] = score
        coarse_ref[kt] = score.astype(BF16)
        return carry

    lax.fori_loop(0, nkt, score_tile, 0)

    def count(pred_fn):
        def body(kt, c8):
            return c8 + _colsum8(jnp.where(pred_fn(sc_ref[kt], kt), 1, 0).astype(I32))
        c8 = lax.fori_loop(0, nkt, body, jnp.zeros((SUBLANES, tq), I32))
        return jnp.sum(c8, axis=0, keepdims=True)

    def count_coarse(cand):
        def body(kt, c16):
            ones = jnp.where(coarse_ref[kt] >= cand, jnp.int16(1), jnp.int16(0))
            for j in range(tk // PACKED_SUBLANES):
                c16 = c16 + ones[j * PACKED_SUBLANES:(j + 1) * PACKED_SUBLANES]
            return c16
        c16 = lax.fori_loop(0, nkt, body, jnp.zeros((PACKED_SUBLANES, tq), jnp.int16))
        return jnp.sum(c16.astype(I32), axis=0, keepdims=True)

    def coarse_step(i, k):
        cand = k + lax.shift_left(jnp.int32(1), 15 - i)
        c = count_coarse(_key_to_float(cand * 65536).astype(BF16))
        return jnp.where(c >= topk, cand, k)

    k_coarse = lax.fori_loop(0, 16, coarse_step, jnp.full((1, tq), -(2 ** 15), I32))

    def fine_step(i, carry):
        t, n = carry
        cand = t + lax.shift_left(jnp.int32(1), 16 - i)
        cand_f = _key_to_float(cand)
        c = count(lambda s, kt: s >= cand_f)
        return jnp.where(c >= topk, cand, t), jnp.where(c >= topk, c, n)

    thr_key, n_sel = lax.fori_loop(0, 17, fine_step, ((k_coarse - 1) * 65536, jnp.zeros((1, tq), I32)))
    thr = _key_to_float(thr_key)
    few = thr == -jnp.inf
    thr_adm = jnp.where(few, jnp.finfo(F32).min, thr)
    excess_ties = jnp.max(jnp.where(few, 0, n_sel)) > topk

    def store_mask(sel_fn):
        def body(kt, carry):
            sc_ref[kt] = jnp.where(sel_fn(sc_ref[kt], kt), 0.0, NEG_BIG).astype(F32)
            return carry
        lax.fori_loop(0, nkt, body, 0)

    @pl.when(jnp.logical_not(excess_ties))
    def _():
        store_mask(lambda s, kt: s >= thr_adm)

    @pl.when(excess_ties)
    def _():
        need = topk - count(lambda s, kt: s > thr)
        nbits = max(1, (nkt_max * tk - 1).bit_length())

        def tie_step(step, cut):
            cand = cut + lax.shift_left(jnp.int32(1), nbits - 1 - step)
            c = count(lambda s, kt: (s == thr) & (kt * tk + krow < cand))
            return jnp.where(c < need, cand, cut)

        cut = lax.fori_loop(0, nbits, tie_step, jnp.zeros((1, tq), I32))
        cut = jnp.where(few, -1, cut)
        store_mask(lambda s, kt: (s > thr_adm) | ((s == thr) & (kt * tk + krow <= cut)))

    zeros_head = jnp.zeros((HEAD_DIM, tq), BF16)
    ones_rows = jnp.ones((PACKED_SUBLANES, tk), BF16)

    def masked_logits(kt, hd):
        qh = qt_ref[hd * HEAD_DIM:(hd + 1) * HEAD_DIM, :]
        qh = jnp.concatenate([qh, zeros_head] if hd % 2 == 0 else [zeros_head, qh], axis=0)
        k2 = k_ref[pl.ds(pl.multiple_of(kt * tk, tk), tk), (hd // 2) * LANES:(hd // 2 + 1) * LANES]
        return _dot(k2, qh) + sc_ref[kt]

    def weighted_values(kt, hd, p):
        pv = _dot(jnp.concatenate([vt_ref[kt, hd * HEAD_DIM:(hd + 1) * HEAD_DIM, :], ones_rows], axis=0), p)
        return pv[:HEAD_DIM, :], pv[HEAD_DIM:HEAD_DIM + 1, :]

    def attn_tile_plain(kt, carry):
        for hd in range(ATTN_HEADS):
            rows = slice(hd * HEAD_DIM, (hd + 1) * HEAD_DIM)
            o, l = weighted_values(kt, hd, jnp.exp2(masked_logits(kt, hd)).astype(BF16))
            l_ref[hd:hd + 1, :] += l
            ot_ref[rows, :] += o
        return carry

    def attn_tile_online(kt, carry):
        for hd in range(ATTN_HEADS):
            rows = slice(hd * HEAD_DIM, (hd + 1) * HEAD_DIM)
            s = masked_logits(kt, hd)
            m_old = m_ref[hd:hd + 1, :]
            m_new = jnp.maximum(m_old, jnp.max(s, axis=0, keepdims=True))
            alpha = jnp.exp2(m_old - m_new)
            m_ref[hd:hd + 1, :] = m_new
            o, l = weighted_values(kt, hd, jnp.exp2(s - m_new).astype(BF16))
            l_ref[hd:hd + 1, :] = alpha * l_ref[hd:hd + 1, :] + l
            ot_ref[rows, :] = alpha * ot_ref[rows, :] + o
        return carry

    l_ref[...] = jnp.zeros(l_ref.shape, F32)
    ot_ref[...] = jnp.zeros(ot_ref.shape, F32)
    lax.fori_loop(0, nkt, attn_tile_plain, 0)
    l_all = l_ref[...]
    unusable = jnp.where((l_all > 0.0) & (l_all < jnp.finfo(F32).max), 0, 1)

    @pl.when(jnp.max(unusable) > 0)
    def _():
        m_ref[...] = jnp.full(m_ref.shape, NEG_BIG, F32)
        l_ref[...] = jnp.zeros(l_ref.shape, F32)
        ot_ref[...] = jnp.zeros(ot_ref.shape, F32)
        lax.fori_loop(0, nkt, attn_tile_online, 0)

    for hd in range(ATTN_HEADS):
        rows = slice(hd * HEAD_DIM, (hd + 1) * HEAD_DIM)
        ot_ref[rows, :] = ot_ref[rows, :] / l_ref[hd:hd + 1, :]
    out_ref[...] = ot_ref[...].T.astype(out_ref.dtype)


def _dsa_call(qit, wit, qt, ki, k, vt, *, tq, topk):
    B, S, _ = k.shape
    _, nkt, _, tk = vt.shape
    kernel = functools.partial(_dsa_kernel, topk=topk)
    return pl.pallas_call(
        kernel,
        grid=(B, S // tq),
        in_specs=[
            pl.BlockSpec((None, IDX_HEADS * IDX_DIM, tq), lambda b, i: (b, 0, i)),
            pl.BlockSpec((None, IDX_HEADS, tq), lambda b, i: (b, 0, i)),
            pl.BlockSpec((None, ATTN_WIDTH, tq), lambda b, i: (b, 0, i)),
            pl.BlockSpec((None, S, LANES), lambda b, i: (b, 0, 0)),
            pl.BlockSpec((None, S, ATTN_WIDTH), lambda b, i: (b, 0, 0)),
            pl.BlockSpec((None, nkt, ATTN_WIDTH, tk), lambda b, i: (b, 0, 0, 0)),
        ],
        out_specs=pl.BlockSpec((None, tq, ATTN_WIDTH), lambda b, i: (b, i, 0)),
        out_shape=jax.ShapeDtypeStruct((B, S, ATTN_WIDTH), BF16),
        scratch_shapes=[
            pltpu.VMEM((nkt, tk, tq), F32),
            pltpu.VMEM((nkt, tk, tq), BF16),
            pltpu.VMEM((ATTN_WIDTH, tq), F32),
            pltpu.VMEM((ATTN_HEADS, tq), F32),
            pltpu.VMEM((ATTN_HEADS, tq), F32),
        ],
        compiler_params=pltpu.CompilerParams(
            dimension_semantics=("arbitrary", "arbitrary"), vmem_limit_bytes=VMEM_LIMIT_BYTES),
    )(qit, wit, qt, ki, k, vt)


def _memkv_kernel(mem_ref, g_ref, wkv_ref, kn_ref, k_ref, v_ref):
    m = _rms_rows(mem_ref[...], g_ref[...]).astype(BF16)
    kv = _dot(m, wkv_ref[...])
    for hd in range(MEM_HEADS):
        cols = slice(hd * MEM_HEAD_DIM, (hd + 1) * MEM_HEAD_DIM)
        k_ref[:, cols] = _rms_rows(kv[:, cols], kn_ref[...]).astype(k_ref.dtype)
    v_ref[...] = kv[:, MEM_WIDTH:].astype(v_ref.dtype)


def _memkv_call(mem, g, wkv, kn):
    B, M, D = mem.shape
    full = lambda shape: pl.BlockSpec(shape, lambda b: (0,) * len(shape))
    return pl.pallas_call(
        _memkv_kernel,
        grid=(B,),
        in_specs=[pl.BlockSpec((None, M, D), lambda b: (b, 0, 0)), full(g.shape), full(wkv.shape), full(kn.shape)],
        out_specs=(pl.BlockSpec((None, M, MEM_WIDTH), lambda b: (b, 0, 0)),
                   pl.BlockSpec((None, M, MEM_WIDTH), lambda b: (b, 0, 0))),
        out_shape=(jax.ShapeDtypeStruct((B, M, MEM_WIDTH), BF16), jax.ShapeDtypeStruct((B, M, MEM_WIDTH), BF16)),
        compiler_params=pltpu.CompilerParams(dimension_semantics=("arbitrary",), vmem_limit_bytes=VMEM_LIMIT_BYTES),
    )(mem, g, wkv, kn)


def _split_bf16(a):
    hi = a.astype(BF16)
    lo = (a - hi.astype(F32)).astype(BF16)
    return hi, lo


def _mid_kernel(x_ref, yp_ref, ya_ref, wo1_ref, wo2_ref, gx_ref, wq_ref, qn_ref, km_ref, vm_ref, wo_ref,
                gf_ref, wr_ref, br_ref, rows_ref, gsel_ref):
    tm, d = x_ref.shape
    x1 = x_ref[...] + _dot(yp_ref[...], wo1_ref[...]) + _dot(ya_ref[...], wo2_ref[...])

    h = _rms_rows(x1, gx_ref[...]).astype(BF16)
    q = _dot(h, wq_ref[...])
    heads = []
    for hd in range(MEM_HEADS):
        cols = slice(hd * MEM_HEAD_DIM, (hd + 1) * MEM_HEAD_DIM)
        qh = (_rms_rows(q[:, cols], qn_ref[...]) * (MEM_HEAD_DIM ** -0.5)).astype(BF16)
        s = _dot_nt(qh, km_ref[:, cols])
        p = jnp.exp(s - jnp.max(s, axis=-1, keepdims=True))
        p = p / jnp.sum(p, axis=-1, keepdims=True)
        heads.append(_dot(p.astype(BF16), vm_ref[:, cols]))
    o = jnp.concatenate(heads, axis=-1).astype(BF16)
    x2 = x1 + _dot(o, wo_ref[...])
    rows_ref[:, :d] = x2

    h_hi, h_lo = _split_bf16(_rms_rows(x2, gf_ref[...]))
    w_hi, w_lo = _split_bf16(wr_ref[...])
    hw = _dot(h_hi, jnp.concatenate([w_hi, w_lo], axis=1))
    logits = hw[:, :LANES] + (hw[:, LANES:] + _dot(h_lo, w_hi)) + br_ref[...]
    lane = lax.broadcasted_iota(I32, (tm, LANES), 1).astype(F32)
    neg_inf = -jnp.inf
    g_logit = jnp.where(lane < N_GROUPS, logits, neg_inf)
    g_max = jnp.max(g_logit, axis=-1, keepdims=True)
    g_sel = jnp.min(jnp.where(g_logit == g_max, lane, LANES), axis=-1, keepdims=True)
    g_w = 1.0 / jnp.sum(jnp.exp(g_logit - g_max), axis=-1, keepdims=True)
    e_lo = N_GROUPS + g_sel * EXPERTS_PER_GROUP
    in_group = (lane >= e_lo) & (lane < e_lo + EXPERTS_PER_GROUP)
    e_logit = jnp.where(in_group, logits, neg_inf)
    v1 = jnp.max(e_logit, axis=-1, keepdims=True)
    i1 = jnp.min(jnp.where(e_logit == v1, lane, LANES), axis=-1, keepdims=True)
    rest = jnp.where(lane == i1, neg_inf, e_logit)
    v2 = jnp.max(rest, axis=-1, keepdims=True)
    i2 = jnp.min(jnp.where(rest == v2, lane, LANES), axis=-1, keepdims=True)
    e2 = jnp.exp(v2 - v1)
    w1 = g_w / (1.0 + e2)
    w2 = g_w * e2 / (1.0 + e2)
    gates = jnp.where(lane == i1 - N_GROUPS, w1, 0.0) + jnp.where(lane == i2 - N_GROUPS, w2, 0.0)
    rows_ref[:, d:] = gates
    gsel_ref[...] = jnp.broadcast_to(g_sel, (tm, LANES)).T[0:1, :]


def _mid_call(x, ypool, yattn, wo1, wo2, gx, wq, qn, kmem, vmem, wo, gf, wr, br, *, tm):
    B, S, D = x.shape
    M = kmem.shape[1]
    nt = S // tm
    full = lambda shape: pl.BlockSpec(shape, lambda b, i: (0,) * len(shape))
    tile = lambda width: pl.BlockSpec((None, tm, width), lambda b, i: (b, i, 0))
    return pl.pallas_call(
        _mid_kernel,
        grid=(B, nt),
        in_specs=[tile(D), tile(POOL_WIDTH), tile(ATTN_WIDTH), full(wo1.shape), full(wo2.shape), full(gx.shape),
                  full(wq.shape), full(qn.shape),
                  pl.BlockSpec((None, M, MEM_WIDTH), lambda b, i: (b, 0, 0)),
                  pl.BlockSpec((None, M, MEM_WIDTH), lambda b, i: (b, 0, 0)),
                  full(wo.shape), full(gf.shape), full(wr.shape), full(br.shape)],
        out_specs=(tile(D + LANES), pl.BlockSpec((None, 1, tm), lambda b, i: (b * nt + i, 0, 0))),
        out_shape=(jax.ShapeDtypeStruct((B, S, D + LANES), F32),
                   jax.ShapeDtypeStruct((B * nt, 1, tm), F32)),
        compiler_params=pltpu.CompilerParams(
            dimension_semantics=("arbitrary", "arbitrary"), vmem_limit_bytes=VMEM_LIMIT_BYTES),
    )(x, ypool, yattn, wo1, wo2, gx, wq, qn, kmem, vmem, wo, gf, wr, br)


META_NUSED = 64
ROW_DMA_UNROLL = 8


def _route_kernel(gsel_ref, pos_ref, meta_ref, *, tr):
    nb, w = gsel_ref.shape
    gsel = gsel_ref[...]
    lane = lax.broadcasted_iota(I32, (1, LANES), 1)
    before = (lax.broadcasted_iota(I32, (w, w), 0) < lax.broadcasted_iota(I32, (w, w), 1)).astype(BF16)
    onehot = [(gsel == g).astype(F32) for g in range(N_GROUPS)]
    cnt = sum(jnp.where(lane == g, jnp.sum(onehot[g], axis=1, keepdims=True), 0.0) for g in range(N_GROUPS))
    run = jnp.zeros((1, LANES), F32)
    carries = []
    for b in range(nb):
        carries.append(run)
        run = run + cnt[b:b + 1]
    carry = jnp.concatenate(carries, axis=0)
    padded = jnp.floor((run + (tr - 1)) * (1.0 / tr)) * tr
    size = [jnp.sum(jnp.where(lane == g, padded, 0.0), axis=1, keepdims=True) for g in range(N_GROUPS)]
    start = [sum(size[:g], jnp.zeros((1, 1), F32)) for g in range(N_GROUPS)]
    total = sum(size, jnp.zeros((1, 1), F32))
    pos = jnp.zeros((nb, w), F32)
    for g in range(N_GROUPS):
        rank = _dot(onehot[g].astype(BF16), before)
        base = jnp.sum(jnp.where(lane == g, carry, 0.0), axis=1, keepdims=True) + start[g]
        pos = pos + onehot[g] * (rank + base)
    pos_ref[...] = pos.astype(I32)
    tile_row = (lane * tr).astype(F32)
    tile_group = jnp.zeros((1, LANES), I32)
    last_group = jnp.zeros((1, 1), I32)
    for g in range(N_GROUPS):
        tile_group = tile_group + jnp.where((tile_row >= start[g]) & (tile_row < start[g] + size[g]), g, 0)
        last_group = jnp.where(size[g] > 0, g, last_group)
    tile_group = jnp.where(tile_row < total, tile_group, last_group)
    meta_ref[...] = jnp.where(lane == META_NUSED, (total * (1.0 / tr)).astype(I32), tile_group)


def _route_call(gsel, *, tr):
    nb, w = gsel.shape
    return pl.pallas_call(
        functools.partial(_route_kernel, tr=tr),
        out_shape=(jax.ShapeDtypeStruct((nb, w), I32), jax.ShapeDtypeStruct((1, LANES), I32)),
        compiler_params=pltpu.CompilerParams(vmem_limit_bytes=VMEM_LIMIT_BYTES),
    )(gsel)


def _scatter_rows_kernel(pos_ref, src_ref, zeros_ref, dst_ref, sem):
    del zeros_ref
    tile = src_ref.shape[0]
    base = pl.program_id(0) * tile

    def body(i, carry):
        for u in range(ROW_DMA_UNROLL):
            r = i * ROW_DMA_UNROLL + u
            pltpu.make_async_copy(
                src_ref.at[pl.ds(r, 1)], dst_ref.at[pl.ds(pos_ref[base + r], 1)], sem).start(priority=u % 2)
        return carry

    lax.fori_loop(0, tile // ROW_DMA_UNROLL, body, 0)
    pltpu.make_async_copy(src_ref, dst_ref.at[pl.ds(0, tile)], sem).wait()


def _scatter_rows_call(pos, src, *, n_out, tile):
    n, width = src.shape
    any_spec = pl.BlockSpec(memory_space=pl.ANY)
    return pl.pallas_call(
        _scatter_rows_kernel,
        grid_spec=pltpu.PrefetchScalarGridSpec(
            num_scalar_prefetch=1, grid=(n // tile,),
            in_specs=[pl.BlockSpec((tile, width), lambda i, pos: (i, 0)), any_spec], out_specs=any_spec,
            scratch_shapes=[pltpu.SemaphoreType.DMA(())]),
        out_shape=jax.ShapeDtypeStruct((n_out, width), src.dtype),
        input_output_aliases={2: 0},
        compiler_params=pltpu.CompilerParams(dimension_semantics=("arbitrary",), has_side_effects=True),
    )(pos, src, jnp.zeros((n_out, width), src.dtype))


def _gather_rows_kernel(pos_ref, src_ref, out_ref, sem):
    tile = out_ref.shape[0]
    base = pl.program_id(0) * tile

    def body(i, carry):
        for u in range(ROW_DMA_UNROLL):
            r = i * ROW_DMA_UNROLL + u
            pltpu.make_async_copy(
                src_ref.at[pl.ds(pos_ref[base + r], 1)], out_ref.at[pl.ds(r, 1)], sem).start(priority=u % 2)
        return carry

    lax.fori_loop(0, tile // ROW_DMA_UNROLL, body, 0)
    pltpu.make_async_copy(src_ref.at[pl.ds(0, tile)], out_ref, sem).wait()


def _gather_rows_call(pos, src, *, tile):
    n = pos.shape[0]
    width = src.shape[1]
    return pl.pallas_call(
        _gather_rows_kernel,
        grid_spec=pltpu.PrefetchScalarGridSpec(
            num_scalar_prefetch=1, grid=(n // tile,),
            in_specs=[pl.BlockSpec(memory_space=pl.ANY)],
            out_specs=pl.BlockSpec((tile, width), lambda i, pos: (i, 0)),
            scratch_shapes=[pltpu.SemaphoreType.DMA(())]),
        out_shape=jax.ShapeDtypeStruct((n, width), src.dtype),
        compiler_params=pltpu.CompilerParams(dimension_semantics=("arbitrary",)),
    )(pos, src)


def _expert_kernel(meta_ref, rows_ref, gf_ref, wg_ref, wu_ref, wd_ref, out_ref):
    j = pl.program_id(0)
    tr, d = out_ref.shape
    n_exp, _, ff = wg_ref.shape

    @pl.when(j < meta_ref[META_NUSED])
    def _():
        x2 = rows_ref[:, :d]
        gates = rows_ref[:, d:]
        h = _rms_rows(x2, gf_ref[...]).astype(BF16)
        lane = lax.broadcasted_iota(I32, (tr, LANES), 1)
        first = meta_ref[j] * n_exp
        hid = []
        for e in range(n_exp):
            a = _dot(h, wg_ref[e].astype(BF16))
            b = _dot(h, wu_ref[e].astype(BF16))
            gate = jnp.sum(jnp.where(lane == first + e, gates, 0.0), axis=-1, keepdims=True)
            hid.append((a * jax.nn.sigmoid(a) * b * gate).astype(BF16))
        hid = jnp.concatenate(hid, axis=-1)
        out_ref[...] = x2 + _dot(hid, wd_ref[...].reshape(n_exp * ff, d).astype(BF16))

    @pl.when(j >= meta_ref[META_NUSED])
    def _():
        out_ref[...] = jnp.zeros_like(out_ref)


def _expert_call(meta, rows, gf, wg, wu, wd, *, tr):
    n_rows, width = rows.shape
    d = width - LANES
    _, n_exp, _, ff = wg.shape
    by_group = lambda shape: pl.BlockSpec(
        (None,) + shape, lambda j, meta: (meta[j], 0, 0, 0), pipeline_mode=pl.Buffered(1))
    return pl.pallas_call(
        _expert_kernel,
        grid_spec=pltpu.PrefetchScalarGridSpec(
            num_scalar_prefetch=1, grid=(n_rows // tr,),
            in_specs=[pl.BlockSpec((tr, width), lambda j, meta: (j, 0)),
                      pl.BlockSpec(gf.shape, lambda j, meta: (0, 0)),
                      by_group((n_exp, d, ff)), by_group((n_exp, d, ff)), by_group((n_exp, ff, d))],
            out_specs=pl.BlockSpec((tr, d), lambda j, meta: (j, 0))),
        out_shape=jax.ShapeDtypeStruct((n_rows, d), F32),
        compiler_params=pltpu.CompilerParams(dimension_semantics=("arbitrary",), vmem_limit_bytes=VMEM_LIMIT_BYTES),
    )(meta, rows, gf, wg, wu, wd)


def _rope_tables_t(seq_len, dim):
    inv = ROPE_THETA ** (-jnp.arange(0, dim, 2, dtype=F32) / dim)
    ang = jnp.arange(seq_len, dtype=F32)[:, None] * inv[None, :]
    ang = jnp.concatenate([ang, ang], axis=-1)
    sign = jnp.concatenate([-jnp.ones((dim // 2,), F32), jnp.ones((dim // 2,), F32)])
    return jnp.cos(ang).T, (jnp.sin(ang) * sign[None, :]).T


def _layer(x, mem, mix_norm, w_in, pool_w, pool_scale, q_norm, k_norm, idx_k_norm, w_out,
           xattn_norm, mem_norm, xattn_wq, xattn_wkv, xattn_q_norm, xattn_k_norm, xattn_wo,
           ffn_norm, router_group_w, router_group_b, router_expert_w, router_expert_b,
           expert_w_gate, expert_w_up, expert_w_down, *, tm, tm_mid, tq, tk, tr):
    B, S, D = x.shape
    topk = min(TOPK_MAX, S // 4)
    row = lambda v: v.reshape(1, -1).astype(F32)
    col = lambda v: jnp.broadcast_to(v.astype(F32)[:, None], (v.shape[0], tm))

    wu = w_in[:, :POOL_WIDTH].astype(BF16)
    pad = lambda n: jnp.zeros((n, D), F32)
    wt = jnp.concatenate([
        w_in[:, OFF_Q:OFF_KI].T, w_in[:, OFF_KI:OFF_WI].T, pad(LANES - IDX_DIM),
        w_in[:, OFF_WI:IN_COLS].T, pad(2 * SUBLANES - IDX_HEADS)], axis=0).astype(BF16)
    cos_t, sin_t = _rope_tables_t(S, HEAD_DIM)

    ypool, qt, k, vt, qit, ki, wit = _proj_call(
        x, row(mix_norm), wu, wt, pool_w.astype(BF16), row(pool_scale), col(q_norm), col(k_norm),
        col(idx_k_norm), cos_t, sin_t, tm=tm, tk=tk)
    yattn = _dsa_call(qit, wit, qt, ki, k, vt, tq=tq, topk=topk)
    kmem, vmem = _memkv_call(mem, row(mem_norm), xattn_wkv.astype(BF16), row(xattn_k_norm))

    n_logits = N_GROUPS + N_EXPERTS
    wr = jnp.concatenate([router_group_w, router_expert_w, jnp.zeros((D, LANES - n_logits), F32)], axis=1)
    br = jnp.concatenate([router_group_b, router_expert_b, jnp.zeros((LANES - n_logits,), F32)]).reshape(1, LANES)
    rows, gsel = _mid_call(
        x, ypool, yattn, w_out[:POOL_WIDTH].astype(BF16), w_out[POOL_WIDTH:].astype(BF16), row(xattn_norm),
        xattn_wq.astype(BF16), row(xattn_q_norm), kmem, vmem, xattn_wo.astype(BF16), row(ffn_norm), wr, br,
        tm=tm_mid)

    n_tok = B * S
    n_sorted = n_tok + N_GROUPS * tr
    pos, meta = _route_call(gsel.reshape(n_tok // tm_mid, tm_mid), tr=tr)
    pos = pos.reshape(n_tok)
    sorted_rows = _scatter_rows_call(pos, rows.reshape(n_tok, D + LANES), n_out=n_sorted, tile=tm_mid)
    sorted_out = _expert_call(
        meta.reshape(LANES), sorted_rows, row(ffn_norm), expert_w_gate, expert_w_up, expert_w_down, tr=tr)
    out = _gather_rows_call(pos, sorted_out, tile=tm_mid)
    return out.reshape(B, S, D)


def kernel(x, mem, mix_norm, w_in, pool_w, pool_scale, q_norm, k_norm, idx_k_norm, w_out, xattn_norm, mem_norm,
           xattn_wq, xattn_wkv, xattn_q_norm, xattn_k_norm, xattn_wo, ffn_norm, router_group_w, router_group_b,
           router_expert_w, router_expert_b, expert_w_gate, expert_w_up, expert_w_down):
    depth = mix_norm.shape[0]
    for l in range(depth):
        x = _layer(
            x, mem, mix_norm[l], w_in[l], pool_w[l], pool_scale[l], q_norm[l], k_norm[l], idx_k_norm[l], w_out[l],
            xattn_norm[l], mem_norm[l], xattn_wq[l], xattn_wkv[l], xattn_q_norm[l], xattn_k_norm[l], xattn_wo[l],
            ffn_norm[l], router_group_w[l], router_group_b[l], router_expert_w[l], router_expert_b[l],
            expert_w_gate[l], expert_w_up[l], expert_w_down[l], tm=1024, tm_mid=1024, tq=512, tk=512, tr=512)
    return x
```

```python
import functools
import math

import jax
import jax.numpy as jnp
from jax import lax
from jax.experimental import pallas as pl
from jax.experimental.pallas import tpu as pltpu

CHUNK = 64
POOL_WINDOWS = (2, 4, 8, 16)
POOL_GROUP = 128
POOL_WIDTH = POOL_GROUP * len(POOL_WINDOWS)
ATTN_HEADS = 8
HEAD_DIM = 64
ATTN_WIDTH = ATTN_HEADS * HEAD_DIM
IDX_HEADS = 8
IDX_DIM = 64
TOPK_MAX = 256
ROPE_THETA = 10000.0
MEM_HEADS = 4
MEM_HEAD_DIM = 128
MEM_WIDTH = MEM_HEADS * MEM_HEAD_DIM
N_GROUPS = 4
EXPERTS_PER_GROUP = 8
N_EXPERTS = N_GROUPS * EXPERTS_PER_GROUP
EPS = 1e-6
OFF_Q = POOL_WIDTH
OFF_K = OFF_Q + ATTN_WIDTH
OFF_V = OFF_K + ATTN_WIDTH
OFF_QI = OFF_V + ATTN_WIDTH
OFF_KI = OFF_QI + IDX_HEADS * IDX_DIM
OFF_WI = OFF_KI + IDX_DIM
IN_COLS = OFF_WI + IDX_HEADS

LANES = 128
SUBLANES = 8
VMEM_LIMIT_BYTES = 56 * 1024 * 1024

HALO = 16
PACKED_SUBLANES = 2 * SUBLANES
NEG_BIG = -1e30
Q_SCALE = HEAD_DIM ** -0.5 * math.log2(math.e)

F32 = jnp.float32
BF16 = jnp.bfloat16
I32 = jnp.int32


def _dot(a, b):
    return jnp.dot(a, b, preferred_element_type=F32)


def _dot_nt(a, b):
    return lax.dot_general(a, b, (((1,), (1,)), ((), ())), preferred_element_type=F32)


def _rms_rows(x, g):
    ms = jnp.mean(x * x, axis=-1, keepdims=True)
    return x * lax.rsqrt(ms + EPS) * g


ROW_Q = 0
ROW_K = ROW_Q + ATTN_WIDTH
ROW_V = ROW_K + ATTN_WIDTH
ROW_QI = ROW_V + ATTN_WIDTH
ROW_KI = ROW_QI + IDX_HEADS * IDX_DIM
ROW_WI = ROW_KI + LANES
ROWS_T = ROW_WI + 2 * SUBLANES


def _norm_rope_t(z, gain, cos, sin_signed):
    if gain is not None:
        ms = jnp.mean(z * z, axis=0, keepdims=True)
        z = z * lax.rsqrt(ms + EPS) * gain
    half = z.shape[0] // 2
    swapped = jnp.concatenate([z[half:], z[:half]], axis=0)
    return z * cos + swapped * sin_signed


def _proj_kernel(x_ref, g_ref, wu_ref, wt_ref, poolw_ref, pscale_ref, qg_ref, kg_ref, ig_ref,
                 cos_ref, sin_ref,
                 ypool_ref, qt_ref, k_ref, vt_ref, qit_ref, ki_ref, wit_ref,
                 halo_ref, ext_ref, kt_ref, *, idx_scale):
    tm = x_ref.shape[0]
    h = _rms_rows(x_ref[...], g_ref[...]).astype(BF16)
    cos = cos_ref[...]
    sin = sin_ref[...]

    u = _dot(h, wu_ref[...])
    @pl.when(pl.program_id(1) == 0)
    def _():
        halo_ref[...] = jnp.zeros_like(halo_ref)

    ext_ref[0:HALO, :] = halo_ref[...]
    ext_ref[HALO:HALO + tm, :] = u
    halo_ref[...] = u[tm - HALO:, :]
    pos1 = pl.program_id(1) * tm + lax.broadcasted_iota(I32, (tm, 1), 0) + 1
    for g, w in enumerate(POOL_WINDOWS):
        cols = slice(g * POOL_GROUP, (g + 1) * POOL_GROUP)
        win = u[:, cols]
        for j in range(1, w):
            win = win + ext_ref[HALO - j:HALO - j + tm, cols]
        cnt = jnp.minimum(pos1, w).astype(F32)
        mixed = (win / cnt - u[:, cols]).astype(BF16)
        y = _dot(mixed, poolw_ref[g]) * pscale_ref[:, cols]
        ypool_ref[:, cols] = y.astype(ypool_ref.dtype)

    qg = qg_ref[...]
    kg = kg_ref[...]
    zq = _dot_nt(wt_ref[ROW_Q:ROW_K, :], h)
    for hd in range(ATTN_HEADS):
        rows = slice(hd * HEAD_DIM, (hd + 1) * HEAD_DIM)
        qt_ref[rows, :] = (_norm_rope_t(zq[rows], qg, cos, sin) * Q_SCALE).astype(qt_ref.dtype)
    zk = _dot_nt(wt_ref[ROW_K:ROW_V, :], h)
    for hd in range(ATTN_HEADS):
        rows = slice(hd * HEAD_DIM, (hd + 1) * HEAD_DIM)
        kt_ref[rows, :] = _norm_rope_t(zk[rows], kg, cos, sin)
    k_ref[...] = kt_ref[...].T.astype(k_ref.dtype)
    vt = _dot_nt(wt_ref[ROW_V:ROW_QI, :], h).astype(vt_ref.dtype)
    tk = vt_ref.shape[-1]
    for c in range(vt_ref.shape[0]):
        vt_ref[c] = vt[:, c * tk:(c + 1) * tk]
    zqi = _dot_nt(wt_ref[ROW_QI:ROW_KI, :], h)
    for hd in range(IDX_HEADS):
        rows = slice(hd * IDX_DIM, (hd + 1) * IDX_DIM)
        qit_ref[rows, :] = _norm_rope_t(zqi[rows], None, cos, sin).astype(qit_ref.dtype)
    zi = _dot_nt(wt_ref[ROW_KI:ROWS_T, :], h)
    kit = _norm_rope_t(zi[0:IDX_DIM], ig_ref[...], cos, sin)
    kit = jnp.concatenate([kit, jnp.zeros((LANES - IDX_DIM, tm), F32)], axis=0)
    ki_ref[...] = kit.T.astype(ki_ref.dtype)
    wit_ref[...] = zi[LANES:LANES + IDX_HEADS] * idx_scale


def _proj_call(x, g, wu, wt, poolw, pscale, qg, kg, ig, cos_t, sin_t, *, tm, tk):
    B, S, D = x.shape
    nt = S // tm
    kernel = functools.partial(_proj_kernel, idx_scale=(IDX_DIM ** -0.5) * (IDX_HEADS ** -0.5))
    full = lambda shape: pl.BlockSpec(shape, lambda b, i: (0,) * len(shape))
    out_shape = (
        jax.ShapeDtypeStruct((B, S, POOL_WIDTH), BF16),
        jax.ShapeDtypeStruct((B, ATTN_WIDTH, S), BF16),
        jax.ShapeDtypeStruct((B, S, ATTN_WIDTH), BF16),
        jax.ShapeDtypeStruct((B, S // tk, ATTN_WIDTH, tk), BF16),
        jax.ShapeDtypeStruct((B, IDX_HEADS * IDX_DIM, S), BF16),
        jax.ShapeDtypeStruct((B, S, LANES), BF16),
        jax.ShapeDtypeStruct((B, IDX_HEADS, S), F32),
    )
    return pl.pallas_call(
        kernel,
        grid=(B, nt),
        in_specs=[
            pl.BlockSpec((None, tm, D), lambda b, i: (b, i, 0)),
            full(g.shape), full(wu.shape), full(wt.shape), full(poolw.shape), full(pscale.shape),
            full(qg.shape), full(kg.shape), full(ig.shape),
            pl.BlockSpec((HEAD_DIM, tm), lambda b, i: (0, i)),
            pl.BlockSpec((HEAD_DIM, tm), lambda b, i: (0, i)),
        ],
        out_specs=(
            pl.BlockSpec((None, tm, POOL_WIDTH), lambda b, i: (b, i, 0)),
            pl.BlockSpec((None, ATTN_WIDTH, tm), lambda b, i: (b, 0, i)),
            pl.BlockSpec((None, tm, ATTN_WIDTH), lambda b, i: (b, i, 0)),
            pl.BlockSpec((None, tm // tk, ATTN_WIDTH, tk), lambda b, i: (b, i, 0, 0)),
            pl.BlockSpec((None, IDX_HEADS * IDX_DIM, tm), lambda b, i: (b, 0, i)),
            pl.BlockSpec((None, tm, LANES), lambda b, i: (b, i, 0)),
            pl.BlockSpec((None, IDX_HEADS, tm), lambda b, i: (b, 0, i)),
        ),
        out_shape=out_shape,
        scratch_shapes=[
            pltpu.VMEM((HALO, POOL_WIDTH), F32),
            pltpu.VMEM((HALO + tm, POOL_WIDTH), F32),
            pltpu.VMEM((ATTN_WIDTH, tm), F32),
        ],
        compiler_params=pltpu.CompilerParams(
            dimension_semantics=("arbitrary", "arbitrary"), vmem_limit_bytes=VMEM_LIMIT_BYTES),
    )(x, g, wu, wt, poolw, pscale, qg, kg, ig, cos_t, sin_t)


KEY_NEG_INF = -0x7F800000


def _key_to_float(key):
    key = jnp.maximum(key, KEY_NEG_INF)
    bits = jnp.where(key < 0, (key - 1) ^ jnp.int32(0x7FFFFFFF), key)
    return lax.bitcast_convert_type(bits, F32)


def _colsum8(x):
    rows, t = x.shape
    return jnp.sum(x.reshape(rows // SUBLANES, SUBLANES, t), axis=0)


def _dsa_kernel(qit_ref, wit_ref, qt_ref, ki_ref, k_ref, vt_ref, out_ref, sc_ref, coarse_ref, ot_ref, m_ref, l_ref,
                *, topk):
    tq = qit_ref.shape[1]
    nkt_max, tk, _ = sc_ref.shape
    q0 = pl.program_id(1) * tq
    nkt = (q0 + tq + tk - 1) // tk
    qpos = q0 + lax.broadcasted_iota(I32, (1, tq), 1)
    qend = (qpos // CHUNK + 1) * CHUNK
    krow = lax.broadcasted_iota(I32, (tk, 1), 0)
    zeros_half = jnp.zeros((LANES - IDX_DIM, tq), BF16)

    def score_tile(kt, carry):
        ki_t = ki_ref[pl.ds(pl.multiple_of(kt * tk, tk), tk), :]
        acc = jnp.zeros((tk, tq), F32)
        for hd in range(IDX_HEADS):
            qh = jnp.concatenate([qit_ref[hd * IDX_DIM:(hd + 1) * IDX_DIM, :], zeros_half], axis=0)
            rel = jnp.maximum(_dot(ki_t, qh), 0.0)
            acc = acc + rel * wit_ref[hd:hd + 1, :]
        score = jnp.where(kt * tk + krow < qend, acc, -jnp.inf)
        sc_ref[kt] = score
        coarse_ref[kt] = score.astype(BF16)
        return carry

    lax.fori_loop(0, nkt, score_tile, 0)

    def count(pred_fn):
        def body(kt, c8):
            return c8 + _colsum8(jnp.where(pred_fn(sc_ref[kt], kt), 1, 0).astype(I32))
        c8 = lax.fori_loop(0, nkt, body, jnp.zeros((SUBLANES, tq), I32))
        return jnp.sum(c8, axis=0, keepdims=True)

    def count_coarse(cand):
        def body(kt, c16):
            ones = jnp.where(coarse_ref[kt] >= cand, jnp.int16(1), jnp.int16(0))
            for j in range(tk // PACKED_SUBLANES):
                c16 = c16 + ones[j * PACKED_SUBLANES:(j + 1) * PACKED_SUBLANES]
            return c16
        c16 = lax.fori_loop(0, nkt, body, jnp.zeros((PACKED_SUBLANES, tq), jnp.int16))
        return jnp.sum(c16.astype(I32), axis=0, keepdims=True)

    def coarse_step(i, k):
        cand = k + lax.shift_left(jnp.int32(1), 15 - i)
        c = count_coarse(_key_to_float(cand * 65536).astype(BF16))
        return jnp.where(c >= topk, cand, k)

    k_coarse = lax.fori_loop(0, 16, coarse_step, jnp.full((1, tq), -(2 ** 15), I32))

    def fine_step(i, carry):
        t, n = carry
        cand = t + lax.shift_left(jnp.int32(1), 16 - i)
        cand_f = _key_to_float(cand)
        c = count(lambda s, kt: s >= cand_f)
        return jnp.where(c >= topk, cand, t), jnp.where(c >= topk, c, n)

    thr_key, n_sel = lax.fori_loop(0, 17, fine_step, ((k_coarse - 1) * 65536, jnp.zeros((1, tq), I32)))
    thr = _key_to_float(thr_key)
    few = thr == -jnp.inf
    thr_adm = jnp.where(few, jnp.finfo(F32).min, thr)
    excess_ties = jnp.max(jnp.where(few, 0, n_sel)) > topk

    def store_mask(sel_fn):
        def body(kt, carry):
            sc_ref[kt] = jnp.where(sel_fn(sc_ref[kt], kt), 0.0, NEG_BIG).astype(F32)
            return carry
        lax.fori_loop(0, nkt, body, 0)

    @pl.when(jnp.logical_not(excess_ties))
    def _():
        store_mask(lambda s, kt: s >= thr_adm)

    @pl.when(excess_ties)
    def _():
        need = topk - count(lambda s, kt: s > thr)
        nbits = max(1, (nkt_max * tk - 1).bit_length())

        def tie_step(step, cut):
            cand = cut + lax.shift_left(jnp.int32(1), nbits - 1 - step)
            c = count(lambda s, kt: (s == thr) & (kt * tk + krow < cand))
            return jnp.where(c < need, cand, cut)

        cut = lax.fori_loop(0, nbits, tie_step, jnp.zeros((1, tq), I32))
        cut = jnp.where(few, -1, cut)
        store_mask(lambda s, kt: (s > thr_adm) | ((s == thr) & (kt * tk + krow <= cut)))

    zeros_head = jnp.zeros((HEAD_DIM, tq), BF16)
    ones_rows = jnp.ones((PACKED_SUBLANES, tk), BF16)

    def masked_logits(kt, hd):
        qh = qt_ref[hd * HEAD_DIM:(hd + 1) * HEAD_DIM, :]
        qh = jnp.concatenate([qh, zeros_head] if hd % 2 == 0 else [zeros_head, qh], axis=0)
        k2 = k_ref[pl.ds(pl.multiple_of(kt * tk, tk), tk), (hd // 2) * LANES:(hd // 2 + 1) * LANES]
        return _dot(k2, qh) + sc_ref[kt]

    def weighted_values(kt, hd, p):
        pv = _dot(jnp.concatenate([vt_ref[kt, hd * HEAD_DIM:(hd + 1) * HEAD_DIM, :], ones_rows], axis=0), p)
        return pv[:HEAD_DIM, :], pv[HEAD_DIM:HEAD_DIM + 1, :]

    def attn_tile_plain(kt, carry):
        for hd in range(ATTN_HEADS):
            rows = slice(hd * HEAD_DIM, (hd + 1) * HEAD_DIM)
            o, l = weighted_values(kt, hd, jnp.exp2(masked_logits(kt, hd)).astype(BF16))
            l_ref[hd:hd + 1, :] += l
            ot_ref[rows, :] += o
        return carry

    def attn_tile_online(kt, carry):
        for hd in range(ATTN_HEADS):
            rows = slice(hd * HEAD_DIM, (hd + 1) * HEAD_DIM)
            s = masked_logits(kt, hd)
            m_old = m_ref[hd:hd + 1, :]
            m_new = jnp.maximum(m_old, jnp.max(s, axis=0, keepdims=True))
            alpha = jnp.exp2(m_old - m_new)
            m_ref[hd:hd + 1, :] = m_new
            o, l = weighted_values(kt, hd, jnp.exp2(s - m_new).astype(BF16))
            l_ref[hd:hd + 1, :] = alpha * l_ref[hd:hd + 1, :] + l
            ot_ref[rows, :] = alpha * ot_ref[rows, :] + o
        return carry

    l_ref[...] = jnp.zeros(l_ref.shape, F32)
    ot_ref[...] = jnp.zeros(ot_ref.shape, F32)
    lax.fori_loop(0, nkt, attn_tile_plain, 0)
    l_all = l_ref[...]
    unusable = jnp.where((l_all > 0.0) & (l_all < jnp.finfo(F32).max), 0, 1)

    @pl.when(jnp.max(unusable) > 0)
    def _():
        m_ref[...] = jnp.full(m_ref.shape, NEG_BIG, F32)
        l_ref[...] = jnp.zeros(l_ref.shape, F32)
        ot_ref[...] = jnp.zeros(ot_ref.shape, F32)
        lax.fori_loop(0, nkt, attn_tile_online, 0)

    for hd in range(ATTN_HEADS):
        rows = slice(hd * HEAD_DIM, (hd + 1) * HEAD_DIM)
        ot_ref[rows, :] = ot_ref[rows, :] / l_ref[hd:hd + 1, :]
    out_ref[...] = ot_ref[...].T.astype(out_ref.dtype)


def _dsa_call(qit, wit, qt, ki, k, vt, *, tq, topk):
    B, S, _ = k.shape
    _, nkt, _, tk = vt.shape
    kernel = functools.partial(_dsa_kernel, topk=topk)
    return pl.pallas_call(
        kernel,
        grid=(B, S // tq),
        in_specs=[
            pl.BlockSpec((None, IDX_HEADS * IDX_DIM, tq), lambda b, i: (b, 0, i)),
            pl.BlockSpec((None, IDX_HEADS, tq), lambda b, i: (b, 0, i)),
            pl.BlockSpec((None, ATTN_WIDTH, tq), lambda b, i: (b, 0, i)),
            pl.BlockSpec((None, S, LANES), lambda b, i: (b, 0, 0)),
            pl.BlockSpec((None, S, ATTN_WIDTH), lambda b, i: (b, 0, 0)),
            pl.BlockSpec((None, nkt, ATTN_WIDTH, tk), lambda b, i: (b, 0, 0, 0)),
        ],
        out_specs=pl.BlockSpec((None, tq, ATTN_WIDTH), lambda b, i: (b, i, 0)),
        out_shape=jax.ShapeDtypeStruct((B, S, ATTN_WIDTH), BF16),
        scratch_shapes=[
            pltpu.VMEM((nkt, tk, tq), F32),
            pltpu.VMEM((nkt, tk, tq), BF16),
            pltpu.VMEM((ATTN_WIDTH, tq), F32),
            pltpu.VMEM((ATTN_HEADS, tq), F32),
            pltpu.VMEM((ATTN_HEADS, tq), F32),
        ],
        compiler_params=pltpu.CompilerParams(
            dimension_semantics=("arbitrary", "arbitrary"), vmem_limit_bytes=VMEM_LIMIT_BYTES),
    )(qit, wit, qt, ki, k, vt)


def _memkv_kernel(mem_ref, g_ref, wkv_ref, kn_ref, k_ref, v_ref):
    m = _rms_rows(mem_ref[...], g_ref[...]).astype(BF16)
    kv = _dot(m, wkv_ref[...])
    for hd in range(MEM_HEADS):
        cols = slice(hd * MEM_HEAD_DIM, (hd + 1) * MEM_HEAD_DIM)
        k_ref[:, cols] = _rms_rows(kv[:, cols], kn_ref[...]).astype(k_ref.dtype)
    v_ref[...] = kv[:, MEM_WIDTH:].astype(v_ref.dtype)


def _memkv_call(mem, g, wkv, kn):
    B, M, D = mem.shape
    full = lambda shape: pl.BlockSpec(shape, lambda b: (0,) * len(shape))
    return pl.pallas_call(
        _memkv_kernel,
        grid=(B,),
        in_specs=[pl.BlockSpec((None, M, D), lambda b: (b, 0, 0)), full(g.shape), full(wkv.shape), full(kn.shape)],
        out_specs=(pl.BlockSpec((None, M, MEM_WIDTH), lambda b: (b, 0, 0)),
                   pl.BlockSpec((None, M, MEM_WIDTH), lambda b: (b, 0, 0))),
        out_shape=(jax.ShapeDtypeStruct((B, M, MEM_WIDTH), BF16), jax.ShapeDtypeStruct((B, M, MEM_WIDTH), BF16)),
        compiler_params=pltpu.CompilerParams(dimension_semantics=("arbitrary",), vmem_limit_bytes=VMEM_LIMIT_BYTES),
    )(mem, g, wkv, kn)


def _split_bf16(a):
    hi = a.astype(BF16)
    lo = (a - hi.astype(F32)).astype(BF16)
    return hi, lo


def _mid_kernel(x_ref, yp_ref, ya_ref, wo1_ref, wo2_ref, gx_ref, wq_ref, qn_ref, km_ref, vm_ref, wo_ref,
                gf_ref, wr_ref, br_ref, rows_ref, gsel_ref):
    tm, d = x_ref.shape
    x1 = x_ref[...] + _dot(yp_ref[...], wo1_ref[...]) + _dot(ya_ref[...], wo2_ref[...])

    h = _rms_rows(x1, gx_ref[...]).astype(BF16)
    q = _dot(h, wq_ref[...])
    heads = []
    for hd in range(MEM_HEADS):
        cols = slice(hd * MEM_HEAD_DIM, (hd + 1) * MEM_HEAD_DIM)
        qh = (_rms_rows(q[:, cols], qn_ref[...]) * (MEM_HEAD_DIM ** -0.5)).astype(BF16)
        s = _dot_nt(qh, km_ref[:, cols])
        p = jnp.exp(s - jnp.max(s, axis=-1, keepdims=True))
        p = p / jnp.sum(p, axis=-1, keepdims=True)
        heads.append(_dot(p.astype(BF16), vm_ref[:, cols]))
    o = jnp.concatenate(heads, axis=-1).astype(BF16)
    x2 = x1 + _dot(o, wo_ref[...])
    rows_ref[:, :d] = x2

    h_hi, h_lo = _split_bf16(_rms_rows(x2, gf_ref[...]))
    w_hi, w_lo = _split_bf16(wr_ref[...])
    hw = _dot(h_hi, jnp.concatenate([w_hi, w_lo], axis=1))
    logits = hw[:, :LANES] + (hw[:, LANES:] + _dot(h_lo, w_hi)) + br_ref[...]
    lane = lax.broadcasted_iota(I32, (tm, LANES), 1).astype(F32)
    neg_inf = -jnp.inf
    g_logit = jnp.where(lane < N_GROUPS, logits, neg_inf)
    g_max = jnp.max(g_logit, axis=-1, keepdims=True)
    g_sel = jnp.min(jnp.where(g_logit == g_max, lane, LANES), axis=-1, keepdims=True)
    g_w = 1.0 / jnp.sum(jnp.exp(g_logit - g_max), axis=-1, keepdims=True)
    e_lo = N_GROUPS + g_sel * EXPERTS_PER_GROUP
    in_group = (lane >= e_lo) & (lane < e_lo + EXPERTS_PER_GROUP)
    e_logit = jnp.where(in_group, logits, neg_inf)
    v1 = jnp.max(e_logit, axis=-1, keepdims=True)
    i1 = jnp.min(jnp.where(e_logit == v1, lane, LANES), axis=-1, keepdims=True)
    rest = jnp.where(lane == i1, neg_inf, e_logit)
    v2 = jnp.max(rest, axis=-1, keepdims=True)
    i2 = jnp.min(jnp.where(rest == v2, lane, LANES), axis=-1, keepdims=True)
    e2 = jnp.exp(v2 - v1)
    w1 = g_w / (1.0 + e2)
    w2 = g_w * e2 / (1.0 + e2)
    gates = jnp.where(lane == i1 - N_GROUPS, w1, 0.0) + jnp.where(lane == i2 - N_GROUPS, w2, 0.0)
    rows_ref[:, d:] = gates
    gsel_ref[...] = jnp.broadcast_to(g_sel, (tm, LANES)).T[0:1, :]


def _mid_call(x, ypool, yattn, wo1, wo2, gx, wq, qn, kmem, vmem, wo, gf, wr, br, *, tm):
    B, S, D = x.shape
    M = kmem.shape[1]
    nt = S // tm
    full = lambda shape: pl.BlockSpec(shape, lambda b, i: (0,) * len(shape))
    tile = lambda width: pl.BlockSpec((None, tm, width), lambda b, i: (b, i, 0))
    return pl.pallas_call(
        _mid_kernel,
        grid=(B, nt),
        in_specs=[tile(D), tile(POOL_WIDTH), tile(ATTN_WIDTH), full(wo1.shape), full(wo2.shape), full(gx.shape),
                  full(wq.shape), full(qn.shape),
                  pl.BlockSpec((None, M, MEM_WIDTH), lambda b, i: (b, 0, 0)),
                  pl.BlockSpec((None, M, MEM_WIDTH), lambda b, i: (b, 0, 0)),
                  full(wo.shape), full(gf.shape), full(wr.shape), full(br.shape)],
        out_specs=(tile(D + LANES), pl.BlockSpec((None, 1, tm), lambda b, i: (b * nt + i, 0, 0))),
        out_shape=(jax.ShapeDtypeStruct((B, S, D + LANES), F32),
                   jax.ShapeDtypeStruct((B * nt, 1, tm), F32)),
        compiler_params=pltpu.CompilerParams(
            dimension_semantics=("arbitrary", "arbitrary"), vmem_limit_bytes=VMEM_LIMIT_BYTES),
    )(x, ypool, yattn, wo1, wo2, gx, wq, qn, kmem, vmem, wo, gf, wr, br)


META_NUSED = 64
ROW_DMA_UNROLL = 8


def _route_kernel(gsel_ref, pos_ref, meta_ref, *, tr):
    nb, w = gsel_ref.shape
    gsel = gsel_ref[...]
    lane = lax.broadcasted_iota(I32, (1, LANES), 1)
    before = (lax.broadcasted_iota(I32, (w, w), 0) < lax.broadcasted_iota(I32, (w, w), 1)).astype(BF16)
    onehot = [(gsel == g).astype(F32) for g in range(N_GROUPS)]
    cnt = sum(jnp.where(lane == g, jnp.sum(onehot[g], axis=1, keepdims=True), 0.0) for g in range(N_GROUPS))
    run = jnp.zeros((1, LANES), F32)
    carries = []
    for b in range(nb):
        carries.append(run)
        run = run + cnt[b:b + 1]
    carry = jnp.concatenate(carries, axis=0)
    padded = jnp.floor((run + (tr - 1)) * (1.0 / tr)) * tr
    size = [jnp.sum(jnp.where(lane == g, padded, 0.0), axis=1, keepdims=True) for g in range(N_GROUPS)]
    start = [sum(size[:g], jnp.zeros((1, 1), F32)) for g in range(N_GROUPS)]
    total = sum(size, jnp.zeros((1, 1), F32))
    pos = jnp.zeros((nb, w), F32)
    for g in range(N_GROUPS):
        rank = _dot(onehot[g].astype(BF16), before)
        base = jnp.sum(jnp.where(lane == g, carry, 0.0), axis=1, keepdims=True) + start[g]
        pos = pos + onehot[g] * (rank + base)
    pos_ref[...] = pos.astype(I32)
    tile_row = (lane * tr).astype(F32)
    tile_group = jnp.zeros((1, LANES), I32)
    last_group = jnp.zeros((1, 1), I32)
    for g in range(N_GROUPS):
        tile_group = tile_group + jnp.where((tile_row >= start[g]) & (tile_row < start[g] + size[g]), g, 0)
        last_group = jnp.where(size[g] > 0, g, last_group)
    tile_group = jnp.where(tile_row < total, tile_group, last_group)
    meta_ref[...] = jnp.where(lane == META_NUSED, (total * (1.0 / tr)).astype(I32), tile_group)


def _route_call(gsel, *, tr):
    nb, w = gsel.shape
    return pl.pallas_call(
        functools.partial(_route_kernel, tr=tr),
        out_shape=(jax.ShapeDtypeStruct((nb, w), I32), jax.ShapeDtypeStruct((1, LANES), I32)),
        compiler_params=pltpu.CompilerParams(vmem_limit_bytes=VMEM_LIMIT_BYTES),
    )(gsel)


def _scatter_rows_kernel(pos_ref, src_ref, zeros_ref, dst_ref, sem):
    del zeros_ref
    tile = src_ref.shape[0]
    base = pl.program_id(0) * tile

    def body(i, carry):
        for u in range(ROW_DMA_UNROLL):
            r = i * ROW_DMA_UNROLL + u
            pltpu.make_async_copy(
                src_ref.at[pl.ds(r, 1)], dst_ref.at[pl.ds(pos_ref[base + r], 1)], sem).start(priority=u % 2)
        return carry

    lax.fori_loop(0, tile // ROW_DMA_UNROLL, body, 0)
    pltpu.make_async_copy(src_ref, dst_ref.at[pl.ds(0, tile)], sem).wait()


def _scatter_rows_call(pos, src, *, n_out, tile):
    n, width = src.shape
    any_spec = pl.BlockSpec(memory_space=pl.ANY)
    return pl.pallas_call(
        _scatter_rows_kernel,
        grid_spec=pltpu.PrefetchScalarGridSpec(
            num_scalar_prefetch=1, grid=(n // tile,),
            in_specs=[pl.BlockSpec((tile, width), lambda i, pos: (i, 0)), any_spec], out_specs=any_spec,
            scratch_shapes=[pltpu.SemaphoreType.DMA(())]),
        out_shape=jax.ShapeDtypeStruct((n_out, width), src.dtype),
        input_output_aliases={2: 0},
        compiler_params=pltpu.CompilerParams(dimension_semantics=("arbitrary",), has_side_effects=True),
    )(pos, src, jnp.zeros((n_out, width), src.dtype))


def _gather_rows_kernel(pos_ref, src_ref, out_ref, sem):
    tile = out_ref.shape[0]
    base = pl.program_id(0) * tile

    def body(i, carry):
        for u in range(ROW_DMA_UNROLL):
            r = i * ROW_DMA_UNROLL + u
            pltpu.make_async_copy(
                src_ref.at[pl.ds(pos_ref[base + r], 1)], out_ref.at[pl.ds(r, 1)], sem).start(priority=u % 2)
        return carry

    lax.fori_loop(0, tile // ROW_DMA_UNROLL, body, 0)
    pltpu.make_async_copy(src_ref.at[pl.ds(0, tile)], out_ref, sem).wait()


def _gather_rows_call(pos, src, *, tile):
    n = pos.shape[0]
    width = src.shape[1]
    return pl.pallas_call(
        _gather_rows_kernel,
        grid_spec=pltpu.PrefetchScalarGridSpec(
            num_scalar_prefetch=1, grid=(n // tile,),
            in_specs=[pl.BlockSpec(memory_space=pl.ANY)],
            out_specs=pl.BlockSpec((tile, width), lambda i, pos: (i, 0)),
            scratch_shapes=[pltpu.SemaphoreType.DMA(())]),
        out_shape=jax.ShapeDtypeStruct((n, width), src.dtype),
        compiler_params=pltpu.CompilerParams(dimension_semantics=("arbitrary",)),
    )(pos, src)


def _expert_kernel(meta_ref, rows_ref, gf_ref, wg_ref, wu_ref, wd_ref, out_ref):
    j = pl.program_id(0)
    tr, d = out_ref.shape
    n_exp, _, ff = wg_ref.shape

    @pl.when(j < meta_ref[META_NUSED])
    def _():
        x2 = rows_ref[:, :d]
        gates = rows_ref[:, d:]
        h = _rms_rows(x2, gf_ref[...]).astype(BF16)
        lane = lax.broadcasted_iota(I32, (tr, LANES), 1)
        first = meta_ref[j] * n_exp
        hid = []
        for e in range(n_exp):
            a = _dot(h, wg_ref[e].astype(BF16))
            b = _dot(h, wu_ref[e].astype(BF16))
            gate = jnp.sum(jnp.where(lane == first + e, gates, 0.0), axis=-1, keepdims=True)
            hid.append((a * jax.nn.sigmoid(a) * b * gate).astype(BF16))
        hid = jnp.concatenate(hid, axis=-1)
        out_ref[...] = x2 + _dot(hid, wd_ref[...].reshape(n_exp * ff, d).astype(BF16))

    @pl.when(j >= meta_ref[META_NUSED])
    def _():
        out_ref[...] = jnp.zeros_like(out_ref)


def _expert_call(meta, rows, gf, wg, wu, wd, *, tr):
    n_rows, width = rows.shape
    d = width - LANES
    _, n_exp, _, ff = wg.shape
    by_group = lambda shape: pl.BlockSpec(
        (None,) + shape, lambda j, meta: (meta[j], 0, 0, 0), pipeline_mode=pl.Buffered(1))
    return pl.pallas_call(
        _expert_kernel,
        grid_spec=pltpu.PrefetchScalarGridSpec(
            num_scalar_prefetch=1, grid=(n_rows // tr,),
            in_specs=[pl.BlockSpec((tr, width), lambda j, meta: (j, 0)),
                      pl.BlockSpec(gf.shape, lambda j, meta: (0, 0)),
                      by_group((n_exp, d, ff)), by_group((n_exp, d, ff)), by_group((n_exp, ff, d))],
            out_specs=pl.BlockSpec((tr, d), lambda j, meta: (j, 0))),
        out_shape=jax.ShapeDtypeStruct((n_rows, d), F32),
        compiler_params=pltpu.CompilerParams(dimension_semantics=("arbitrary",), vmem_limit_bytes=VMEM_LIMIT_BYTES),
    )(meta, rows, gf, wg, wu, wd)


def _rope_tables_t(seq_len, dim):
    inv = ROPE_THETA ** (-jnp.arange(0, dim, 2, dtype=F32) / dim)
    ang = jnp.arange(seq_len, dtype=F32)[:, None] * inv[None, :]
    ang = jnp.concatenate([ang, ang], axis=-1)
    sign = jnp.concatenate([-jnp.ones((dim // 2,), F32), jnp.ones((dim // 2,), F32)])
    return jnp.cos(ang).T, (jnp.sin(ang) * sign[None, :]).T


def _layer(x, mem, mix_norm, w_in, pool_w, pool_scale, q_norm, k_norm, idx_k_norm, w_out,
           xattn_norm, mem_norm, xattn_wq, xattn_wkv, xattn_q_norm, xattn_k_norm, xattn_wo,
           ffn_norm, router_group_w, router_group_b, router_expert_w, router_expert_b,
           expert_w_gate, expert_w_up, expert_w_down, *, tm, tm_mid, tq, tk, tr):
    B, S, D = x.shape
    topk = min(TOPK_MAX, S // 4)
    row = lambda v: v.reshape(1, -1).astype(F32)
    col = lambda v: jnp.broadcast_to(v.astype(F32)[:, None], (v.shape[0], tm))

    wu = w_in[:, :POOL_WIDTH].astype(BF16)
    pad = lambda n: jnp.zeros((n, D), F32)
    wt = jnp.concatenate([
        w_in[:, OFF_Q:OFF_KI].T, w_in[:, OFF_KI:OFF_WI].T, pad(LANES - IDX_DIM),
        w_in[:, OFF_WI:IN_COLS].T, pad(2 * SUBLANES - IDX_HEADS)], axis=0).astype(BF16)
    cos_t, sin_t = _rope_tables_t(S, HEAD_DIM)

    ypool, qt, k, vt, qit, ki, wit = _proj_call(
        x, row(mix_norm), wu, wt, pool_w.astype(BF16), row(pool_scale), col(q_norm), col(k_norm),
        col(idx_k_norm), cos_t, sin_t, tm=tm, tk=tk)
    yattn = _dsa_call(qit, wit, qt, ki, k, vt, tq=tq, topk=topk)
    kmem, vmem = _memkv_call(mem, row(mem_norm), xattn_wkv.astype(BF16), row(xattn_k_norm))

    n_logits = N_GROUPS + N_EXPERTS
    wr = jnp.concatenate([router_group_w, router_expert_w, jnp.zeros((D, LANES - n_logits), F32)], axis=1)
    br = jnp.concatenate([router_group_b, router_expert_b, jnp.zeros((LANES - n_logits,), F32)]).reshape(1, LANES)
    rows, gsel = _mid_call(
        x, ypool, yattn, w_out[:POOL_WIDTH].astype(BF16), w_out[POOL_WIDTH:].astype(BF16), row(xattn_norm),
        xattn_wq.astype(BF16), row(xattn_q_norm), kmem, vmem, xattn_wo.astype(BF16), row(ffn_norm), wr, br,
        tm=tm_mid)

    n_tok = B * S
    n_sorted = n_tok + N_GROUPS * tr
    pos, meta = _route_call(gsel.reshape(n_tok // tm_mid, tm_mid), tr=tr)
    pos = pos.reshape(n_tok)
    sorted_rows = _scatter_rows_call(pos, rows.reshape(n_tok, D + LANES), n_out=n_sorted, tile=tm_mid)
    sorted_out = _expert_call(
        meta.reshape(LANES), sorted_rows, row(ffn_norm), expert_w_gate, expert_w_up, expert_w_down, tr=tr)
    out = _gather_rows_call(pos, sorted_out, tile=tm_mid)
    return out.reshape(B, S, D)


def kernel(x, mem, mix_norm, w_in, pool_w, pool_scale, q_norm, k_norm, idx_k_norm, w_out, xattn_norm, mem_norm,
           xattn_wq, xattn_wkv, xattn_q_norm, xattn_k_norm, xattn_wo, ffn_norm, router_group_w, router_group_b,
           router_expert_w, router_expert_b, expert_w_gate, expert_w_up, expert_w_down):
    depth = mix_norm.shape[0]
    for l in range(depth):
        x = _layer(
            x, mem, mix_norm[l], w_in[l], pool_w[l], pool_scale[l], q_norm[l], k_norm[l], idx_k_norm[l], w_out[l],
            xattn_norm[l], mem_norm[l], xattn_wq[l], xattn_wkv[l], xattn_q_norm[l], xattn_k_norm[l], xattn_wo[l],
            ffn_norm[l], router_group_w[l], router_group_b[l], router_expert_w[l], router_expert_b[l],
            expert_w_gate[l], expert_w_up[l], expert_w_down[l], tm=1024, tm_mid=1024, tq=512, tk=512, tr=512)
    return x
```

```python
import functools
import math

import jax
import jax.numpy as jnp
from jax import lax
from jax.experimental import pallas as pl
from jax.experimental.pallas import tpu as pltpu

CHUNK = 64
POOL_WINDOWS = (2, 4, 8, 16)
POOL_GROUP = 128
POOL_WIDTH = POOL_GROUP * len(POOL_WINDOWS)
ATTN_HEADS = 8
HEAD_DIM = 64
ATTN_WIDTH = ATTN_HEADS * HEAD_DIM
IDX_HEADS = 8
IDX_DIM = 64
TOPK_MAX = 256
ROPE_THETA = 10000.0
MEM_HEADS = 4
MEM_HEAD_DIM = 128
MEM_WIDTH = MEM_HEADS * MEM_HEAD_DIM
N_GROUPS = 4
EXPERTS_PER_GROUP = 8
N_EXPERTS = N_GROUPS * EXPERTS_PER_GROUP
EPS = 1e-6
OFF_Q = POOL_WIDTH
OFF_K = OFF_Q + ATTN_WIDTH
OFF_V = OFF_K + ATTN_WIDTH
OFF_QI = OFF_V + ATTN_WIDTH
OFF_KI = OFF_QI + IDX_HEADS * IDX_DIM
OFF_WI = OFF_KI + IDX_DIM
IN_COLS = OFF_WI + IDX_HEADS

LANES = 128
SUBLANES = 8
VMEM_LIMIT_BYTES = 56 * 1024 * 1024

HALO = 16
PACKED_SUBLANES = 2 * SUBLANES
NEG_BIG = -1e30
Q_SCALE = HEAD_DIM ** -0.5 * math.log2(math.e)

F32 = jnp.float32
BF16 = jnp.bfloat16
I32 = jnp.int32


def _dot(a, b):
    return jnp.dot(a, b, preferred_element_type=F32)


def _dot_nt(a, b):
    return lax.dot_general(a, b, (((1,), (1,)), ((), ())), preferred_element_type=F32)


def _rms_rows(x, g):
    ms = jnp.mean(x * x, axis=-1, keepdims=True)
    return x * lax.rsqrt(ms + EPS) * g


ROW_Q = 0
ROW_K = ROW_Q + ATTN_WIDTH
ROW_V = ROW_K + ATTN_WIDTH
ROW_QI = ROW_V + ATTN_WIDTH
ROW_KI = ROW_QI + IDX_HEADS * IDX_DIM
ROW_WI = ROW_KI + LANES
ROWS_T = ROW_WI + 2 * SUBLANES


def _norm_rope_t(z, gain, cos, sin_signed):
    if gain is not None:
        ms = jnp.mean(z * z, axis=0, keepdims=True)
        z = z * lax.rsqrt(ms + EPS) * gain
    half = z.shape[0] // 2
    swapped = jnp.concatenate([z[half:], z[:half]], axis=0)
    return z * cos + swapped * sin_signed


def _proj_kernel(x_ref, g_ref, wu_ref, wt_ref, poolw_ref, pscale_ref, qg_ref, kg_ref, ig_ref,
                 cos_ref, sin_ref,
                 ypool_ref, qt_ref, k_ref, vt_ref, qit_ref, ki_ref, wit_ref,
                 halo_ref, ext_ref, kt_ref, *, idx_scale):
    tm = x_ref.shape[0]
    h = _rms_rows(x_ref[...], g_ref[...]).astype(BF16)
    cos = cos_ref[...]
    sin = sin_ref[...]

    u = _dot(h, wu_ref[...])
    @pl.when(pl.program_id(1) == 0)
    def _():
        halo_ref[...] = jnp.zeros_like(halo_ref)

    ext_ref[0:HALO, :] = halo_ref[...]
    ext_ref[HALO:HALO + tm, :] = u
    halo_ref[...] = u[tm - HALO:, :]
    pos1 = pl.program_id(1) * tm + lax.broadcasted_iota(I32, (tm, 1), 0) + 1
    for g, w in enumerate(POOL_WINDOWS):
        cols = slice(g * POOL_GROUP, (g + 1) * POOL_GROUP)
        win = u[:, cols]
        for j in range(1, w):
            win = win + ext_ref[HALO - j:HALO - j + tm, cols]
        cnt = jnp.minimum(pos1, w).astype(F32)
        mixed = (win / cnt - u[:, cols]).astype(BF16)
        y = _dot(mixed, poolw_ref[g]) * pscale_ref[:, cols]
        ypool_ref[:, cols] = y.astype(ypool_ref.dtype)

    qg = qg_ref[...]
    kg = kg_ref[...]
    zq = _dot_nt(wt_ref[ROW_Q:ROW_K, :], h)
    for hd in range(ATTN_HEADS):
        rows = slice(hd * HEAD_DIM, (hd + 1) * HEAD_DIM)
        qt_ref[rows, :] = (_norm_rope_t(zq[rows], qg, cos, sin) * Q_SCALE).astype(qt_ref.dtype)
    zk = _dot_nt(wt_ref[ROW_K:ROW_V, :], h)
    for hd in range(ATTN_HEADS):
        rows = slice(hd * HEAD_DIM, (hd + 1) * HEAD_DIM)
        kt_ref[rows, :] = _norm_rope_t(zk[rows], kg, cos, sin)
    k_ref[...] = kt_ref[...].T.astype(k_ref.dtype)
    vt = _dot_nt(wt_ref[ROW_V:ROW_QI, :], h).astype(vt_ref.dtype)
    tk = vt_ref.shape[-1]
    for c in range(vt_ref.shape[0]):
        vt_ref[c] = vt[:, c * tk:(c + 1) * tk]
    zqi = _dot_nt(wt_ref[ROW_QI:ROW_KI, :], h)
    for hd in range(IDX_HEADS):
        rows = slice(hd * IDX_DIM, (hd + 1) * IDX_DIM)
        qit_ref[rows, :] = _norm_rope_t(zqi[rows], None, cos, sin).astype(qit_ref.dtype)
    zi = _dot_nt(wt_ref[ROW_KI:ROWS_T, :], h)
    kit = _norm_rope_t(zi[0:IDX_DIM], ig_ref[...], cos, sin)
    kit = jnp.concatenate([kit, jnp.zeros((LANES - IDX_DIM, tm), F32)], axis=0)
    ki_ref[...] = kit.T.astype(ki_ref.dtype)
    wit_ref[...] = zi[LANES:LANES + IDX_HEADS] * idx_scale


def _proj_call(x, g, wu, wt, poolw, pscale, qg, kg, ig, cos_t, sin_t, *, tm, tk):
    B, S, D = x.shape
    nt = S // tm
    kernel = functools.partial(_proj_kernel, idx_scale=(IDX_DIM ** -0.5) * (IDX_HEADS ** -0.5))
    full = lambda shape: pl.BlockSpec(shape, lambda b, i: (0,) * len(shape))
    out_shape = (
        jax.ShapeDtypeStruct((B, S, POOL_WIDTH), BF16),
        jax.ShapeDtypeStruct((B, ATTN_WIDTH, S), BF16),
        jax.ShapeDtypeStruct((B, S, ATTN_WIDTH), BF16),
        jax.ShapeDtypeStruct((B, S // tk, ATTN_WIDTH, tk), BF16),
        jax.ShapeDtypeStruct((B, IDX_HEADS * IDX_DIM, S), BF16),
        jax.ShapeDtypeStruct((B, S, LANES), BF16),
        jax.ShapeDtypeStruct((B, IDX_HEADS, S), F32),
    )
    return pl.pallas_call(
        kernel,
        grid=(B, nt),
        in_specs=[
            pl.BlockSpec((None, tm, D), lambda b, i: (b, i, 0)),
            full(g.shape), full(wu.shape), full(wt.shape), full(poolw.shape), full(pscale.shape),
            full(qg.shape), full(kg.shape), full(ig.shape),
            pl.BlockSpec((HEAD_DIM, tm), lambda b, i: (0, i)),
            pl.BlockSpec((HEAD_DIM, tm), lambda b, i: (0, i)),
        ],
        out_specs=(
            pl.BlockSpec((None, tm, POOL_WIDTH), lambda b, i: (b, i, 0)),
            pl.BlockSpec((None, ATTN_WIDTH, tm), lambda b, i: (b, 0, i)),
            pl.BlockSpec((None, tm, ATTN_WIDTH), lambda b, i: (b, i, 0)),
            pl.BlockSpec((None, tm // tk, ATTN_WIDTH, tk), lambda b, i: (b, i, 0, 0)),
            pl.BlockSpec((None, IDX_HEADS * IDX_DIM, tm), lambda b, i: (b, 0, i)),
            pl.BlockSpec((None, tm, LANES), lambda b, i: (b, i, 0)),
            pl.BlockSpec((None, IDX_HEADS, tm), lambda b, i: (b, 0, i)),
        ),
        out_shape=out_shape,
        scratch_shapes=[
            pltpu.VMEM((HALO, POOL_WIDTH), F32),
            pltpu.VMEM((HALO + tm, POOL_WIDTH), F32),
            pltpu.VMEM((ATTN_WIDTH, tm), F32),
        ],
        compiler_params=pltpu.CompilerParams(
            dimension_semantics=("arbitrary", "arbitrary"), vmem_limit_bytes=VMEM_LIMIT_BYTES),
    )(x, g, wu, wt, poolw, pscale, qg, kg, ig, cos_t, sin_t)


KEY_NEG_INF = -0x7F800000


def _key_to_float(key):
    key = jnp.maximum(key, KEY_NEG_INF)
    bits = jnp.where(key < 0, (key - 1) ^ jnp.int32(0x7FFFFFFF), key)
    return lax.bitcast_convert_type(bits, F32)


def _colsum8(x):
    rows, t = x.shape
    return jnp.sum(x.reshape(rows // SUBLANES, SUBLANES, t), axis=0)


def _dsa_kernel(qit_ref, wit_ref, qt_ref, ki_ref, k_ref, vt_ref, out_ref, sc_ref, coarse_ref, ot_ref, m_ref, l_ref,
                *, topk):
    tq = qit_ref.shape[1]
    nkt_max, tk, _ = sc_ref.shape
    q0 = pl.program_id(1) * tq
    nkt = (q0 + tq + tk - 1) // tk
    qpos = q0 + lax.broadcasted_iota(I32, (1, tq), 1)
    qend = (qpos // CHUNK + 1) * CHUNK
    krow = lax.broadcasted_iota(I32, (tk, 1), 0)
    zeros_half = jnp.zeros((LANES - IDX_DIM, tq), BF16)

    def score_tile(kt, carry):
        ki_t = ki_ref[pl.ds(pl.multiple_of(kt * tk, tk), tk), :]
        acc = jnp.zeros((tk, tq), F32)
        for hd in range(IDX_HEADS):
            qh = jnp.concatenate([qit_ref[hd * IDX_DIM:(hd + 1) * IDX_DIM, :], zeros_half], axis=0)
            rel = jnp.maximum(_dot(ki_t, qh), 0.0)
            acc = acc + rel * wit_ref[hd:hd + 1, :]
        score = jnp.where(kt * tk + krow < qend, acc, -jnp.inf)
        sc_ref[kt] = score
        coarse_ref[kt] = score.astype(BF16)
        return carry

    lax.fori_loop(0, nkt, score_tile, 0)

    def count(pred_fn):
        def body(kt, c8):
            return c8 + _colsum8(jnp.where(pred_fn(sc_ref[kt], kt), 1, 0).astype(I32))
        c8 = lax.fori_loop(0, nkt, body, jnp.zeros((SUBLANES, tq), I32))
        return jnp.sum(c8, axis=0, keepdims=True)

    def count_coarse(cand):
        def body(kt, c16):
            ones = jnp.where(coarse_ref[kt] >= cand, jnp.int16(1), jnp.int16(0))
            for j in range(tk // PACKED_SUBLANES):
                c16 = c16 + ones[j * PACKED_SUBLANES:(j + 1) * PACKED_SUBLANES]
            return c16
        c16 = lax.fori_loop(0, nkt, body, jnp.zeros((PACKED_SUBLANES, tq), jnp.int16))
        return jnp.sum(c16.astype(I32), axis=0, keepdims=True)

    def coarse_step(i, k):
        cand = k + lax.shift_left(jnp.int32(1), 15 - i)
        c = count_coarse(_key_to_float(cand * 65536).astype(BF16))
        return jnp.where(c >= topk, cand, k)

    k_coarse = lax.fori_loop(0, 16, coarse_step, jnp.full((1, tq), -(2 ** 15), I32))

    def fine_step(i, carry):
        t, n = carry
        cand = t + lax.shift_left(jnp.int32(1), 16 - i)
        cand_f = _key_to_float(cand)
        c = count(lambda s, kt: s >= cand_f)
        return jnp.where(c >= topk, cand, t), jnp.where(c >= topk, c, n)

    thr_key, n_sel = lax.fori_loop(0, 17, fine_step, ((k_coarse - 1) * 65536, jnp.zeros((1, tq), I32)))
    thr = _key_to_float(thr_key)
    few = thr == -jnp.inf
    thr_adm = jnp.where(few, jnp.finfo(F32).min, thr)
    excess_ties = jnp.max(jnp.where(few, 0, n_sel)) > topk

    def store_mask(sel_fn):
        def body(kt, carry):
            sc_ref[kt] = jnp.where(sel_fn(sc_ref[kt], kt), 0.0, NEG_BIG).astype(F32)
            return carry
        lax.fori_loop(0, nkt, body, 0)

    @pl.when(jnp.logical_not(excess_ties))
    def _():
        store_mask(lambda s, kt: s >= thr_adm)

    @pl.when(excess_ties)
    def _():
        need = topk - count(lambda s, kt: s > thr)
        nbits = max(1, (nkt_max * tk - 1).bit_length())

        def tie_step(step, cut):
            cand = cut + lax.shift_left(jnp.int32(1), nbits - 1 - step)
            c = count(lambda s, kt: (s == thr) & (kt * tk + krow < cand))
            return jnp.where(c < need, cand, cut)

        cut = lax.fori_loop(0, nbits, tie_step, jnp.zeros((1, tq), I32))
        cut = jnp.where(few, -1, cut)
        store_mask(lambda s, kt: (s > thr_adm) | ((s == thr) & (kt * tk + krow <= cut)))

    zeros_head = jnp.zeros((HEAD_DIM, tq), BF16)
    ones_rows = jnp.ones((PACKED_SUBLANES, tk), BF16)

    def masked_logits(kt, hd):
        qh = qt_ref[hd * HEAD_DIM:(hd + 1) * HEAD_DIM, :]
        qh = jnp.concatenate([qh, zeros_head] if hd % 2 == 0 else [zeros_head, qh], axis=0)
        k2 = k_ref[pl.ds(pl.multiple_of(kt * tk, tk), tk), (hd // 2) * LANES:(hd // 2 + 1) * LANES]
        return _dot(k2, qh) + sc_ref[kt]

    def weighted_values(kt, hd, p):
        pv = _dot(jnp.concatenate([vt_ref[kt, hd * HEAD_DIM:(hd + 1) * HEAD_DIM, :], ones_rows], axis=0), p)
        return pv[:HEAD_DIM, :], pv[HEAD_DIM:HEAD_DIM + 1, :]

    def attn_tile_plain(kt, carry):
        for hd in range(ATTN_HEADS):
            rows = slice(hd * HEAD_DIM, (hd + 1) * HEAD_DIM)
            o, l = weighted_values(kt, hd, jnp.exp2(masked_logits(kt, hd)).astype(BF16))
            l_ref[hd:hd + 1, :] += l
            ot_ref[rows, :] += o
        return carry

    def attn_tile_online(kt, carry):
        for hd in range(ATTN_HEADS):
            rows = slice(hd * HEAD_DIM, (hd + 1) * HEAD_DIM)
            s = masked_logits(kt, hd)
            m_old = m_ref[hd:hd + 1, :]
            m_new = jnp.maximum(m_old, jnp.max(s, axis=0, keepdims=True))
            alpha = jnp.exp2(m_old - m_new)
            m_ref[hd:hd + 1, :] = m_new
            o, l = weighted_values(kt, hd, jnp.exp2(s - m_new).astype(BF16))
            l_ref[hd:hd + 1, :] = alpha * l_ref[hd:hd + 1, :] + l
            ot_ref[rows, :] = alpha * ot_ref[rows, :] + o
        return carry

    l_ref[...] = jnp.zeros(l_ref.shape, F32)
    ot_ref[...] = jnp.zeros(ot_ref.shape, F32)
    lax.fori_loop(0, nkt, attn_tile_plain, 0)
    l_all = l_ref[...]
    unusable = jnp.where((l_all > 0.0) & (l_all < jnp.finfo(F32).max), 0, 1)

    @pl.when(jnp.max(unusable) > 0)
    def _():
        m_ref[...] = jnp.full(m_ref.shape, NEG_BIG, F32)
        l_ref[...] = jnp.zeros(l_ref.shape, F32)
        ot_ref[...] = jnp.zeros(ot_ref.shape, F32)
        lax.fori_loop(0, nkt, attn_tile_online, 0)

    for hd in range(ATTN_HEADS):
        rows = slice(hd * HEAD_DIM, (hd + 1) * HEAD_DIM)
        ot_ref[rows, :] = ot_ref[rows, :] / l_ref[hd:hd + 1, :]
    out_ref[...] = ot_ref[...].T.astype(out_ref.dtype)


def _dsa_call(qit, wit, qt, ki, k, vt, *, tq, topk):
    B, S, _ = k.shape
    _, nkt, _, tk = vt.shape
    kernel = functools.partial(_dsa_kernel, topk=topk)
    return pl.pallas_call(
        kernel,
        grid=(B, S // tq),
        in_specs=[
            pl.BlockSpec((None, IDX_HEADS * IDX_DIM, tq), lambda b, i: (b, 0, i)),
            pl.BlockSpec((None, IDX_HEADS, tq), lambda b, i: (b, 0, i)),
            pl.BlockSpec((None, ATTN_WIDTH, tq), lambda b, i: (b, 0, i)),
            pl.BlockSpec((None, S, LANES), lambda b, i: (b, 0, 0)),
            pl.BlockSpec((None, S, ATTN_WIDTH), lambda b, i: (b, 0, 0)),
            pl.BlockSpec((None, nkt, ATTN_WIDTH, tk), lambda b, i: (b, 0, 0, 0)),
        ],
        out_specs=pl.BlockSpec((None, tq, ATTN_WIDTH), lambda b, i: (b, i, 0)),
        out_shape=jax.ShapeDtypeStruct((B, S, ATTN_WIDTH), BF16),
        scratch_shapes=[
            pltpu.VMEM((nkt, tk, tq), F32),
            pltpu.VMEM((nkt, tk, tq), BF16),
            pltpu.VMEM((ATTN_WIDTH, tq), F32),
            pltpu.VMEM((ATTN_HEADS, tq), F32),
            pltpu.VMEM((ATTN_HEADS, tq), F32),
        ],
        compiler_params=pltpu.CompilerParams(
            dimension_semantics=("arbitrary", "arbitrary"), vmem_limit_bytes=VMEM_LIMIT_BYTES),
    )(qit, wit, qt, ki, k, vt)


def _memkv_kernel(mem_ref, g_ref, wkv_ref, kn_ref, k_ref, v_ref):
    m = _rms_rows(mem_ref[...], g_ref[...]).astype(BF16)
    kv = _dot(m, wkv_ref[...])
    for hd in range(MEM_HEADS):
        cols = slice(hd * MEM_HEAD_DIM, (hd + 1) * MEM_HEAD_DIM)
        k_ref[:, cols] = _rms_rows(kv[:, cols], kn_ref[...]).astype(k_ref.dtype)
    v_ref[...] = kv[:, MEM_WIDTH:].astype(v_ref.dtype)


def _memkv_call(mem, g, wkv, kn):
    B, M, D = mem.shape
    full = lambda shape: pl.BlockSpec(shape, lambda b: (0,) * len(shape))
    return pl.pallas_call(
        _memkv_kernel,
        grid=(B,),
        in_specs=[pl.BlockSpec((None, M, D), lambda b: (b, 0, 0)), full(g.shape), full(wkv.shape), full(kn.shape)],
        out_specs=(pl.BlockSpec((None, M, MEM_WIDTH), lambda b: (b, 0, 0)),
                   pl.BlockSpec((None, M, MEM_WIDTH), lambda b: (b, 0, 0))),
        out_shape=(jax.ShapeDtypeStruct((B, M, MEM_WIDTH), BF16), jax.ShapeDtypeStruct((B, M, MEM_WIDTH), BF16)),
        compiler_params=pltpu.CompilerParams(dimension_semantics=("arbitrary",), vmem_limit_bytes=VMEM_LIMIT_BYTES),
    )(mem, g, wkv, kn)


def _split_bf16(a):
    hi = a.astype(BF16)
    lo = (a - hi.astype(F32)).astype(BF16)
    return hi, lo


def _mid_kernel(x_ref, yp_ref, ya_ref, wo1_ref, wo2_ref, gx_ref, wq_ref, qn_ref, km_ref, vm_ref, wo_ref,
                gf_ref, wr_ref, br_ref, rows_ref, gsel_ref):
    tm, d = x_ref.shape
    x1 = x_ref[...] + _dot(yp_ref[...], wo1_ref[...]) + _dot(ya_ref[...], wo2_ref[...])

    h = _rms_rows(x1, gx_ref[...]).astype(BF16)
    q = _dot(h, wq_ref[...])
    heads = []
    for hd in range(MEM_HEADS):
        cols = slice(hd * MEM_HEAD_DIM, (hd + 1) * MEM_HEAD_DIM)
        qh = (_rms_rows(q[:, cols], qn_ref[...]) * (MEM_HEAD_DIM ** -0.5)).astype(BF16)
        s = _dot_nt(qh, km_ref[:, cols])
        p = jnp.exp(s - jnp.max(s, axis=-1, keepdims=True))
        p = p / jnp.sum(p, axis=-1, keepdims=True)
        heads.append(_dot(p.astype(BF16), vm_ref[:, cols]))
    o = jnp.concatenate(heads, axis=-1).astype(BF16)
    x2 = x1 + _dot(o, wo_ref[...])
    rows_ref[:, :d] = x2

    h_hi, h_lo = _split_bf16(_rms_rows(x2, gf_ref[...]))
    w_hi, w_lo = _split_bf16(wr_ref[...])
    hw = _dot(h_hi, jnp.concatenate([w_hi, w_lo], axis=1))
    logits = hw[:, :LANES] + (hw[:, LANES:] + _dot(h_lo, w_hi)) + br_ref[...]
    lane = lax.broadcasted_iota(I32, (tm, LANES), 1).astype(F32)
    neg_inf = -jnp.inf
    g_logit = jnp.where(lane < N_GROUPS, logits, neg_inf)
    g_max = jnp.max(g_logit, axis=-1, keepdims=True)
    g_sel = jnp.min(jnp.where(g_logit == g_max, lane, LANES), axis=-1, keepdims=True)
    g_w = 1.0 / jnp.sum(jnp.exp(g_logit - g_max), axis=-1, keepdims=True)
    e_lo = N_GROUPS + g_sel * EXPERTS_PER_GROUP
    in_group = (lane >= e_lo) & (lane < e_lo + EXPERTS_PER_GROUP)
    e_logit = jnp.where(in_group, logits, neg_inf)
    v1 = jnp.max(e_logit, axis=-1, keepdims=True)
    i1 = jnp.min(jnp.where(e_logit == v1, lane, LANES), axis=-1, keepdims=True)
    rest = jnp.where(lane == i1, neg_inf, e_logit)
    v2 = jnp.max(rest, axis=-1, keepdims=True)
    i2 = jnp.min(jnp.where(rest == v2, lane, LANES), axis=-1, keepdims=True)
    e2 = jnp.exp(v2 - v1)
    w1 = g_w / (1.0 + e2)
    w2 = g_w * e2 / (1.0 + e2)
    gates = jnp.where(lane == i1 - N_GROUPS, w1, 0.0) + jnp.where(lane == i2 - N_GROUPS, w2, 0.0)
    rows_ref[:, d:] = gates
    gsel_ref[...] = jnp.broadcast_to(g_sel, (tm, LANES)).T[0:1, :]


def _mid_call(x, ypool, yattn, wo1, wo2, gx, wq, qn, kmem, vmem, wo, gf, wr, br, *, tm):
    B, S, D = x.shape
    M = kmem.shape[1]
    nt = S // tm
    full = lambda shape: pl.BlockSpec(shape, lambda b, i: (0,) * len(shape))
    tile = lambda width: pl.BlockSpec((None, tm, width), lambda b, i: (b, i, 0))
    return pl.pallas_call(
        _mid_kernel,
        grid=(B, nt),
        in_specs=[tile(D), tile(POOL_WIDTH), tile(ATTN_WIDTH), full(wo1.shape), full(wo2.shape), full(gx.shape),
                  full(wq.shape), full(qn.shape),
                  pl.BlockSpec((None, M, MEM_WIDTH), lambda b, i: (b, 0, 0)),
                  pl.BlockSpec((None, M, MEM_WIDTH), lambda b, i: (b, 0, 0)),
                  full(wo.shape), full(gf.shape), full(wr.shape), full(br.shape)],
        out_specs=(tile(D + LANES), pl.BlockSpec((None, 1, tm), lambda b, i: (b * nt + i, 0, 0))),
        out_shape=(jax.ShapeDtypeStruct((B, S, D + LANES), F32),
                   jax.ShapeDtypeStruct((B * nt, 1, tm), F32)),
        compiler_params=pltpu.CompilerParams(
            dimension_semantics=("arbitrary", "arbitrary"), vmem_limit_bytes=VMEM_LIMIT_BYTES),
    )(x, ypool, yattn, wo1, wo2, gx, wq, qn, kmem, vmem, wo, gf, wr, br)


META_NUSED = 64
ROW_DMA_UNROLL = 8


def _route_kernel(gsel_ref, pos_ref, meta_ref, *, tr):
    nb, w = gsel_ref.shape
    gsel = gsel_ref[...]
    lane = lax.broadcasted_iota(I32, (1, LANES), 1)
    before = (lax.broadcasted_iota(I32, (w, w), 0) < lax.broadcasted_iota(I32, (w, w), 1)).astype(BF16)
    onehot = [(gsel == g).astype(F32) for g in range(N_GROUPS)]
    cnt = sum(jnp.where(lane == g, jnp.sum(onehot[g], axis=1, keepdims=True), 0.0) for g in range(N_GROUPS))
    run = jnp.zeros((1, LANES), F32)
    carries = []
    for b in range(nb):
        carries.append(run)
        run = run + cnt[b:b + 1]
    carry = jnp.concatenate(carries, axis=0)
    padded = jnp.floor((run + (tr - 1)) * (1.0 / tr)) * tr
    size = [jnp.sum(jnp.where(lane == g, padded, 0.0), axis=1, keepdims=True) for g in range(N_GROUPS)]
    start = [sum(size[:g], jnp.zeros((1, 1), F32)) for g in range(N_GROUPS)]
    total = sum(size, jnp.zeros((1, 1), F32))
    pos = jnp.zeros((nb, w), F32)
    for g in range(N_GROUPS):
        rank = _dot(onehot[g].astype(BF16), before)
        base = jnp.sum(jnp.where(lane == g, carry, 0.0), axis=1, keepdims=True) + start[g]
        pos = pos + onehot[g] * (rank + base)
    pos_ref[...] = pos.astype(I32)
    tile_row = (lane * tr).astype(F32)
    tile_group = jnp.zeros((1, LANES), I32)
    last_group = jnp.zeros((1, 1), I32)
    for g in range(N_GROUPS):
        tile_group = tile_group + jnp.where((tile_row >= start[g]) & (tile_row < start[g] + size[g]), g, 0)
        last_group = jnp.where(size[g] > 0, g, last_group)
    tile_group = jnp.where(tile_row < total, tile_group, last_group)
    meta_ref[...] = jnp.where(lane == META_NUSED, (total * (1.0 / tr)).astype(I32), tile_group)


def _route_call(gsel, *, tr):
    nb, w = gsel.shape
    return pl.pallas_call(
        functools.partial(_route_kernel, tr=tr),
        out_shape=(jax.ShapeDtypeStruct((nb, w), I32), jax.ShapeDtypeStruct((1, LANES), I32)),
        compiler_params=pltpu.CompilerParams(vmem_limit_bytes=VMEM_LIMIT_BYTES),
    )(gsel)


def _scatter_rows_kernel(pos_ref, src_ref, zeros_ref, dst_ref, sem):
    del zeros_ref
    tile = src_ref.shape[0]
    base = pl.program_id(0) * tile

    def body(i, carry):
        for u in range(ROW_DMA_UNROLL):
            r = i * ROW_DMA_UNROLL + u
            pltpu.make_async_copy(
                src_ref.at[pl.ds(r, 1)], dst_ref.at[pl.ds(pos_ref[base + r], 1)], sem).start(priority=u % 2)
        return carry

    lax.fori_loop(0, tile // ROW_DMA_UNROLL, body, 0)
    pltpu.make_async_copy(src_ref, dst_ref.at[pl.ds(0, tile)], sem).wait()


def _scatter_rows_call(pos, src, *, n_out, tile):
    n, width = src.shape
    any_spec = pl.BlockSpec(memory_space=pl.ANY)
    return pl.pallas_call(
        _scatter_rows_kernel,
        grid_spec=pltpu.PrefetchScalarGridSpec(
            num_scalar_prefetch=1, grid=(n // tile,),
            in_specs=[pl.BlockSpec((tile, width), lambda i, pos: (i, 0)), any_spec], out_specs=any_spec,
            scratch_shapes=[pltpu.SemaphoreType.DMA(())]),
        out_shape=jax.ShapeDtypeStruct((n_out, width), src.dtype),
        input_output_aliases={2: 0},
        compiler_params=pltpu.CompilerParams(dimension_semantics=("arbitrary",), has_side_effects=True),
    )(pos, src, jnp.zeros((n_out, width), src.dtype))


def _gather_rows_kernel(pos_ref, src_ref, out_ref, sem):
    tile = out_ref.shape[0]
    base = pl.program_id(0) * tile

    def body(i, carry):
        for u in range(ROW_DMA_UNROLL):
            r = i * ROW_DMA_UNROLL + u
            pltpu.make_async_copy(
                src_ref.at[pl.ds(pos_ref[base + r], 1)], out_ref.at[pl.ds(r, 1)], sem).start(priority=u % 2)
        return carry

    lax.fori_loop(0, tile // ROW_DMA_UNROLL, body, 0)
    pltpu.make_async_copy(src_ref.at[pl.ds(0, tile)], out_ref, sem).wait()


def _gather_rows_call(pos, src, *, tile):
    n = pos.shape[0]
    width = src.shape[1]
    return pl.pallas_call(
        _gather_rows_kernel,
        grid_spec=pltpu.PrefetchScalarGridSpec(
            num_scalar_prefetch=1, grid=(n // tile,),
            in_specs=[pl.BlockSpec(memory_space=pl.ANY)],
            out_specs=pl.BlockSpec((tile, width), lambda i, pos: (i, 0)),
            scratch_shapes=[pltpu.SemaphoreType.DMA(())]),
        out_shape=jax.ShapeDtypeStruct((n, width), src.dtype),
        compiler_params=pltpu.CompilerParams(dimension_semantics=("arbitrary",)),
    )(pos, src)


def _expert_kernel(meta_ref, rows_ref, gf_ref, wg_ref, wu_ref, wd_ref, out_ref):
    j = pl.program_id(0)
    tr, d = out_ref.shape
    n_exp, _, ff = wg_ref.shape

    @pl.when(j < meta_ref[META_NUSED])
    def _():
        x2 = rows_ref[:, :d]
        gates = rows_ref[:, d:]
        h = _rms_rows(x2, gf_ref[...]).astype(BF16)
        lane = lax.broadcasted_iota(I32, (tr, LANES), 1)
        first = meta_ref[j] * n_exp
        hid = []
        for e in range(n_exp):
            a = _dot(h, wg_ref[e].astype(BF16))
            b = _dot(h, wu_ref[e].astype(BF16))
            gate = jnp.sum(jnp.where(lane == first + e, gates, 0.0), axis=-1, keepdims=True)
            hid.append((a * jax.nn.sigmoid(a) * b * gate).astype(BF16))
        hid = jnp.concatenate(hid, axis=-1)
        out_ref[...] = x2 + _dot(hid, wd_ref[...].reshape(n_exp * ff, d).astype(BF16))

    @pl.when(j >= meta_ref[META_NUSED])
    def _():
        out_ref[...] = jnp.zeros_like(out_ref)


def _expert_call(meta, rows, gf, wg, wu, wd, *, tr):
    n_rows, width = rows.shape
    d = width - LANES
    _, n_exp, _, ff = wg.shape
    by_group = lambda shape, buffers: pl.BlockSpec(
        (None,) + shape, lambda j, meta: (meta[j], 0, 0, 0), pipeline_mode=pl.Buffered(buffers))
    return pl.pallas_call(
        _expert_kernel,
        grid_spec=pltpu.PrefetchScalarGridSpec(
            num_scalar_prefetch=1, grid=(n_rows // tr,),
            in_specs=[pl.BlockSpec((tr, width), lambda j, meta: (j, 0)),
                      pl.BlockSpec(gf.shape, lambda j, meta: (0, 0)),
                      by_group((n_exp, d, ff), 2), by_group((n_exp, d, ff), 2), by_group((n_exp, ff, d), 1)],
            out_specs=pl.BlockSpec((tr, d), lambda j, meta: (j, 0))),
        out_shape=jax.ShapeDtypeStruct((n_rows, d), F32),
        compiler_params=pltpu.CompilerParams(dimension_semantics=("arbitrary",), vmem_limit_bytes=VMEM_LIMIT_BYTES),
    )(meta, rows, gf, wg, wu, wd)


def _rope_tables_t(seq_len, dim):
    inv = ROPE_THETA ** (-jnp.arange(0, dim, 2, dtype=F32) / dim)
    ang = jnp.arange(seq_len, dtype=F32)[:, None] * inv[None, :]
    ang = jnp.concatenate([ang, ang], axis=-1)
    sign = jnp.concatenate([-jnp.ones((dim // 2,), F32), jnp.ones((dim // 2,), F32)])
    return jnp.cos(ang).T, (jnp.sin(ang) * sign[None, :]).T


def _layer(x, mem, mix_norm, w_in, pool_w, pool_scale, q_norm, k_norm, idx_k_norm, w_out,
           xattn_norm, mem_norm, xattn_wq, xattn_wkv, xattn_q_norm, xattn_k_norm, xattn_wo,
           ffn_norm, router_group_w, router_group_b, router_expert_w, router_expert_b,
           expert_w_gate, expert_w_up, expert_w_down, *, tm, tm_mid, tq, tk, tr):
    B, S, D = x.shape
    topk = min(TOPK_MAX, S // 4)
    row = lambda v: v.reshape(1, -1).astype(F32)
    col = lambda v: jnp.broadcast_to(v.astype(F32)[:, None], (v.shape[0], tm))

    wu = w_in[:, :POOL_WIDTH].astype(BF16)
    pad = lambda n: jnp.zeros((n, D), F32)
    wt = jnp.concatenate([
        w_in[:, OFF_Q:OFF_KI].T, w_in[:, OFF_KI:OFF_WI].T, pad(LANES - IDX_DIM),
        w_in[:, OFF_WI:IN_COLS].T, pad(2 * SUBLANES - IDX_HEADS)], axis=0).astype(BF16)
    cos_t, sin_t = _rope_tables_t(S, HEAD_DIM)

    ypool, qt, k, vt, qit, ki, wit = _proj_call(
        x, row(mix_norm), wu, wt, pool_w.astype(BF16), row(pool_scale), col(q_norm), col(k_norm),
        col(idx_k_norm), cos_t, sin_t, tm=tm, tk=tk)
    yattn = _dsa_call(qit, wit, qt, ki, k, vt, tq=tq, topk=topk)
    kmem, vmem = _memkv_call(mem, row(mem_norm), xattn_wkv.astype(BF16), row(xattn_k_norm))

    n_logits = N_GROUPS + N_EXPERTS
    wr = jnp.concatenate([router_group_w, router_expert_w, jnp.zeros((D, LANES - n_logits), F32)], axis=1)
    br = jnp.concatenate([router_group_b, router_expert_b, jnp.zeros((LANES - n_logits,), F32)]).reshape(1, LANES)
    rows, gsel = _mid_call(
        x, ypool, yattn, w_out[:POOL_WIDTH].astype(BF16), w_out[POOL_WIDTH:].astype(BF16), row(xattn_norm),
        xattn_wq.astype(BF16), row(xattn_q_norm), kmem, vmem, xattn_wo.astype(BF16), row(ffn_norm), wr, br,
        tm=tm_mid)

    n_tok = B * S
    n_sorted = n_tok + N_GROUPS * tr
    pos, meta = _route_call(gsel.reshape(n_tok // tm_mid, tm_mid), tr=tr)
    pos = pos.reshape(n_tok)
    sorted_rows = _scatter_rows_call(pos, rows.reshape(n_tok, D + LANES), n_out=n_sorted, tile=tm_mid)
    sorted_out = _expert_call(
        meta.reshape(LANES), sorted_rows, row(ffn_norm), expert_w_gate, expert_w_up, expert_w_down, tr=tr)
    out = _gather_rows_call(pos, sorted_out, tile=tm_mid)
    return out.reshape(B, S, D)


def kernel(x, mem, mix_norm, w_in, pool_w, pool_scale, q_norm, k_norm, idx_k_norm, w_out, xattn_norm, mem_norm,
           xattn_wq, xattn_wkv, xattn_q_norm, xattn_k_norm, xattn_wo, ffn_norm, router_group_w, router_group_b,
           router_expert_w, router_expert_b, expert_w_gate, expert_w_up, expert_w_down):
    depth = mix_norm.shape[0]
    for l in range(depth):
        x = _layer(
            x, mem, mix_norm[l], w_in[l], pool_w[l], pool_scale[l], q_norm[l], k_norm[l], idx_k_norm[l], w_out[l],
            xattn_norm[l], mem_norm[l], xattn_wq[l], xattn_wkv[l], xattn_q_norm[l], xattn_k_norm[l], xattn_wo[l],
            ffn_norm[l], router_group_w[l], router_group_b[l], router_expert_w[l], router_expert_b[l],
            expert_w_gate[l], expert_w_up[l], expert_w_down[l], tm=1024, tm_mid=1024, tq=512, tk=512, tr=512)
    return x
```

```python
import functools
import math

import jax
import jax.numpy as jnp
from jax import lax
from jax.experimental import pallas as pl
from jax.experimental.pallas import tpu as pltpu

CHUNK = 64
POOL_WINDOWS = (2, 4, 8, 16)
POOL_GROUP = 128
POOL_WIDTH = POOL_GROUP * len(POOL_WINDOWS)
ATTN_HEADS = 8
HEAD_DIM = 64
ATTN_WIDTH = ATTN_HEADS * HEAD_DIM
IDX_HEADS = 8
IDX_DIM = 64
TOPK_MAX = 256
ROPE_THETA = 10000.0
MEM_HEADS = 4
MEM_HEAD_DIM = 128
MEM_WIDTH = MEM_HEADS * MEM_HEAD_DIM
N_GROUPS = 4
EXPERTS_PER_GROUP = 8
N_EXPERTS = N_GROUPS * EXPERTS_PER_GROUP
EPS = 1e-6
OFF_Q = POOL_WIDTH
OFF_K = OFF_Q + ATTN_WIDTH
OFF_V = OFF_K + ATTN_WIDTH
OFF_QI = OFF_V + ATTN_WIDTH
OFF_KI = OFF_QI + IDX_HEADS * IDX_DIM
OFF_WI = OFF_KI + IDX_DIM
IN_COLS = OFF_WI + IDX_HEADS

LANES = 128
SUBLANES = 8
VMEM_LIMIT_BYTES = 56 * 1024 * 1024

HALO = 16
PACKED_SUBLANES = 2 * SUBLANES
NEG_BIG = -1e30
Q_SCALE = HEAD_DIM ** -0.5 * math.log2(math.e)

F32 = jnp.float32
BF16 = jnp.bfloat16
I32 = jnp.int32


def _dot(a, b):
    return jnp.dot(a, b, preferred_element_type=F32)


def _dot_nt(a, b):
    return lax.dot_general(a, b, (((1,), (1,)), ((), ())), preferred_element_type=F32)


def _rms_rows(x, g):
    ms = jnp.mean(x * x, axis=-1, keepdims=True)
    return x * lax.rsqrt(ms + EPS) * g


ROW_Q = 0
ROW_K = ROW_Q + ATTN_WIDTH
ROW_V = ROW_K + ATTN_WIDTH
ROW_QI = ROW_V + ATTN_WIDTH
ROW_KI = ROW_QI + IDX_HEADS * IDX_DIM
ROW_WI = ROW_KI + LANES
ROWS_T = ROW_WI + 2 * SUBLANES


def _norm_rope_t(z, gain, cos, sin_signed):
    if gain is not None:
        ms = jnp.mean(z * z, axis=0, keepdims=True)
        z = z * lax.rsqrt(ms + EPS) * gain
    half = z.shape[0] // 2
    swapped = jnp.concatenate([z[half:], z[:half]], axis=0)
    return z * cos + swapped * sin_signed


def _proj_kernel(x_ref, g_ref, wu_ref, wt_ref, poolw_ref, pscale_ref, qg_ref, kg_ref, ig_ref,
                 cos_ref, sin_ref,
                 ypool_ref, qt_ref, k_ref, vt_ref, qit_ref, ki_ref, wit_ref,
                 halo_ref, ext_ref, kt_ref, *, idx_scale):
    tm = x_ref.shape[0]
    h = _rms_rows(x_ref[...], g_ref[...]).astype(BF16)
    cos = cos_ref[...]
    sin = sin_ref[...]

    u = _dot(h, wu_ref[...])
    @pl.when(pl.program_id(1) == 0)
    def _():
        halo_ref[...] = jnp.zeros_like(halo_ref)

    ext_ref[0:HALO, :] = halo_ref[...]
    ext_ref[HALO:HALO + tm, :] = u
    halo_ref[...] = u[tm - HALO:, :]
    pos1 = pl.program_id(1) * tm + lax.broadcasted_iota(I32, (tm, 1), 0) + 1
    for g, w in enumerate(POOL_WINDOWS):
        cols = slice(g * POOL_GROUP, (g + 1) * POOL_GROUP)
        win = u[:, cols]
        for j in range(1, w):
            win = win + ext_ref[HALO - j:HALO - j + tm, cols]
        cnt = jnp.minimum(pos1, w).astype(F32)
        mixed = (win / cnt - u[:, cols]).astype(BF16)
        y = _dot(mixed, poolw_ref[g]) * pscale_ref[:, cols]
        ypool_ref[:, cols] = y.astype(ypool_ref.dtype)

    qg = qg_ref[...]
    kg = kg_ref[...]
    zq = _dot_nt(wt_ref[ROW_Q:ROW_K, :], h)
    for hd in range(ATTN_HEADS):
        rows = slice(hd * HEAD_DIM, (hd + 1) * HEAD_DIM)
        qt_ref[rows, :] = (_norm_rope_t(zq[rows], qg, cos, sin) * Q_SCALE).astype(qt_ref.dtype)
    zk = _dot_nt(wt_ref[ROW_K:ROW_V, :], h)
    for hd in range(ATTN_HEADS):
        rows = slice(hd * HEAD_DIM, (hd + 1) * HEAD_DIM)
        kt_ref[rows, :] = _norm_rope_t(zk[rows], kg, cos, sin)
    k_ref[...] = kt_ref[...].T.astype(k_ref.dtype)
    vt = _dot_nt(wt_ref[ROW_V:ROW_QI, :], h).astype(vt_ref.dtype)
    tk = vt_ref.shape[-1]
    for c in range(vt_ref.shape[0]):
        vt_ref[c] = vt[:, c * tk:(c + 1) * tk]
    zqi = _dot_nt(wt_ref[ROW_QI:ROW_KI, :], h)
    for hd in range(IDX_HEADS):
        rows = slice(hd * IDX_DIM, (hd + 1) * IDX_DIM)
        qit_ref[rows, :] = _norm_rope_t(zqi[rows], None, cos, sin).astype(qit_ref.dtype)
    zi = _dot_nt(wt_ref[ROW_KI:ROWS_T, :], h)
    kit = _norm_rope_t(zi[0:IDX_DIM], ig_ref[...], cos, sin)
    kit = jnp.concatenate([kit, jnp.zeros((LANES - IDX_DIM, tm), F32)], axis=0)
    ki_ref[...] = kit.T.astype(ki_ref.dtype)
    wit_ref[...] = zi[LANES:LANES + IDX_HEADS] * idx_scale


def _proj_call(x, g, wu, wt, poolw, pscale, qg, kg, ig, cos_t, sin_t, *, tm, tk):
    B, S, D = x.shape
    nt = S // tm
    kernel = functools.partial(_proj_kernel, idx_scale=(IDX_DIM ** -0.5) * (IDX_HEADS ** -0.5))
    full = lambda shape: pl.BlockSpec(shape, lambda b, i: (0,) * len(shape))
    out_shape = (
        jax.ShapeDtypeStruct((B, S, POOL_WIDTH), BF16),
        jax.ShapeDtypeStruct((B, ATTN_WIDTH, S), BF16),
        jax.ShapeDtypeStruct((B, S, ATTN_WIDTH), BF16),
        jax.ShapeDtypeStruct((B, S // tk, ATTN_WIDTH, tk), BF16),
        jax.ShapeDtypeStruct((B, IDX_HEADS * IDX_DIM, S), BF16),
        jax.ShapeDtypeStruct((B, S, LANES), BF16),
        jax.ShapeDtypeStruct((B, IDX_HEADS, S), F32),
    )
    return pl.pallas_call(
        kernel,
        grid=(B, nt),
        in_specs=[
            pl.BlockSpec((None, tm, D), lambda b, i: (b, i, 0)),
            full(g.shape), full(wu.shape), full(wt.shape), full(poolw.shape), full(pscale.shape),
            full(qg.shape), full(kg.shape), full(ig.shape),
            pl.BlockSpec((HEAD_DIM, tm), lambda b, i: (0, i)),
            pl.BlockSpec((HEAD_DIM, tm), lambda b, i: (0, i)),
        ],
        out_specs=(
            pl.BlockSpec((None, tm, POOL_WIDTH), lambda b, i: (b, i, 0)),
            pl.BlockSpec((None, ATTN_WIDTH, tm), lambda b, i: (b, 0, i)),
            pl.BlockSpec((None, tm, ATTN_WIDTH), lambda b, i: (b, i, 0)),
            pl.BlockSpec((None, tm // tk, ATTN_WIDTH, tk), lambda b, i: (b, i, 0, 0)),
            pl.BlockSpec((None, IDX_HEADS * IDX_DIM, tm), lambda b, i: (b, 0, i)),
            pl.BlockSpec((None, tm, LANES), lambda b, i: (b, i, 0)),
            pl.BlockSpec((None, IDX_HEADS, tm), lambda b, i: (b, 0, i)),
        ),
        out_shape=out_shape,
        scratch_shapes=[
            pltpu.VMEM((HALO, POOL_WIDTH), F32),
            pltpu.VMEM((HALO + tm, POOL_WIDTH), F32),
            pltpu.VMEM((ATTN_WIDTH, tm), F32),
        ],
        compiler_params=pltpu.CompilerParams(
            dimension_semantics=("arbitrary", "arbitrary"), vmem_limit_bytes=VMEM_LIMIT_BYTES),
    )(x, g, wu, wt, poolw, pscale, qg, kg, ig, cos_t, sin_t)


KEY_NEG_INF = -0x7F800000


def _key_to_float(key):
    key = jnp.maximum(key, KEY_NEG_INF)
    bits = jnp.where(key < 0, (key - 1) ^ jnp.int32(0x7FFFFFFF), key)
    return lax.bitcast_convert_type(bits, F32)


def _colsum8(x):
    rows, t = x.shape
    return jnp.sum(x.reshape(rows // SUBLANES, SUBLANES, t), axis=0)


def _dsa_kernel(qit_ref, wit_ref, qt_ref, ki_ref, k_ref, vt_ref, out_ref, sc_ref, coarse_ref, ot_ref, m_ref, l_ref,
                *, topk):
    tq = qit_ref.shape[1]
    nkt_max, tk, _ = sc_ref.shape
    q0 = pl.program_id(1) * tq
    nkt = (q0 + tq + tk - 1) // tk
    qpos = q0 + lax.broadcasted_iota(I32, (1, tq), 1)
    qend = (qpos // CHUNK + 1) * CHUNK
    krow = lax.broadcasted_iota(I32, (tk, 1), 0)
    zeros_half = jnp.zeros((LANES - IDX_DIM, tq), BF16)

    def score_tile(kt, carry):
        ki_t = ki_ref[pl.ds(pl.multiple_of(kt * tk, tk), tk), :]
        acc = jnp.zeros((tk, tq), F32)
        for hd in range(IDX_HEADS):
            qh = jnp.concatenate([qit_ref[hd * IDX_DIM:(hd + 1) * IDX_DIM, :], zeros_half], axis=0)
            rel = jnp.maximum(_dot(ki_t, qh), 0.0)
            acc = acc + rel * wit_ref[hd:hd + 1, :]
        score = jnp.where(kt * tk + krow < qend, acc, -jnp.inf)
        sc_ref[kt] = score
        coarse_ref[kt] = score.astype(BF16)
        return carry

    lax.fori_loop(0, nkt, score_tile, 0)

    def count(pred_fn):
        def body(kt, c8):
            return c8 + _colsum8(jnp.where(pred_fn(sc_ref[kt], kt), 1, 0).astype(I32))
        c8 = lax.fori_loop(0, nkt, body, jnp.zeros((SUBLANES, tq), I32))
        return jnp.sum(c8, axis=0, keepdims=True)

    def count_coarse(cand):
        def body(kt, c16):
            ones = jnp.where(coarse_ref[kt] >= cand, jnp.int16(1), jnp.int16(0))
            for j in range(tk // PACKED_SUBLANES):
                c16 = c16 + ones[j * PACKED_SUBLANES:(j + 1) * PACKED_SUBLANES]
            return c16
        c16 = lax.fori_loop(0, nkt, body, jnp.zeros((PACKED_SUBLANES, tq), jnp.int16))
        return jnp.sum(c16.astype(I32), axis=0, keepdims=True)

    def coarse_step(i, k):
        cand = k + lax.shift_left(jnp.int32(1), 15 - i)
        c = count_coarse(_key_to_float(cand * 65536).astype(BF16))
        return jnp.where(c >= topk, cand, k)

    k_coarse = lax.fori_loop(0, 16, coarse_step, jnp.full((1, tq), -(2 ** 15), I32))

    def fine_step(i, carry):
        t, n = carry
        cand = t + lax.shift_left(jnp.int32(1), 16 - i)
        cand_f = _key_to_float(cand)
        c = count(lambda s, kt: s >= cand_f)
        return jnp.where(c >= topk, cand, t), jnp.where(c >= topk, c, n)

    thr_key, n_sel = lax.fori_loop(0, 17, fine_step, ((k_coarse - 1) * 65536, jnp.zeros((1, tq), I32)))
    thr = _key_to_float(thr_key)
    few = thr == -jnp.inf
    thr_adm = jnp.where(few, jnp.finfo(F32).min, thr)
    excess_ties = jnp.max(jnp.where(few, 0, n_sel)) > topk

    def store_mask(sel_fn):
        def body(kt, carry):
            sc_ref[kt] = jnp.where(sel_fn(sc_ref[kt], kt), 0.0, NEG_BIG).astype(F32)
            return carry
        lax.fori_loop(0, nkt, body, 0)

    @pl.when(jnp.logical_not(excess_ties))
    def _():
        store_mask(lambda s, kt: s >= thr_adm)

    @pl.when(excess_ties)
    def _():
        need = topk - count(lambda s, kt: s > thr)
        nbits = max(1, (nkt_max * tk - 1).bit_length())

        def tie_step(step, cut):
            cand = cut + lax.shift_left(jnp.int32(1), nbits - 1 - step)
            c = count(lambda s, kt: (s == thr) & (kt * tk + krow < cand))
            return jnp.where(c < need, cand, cut)

        cut = lax.fori_loop(0, nbits, tie_step, jnp.zeros((1, tq), I32))
        cut = jnp.where(few, -1, cut)
        store_mask(lambda s, kt: (s > thr_adm) | ((s == thr) & (kt * tk + krow <= cut)))

    zeros_head = jnp.zeros((HEAD_DIM, tq), BF16)
    ones_rows = jnp.ones((PACKED_SUBLANES, tk), BF16)

    def masked_logits(kt, hd):
        qh = qt_ref[hd * HEAD_DIM:(hd + 1) * HEAD_DIM, :]
        qh = jnp.concatenate([qh, zeros_head] if hd % 2 == 0 else [zeros_head, qh], axis=0)
        k2 = k_ref[pl.ds(pl.multiple_of(kt * tk, tk), tk), (hd // 2) * LANES:(hd // 2 + 1) * LANES]
        return _dot(k2, qh) + sc_ref[kt]

    def weighted_values(kt, hd, p):
        pv = _dot(jnp.concatenate([vt_ref[kt, hd * HEAD_DIM:(hd + 1) * HEAD_DIM, :], ones_rows], axis=0), p)
        return pv[:HEAD_DIM, :], pv[HEAD_DIM:HEAD_DIM + 1, :]

    def attn_tile_plain(kt, carry):
        for hd in range(ATTN_HEADS):
            rows = slice(hd * HEAD_DIM, (hd + 1) * HEAD_DIM)
            o, l = weighted_values(kt, hd, jnp.exp2(masked_logits(kt, hd)).astype(BF16))
            l_ref[hd:hd + 1, :] += l
            ot_ref[rows, :] += o
        return carry

    def attn_tile_online(kt, carry):
        for hd in range(ATTN_HEADS):
            rows = slice(hd * HEAD_DIM, (hd + 1) * HEAD_DIM)
            s = masked_logits(kt, hd)
            m_old = m_ref[hd:hd + 1, :]
            m_new = jnp.maximum(m_old, jnp.max(s, axis=0, keepdims=True))
            alpha = jnp.exp2(m_old - m_new)
            m_ref[hd:hd + 1, :] = m_new
            o, l = weighted_values(kt, hd, jnp.exp2(s - m_new).astype(BF16))
            l_ref[hd:hd + 1, :] = alpha * l_ref[hd:hd + 1, :] + l
            ot_ref[rows, :] = alpha * ot_ref[rows, :] + o
        return carry

    l_ref[...] = jnp.zeros(l_ref.shape, F32)
    ot_ref[...] = jnp.zeros(ot_ref.shape, F32)
    lax.fori_loop(0, nkt, attn_tile_plain, 0)
    l_all = l_ref[...]
    unusable = jnp.where((l_all > 0.0) & (l_all < jnp.finfo(F32).max), 0, 1)

    @pl.when(jnp.max(unusable) > 0)
    def _():
        m_ref[...] = jnp.full(m_ref.shape, NEG_BIG, F32)
        l_ref[...] = jnp.zeros(l_ref.shape, F32)
        ot_ref[...] = jnp.zeros(ot_ref.shape, F32)
        lax.fori_loop(0, nkt, attn_tile_online, 0)

    for hd in range(ATTN_HEADS):
        rows = slice(hd * HEAD_DIM, (hd + 1) * HEAD_DIM)
        ot_ref[rows, :] = ot_ref[rows, :] / l_ref[hd:hd + 1, :]
    out_ref[...] = ot_ref[...].T.astype(out_ref.dtype)


def _dsa_call(qit, wit, qt, ki, k, vt, *, tq, topk):
    B, S, _ = k.shape
    _, nkt, _, tk = vt.shape
    kernel = functools.partial(_dsa_kernel, topk=topk)
    return pl.pallas_call(
        kernel,
        grid=(B, S // tq),
        in_specs=[
            pl.BlockSpec((None, IDX_HEADS * IDX_DIM, tq), lambda b, i: (b, 0, i)),
            pl.BlockSpec((None, IDX_HEADS, tq), lambda b, i: (b, 0, i)),
            pl.BlockSpec((None, ATTN_WIDTH, tq), lambda b, i: (b, 0, i)),
            pl.BlockSpec((None, S, LANES), lambda b, i: (b, 0, 0)),
            pl.BlockSpec((None, S, ATTN_WIDTH), lambda b, i: (b, 0, 0)),
            pl.BlockSpec((None, nkt, ATTN_WIDTH, tk), lambda b, i: (b, 0, 0, 0)),
        ],
        out_specs=pl.BlockSpec((None, tq, ATTN_WIDTH), lambda b, i: (b, i, 0)),
        out_shape=jax.ShapeDtypeStruct((B, S, ATTN_WIDTH), BF16),
        scratch_shapes=[
            pltpu.VMEM((nkt, tk, tq), F32),
            pltpu.VMEM((nkt, tk, tq), BF16),
            pltpu.VMEM((ATTN_WIDTH, tq), F32),
            pltpu.VMEM((ATTN_HEADS, tq), F32),
            pltpu.VMEM((ATTN_HEADS, tq), F32),
        ],
        compiler_params=pltpu.CompilerParams(
            dimension_semantics=("arbitrary", "arbitrary"), vmem_limit_bytes=VMEM_LIMIT_BYTES),
    )(qit, wit, qt, ki, k, vt)


def _memkv_kernel(mem_ref, g_ref, wkv_ref, kn_ref, k_ref, v_ref):
    m = _rms_rows(mem_ref[...], g_ref[...]).astype(BF16)
    kv = _dot(m, wkv_ref[...])
    for hd in range(MEM_HEADS):
        cols = slice(hd * MEM_HEAD_DIM, (hd + 1) * MEM_HEAD_DIM)
        k_ref[:, cols] = _rms_rows(kv[:, cols], kn_ref[...]).astype(k_ref.dtype)
    v_ref[...] = kv[:, MEM_WIDTH:].astype(v_ref.dtype)


def _memkv_call(mem, g, wkv, kn):
    B, M, D = mem.shape
    full = lambda shape: pl.BlockSpec(shape, lambda b: (0,) * len(shape))
    return pl.pallas_call(
        _memkv_kernel,
        grid=(B,),
        in_specs=[pl.BlockSpec((None, M, D), lambda b: (b, 0, 0)), full(g.shape), full(wkv.shape), full(kn.shape)],
        out_specs=(pl.BlockSpec((None, M, MEM_WIDTH), lambda b: (b, 0, 0)),
                   pl.BlockSpec((None, M, MEM_WIDTH), lambda b: (b, 0, 0))),
        out_shape=(jax.ShapeDtypeStruct((B, M, MEM_WIDTH), BF16), jax.ShapeDtypeStruct((B, M, MEM_WIDTH), BF16)),
        compiler_params=pltpu.CompilerParams(dimension_semantics=("arbitrary",), vmem_limit_bytes=VMEM_LIMIT_BYTES),
    )(mem, g, wkv, kn)


def _split_bf16(a):
    hi = a.astype(BF16)
    lo = (a - hi.astype(F32)).astype(BF16)
    return hi, lo


def _mid_kernel(x_ref, yp_ref, ya_ref, wo1_ref, wo2_ref, gx_ref, wq_ref, qn_ref, km_ref, vm_ref, wo_ref,
                gf_ref, wr_ref, br_ref, rows_ref, gsel_ref):
    tm, d = x_ref.shape
    x1 = x_ref[...] + _dot(yp_ref[...], wo1_ref[...]) + _dot(ya_ref[...], wo2_ref[...])

    h = _rms_rows(x1, gx_ref[...]).astype(BF16)
    q = _dot(h, wq_ref[...])
    heads = []
    for hd in range(MEM_HEADS):
        cols = slice(hd * MEM_HEAD_DIM, (hd + 1) * MEM_HEAD_DIM)
        qh = (_rms_rows(q[:, cols], qn_ref[...]) * (MEM_HEAD_DIM ** -0.5)).astype(BF16)
        s = _dot_nt(qh, km_ref[:, cols])
        p = jnp.exp(s - jnp.max(s, axis=-1, keepdims=True))
        p = p / jnp.sum(p, axis=-1, keepdims=True)
        heads.append(_dot(p.astype(BF16), vm_ref[:, cols]))
    o = jnp.concatenate(heads, axis=-1).astype(BF16)
    x2 = x1 + _dot(o, wo_ref[...])
    rows_ref[:, :d] = x2

    h_hi, h_lo = _split_bf16(_rms_rows(x2, gf_ref[...]))
    w_hi, w_lo = _split_bf16(wr_ref[...])
    hw = _dot(h_hi, jnp.concatenate([w_hi, w_lo], axis=1))
    logits = hw[:, :LANES] + (hw[:, LANES:] + _dot(h_lo, w_hi)) + br_ref[...]
    lane = lax.broadcasted_iota(I32, (tm, LANES), 1).astype(F32)
    neg_inf = -jnp.inf
    g_logit = jnp.where(lane < N_GROUPS, logits, neg_inf)
    g_max = jnp.max(g_logit, axis=-1, keepdims=True)
    g_sel = jnp.min(jnp.where(g_logit == g_max, lane, LANES), axis=-1, keepdims=True)
    g_w = 1.0 / jnp.sum(jnp.exp(g_logit - g_max), axis=-1, keepdims=True)
    e_lo = N_GROUPS + g_sel * EXPERTS_PER_GROUP
    in_group = (lane >= e_lo) & (lane < e_lo + EXPERTS_PER_GROUP)
    e_logit = jnp.where(in_group, logits, neg_inf)
    v1 = jnp.max(e_logit, axis=-1, keepdims=True)
    i1 = jnp.min(jnp.where(e_logit == v1, lane, LANES), axis=-1, keepdims=True)
    rest = jnp.where(lane == i1, neg_inf, e_logit)
    v2 = jnp.max(rest, axis=-1, keepdims=True)
    i2 = jnp.min(jnp.where(rest == v2, lane, LANES), axis=-1, keepdims=True)
    e2 = jnp.exp(v2 - v1)
    w1 = g_w / (1.0 + e2)
    w2 = g_w * e2 / (1.0 + e2)
    gates = jnp.where(lane == i1 - N_GROUPS, w1, 0.0) + jnp.where(lane == i2 - N_GROUPS, w2, 0.0)
    rows_ref[:, d:] = gates
    gsel_ref[...] = jnp.broadcast_to(g_sel, (tm, LANES)).T[0:1, :]


def _mid_call(x, ypool, yattn, wo1, wo2, gx, wq, qn, kmem, vmem, wo, gf, wr, br, *, tm):
    B, S, D = x.shape
    M = kmem.shape[1]
    nt = S // tm
    full = lambda shape: pl.BlockSpec(shape, lambda b, i: (0,) * len(shape))
    tile = lambda width: pl.BlockSpec((None, tm, width), lambda b, i: (b, i, 0))
    return pl.pallas_call(
        _mid_kernel,
        grid=(B, nt),
        in_specs=[tile(D), tile(POOL_WIDTH), tile(ATTN_WIDTH), full(wo1.shape), full(wo2.shape), full(gx.shape),
                  full(wq.shape), full(qn.shape),
                  pl.BlockSpec((None, M, MEM_WIDTH), lambda b, i: (b, 0, 0)),
                  pl.BlockSpec((None, M, MEM_WIDTH), lambda b, i: (b, 0, 0)),
                  full(wo.shape), full(gf.shape), full(wr.shape), full(br.shape)],
        out_specs=(tile(D + LANES), pl.BlockSpec((None, 1, tm), lambda b, i: (b * nt + i, 0, 0))),
        out_shape=(jax.ShapeDtypeStruct((B, S, D + LANES), F32),
                   jax.ShapeDtypeStruct((B * nt, 1, tm), F32)),
        compiler_params=pltpu.CompilerParams(
            dimension_semantics=("arbitrary", "arbitrary"), vmem_limit_bytes=VMEM_LIMIT_BYTES),
    )(x, ypool, yattn, wo1, wo2, gx, wq, qn, kmem, vmem, wo, gf, wr, br)


META_NUSED = 64
ROW_DMA_UNROLL = 8


def _route_kernel(gsel_ref, pos_ref, meta_ref, *, tr):
    nb, w = gsel_ref.shape
    gsel = gsel_ref[...]
    lane = lax.broadcasted_iota(I32, (1, LANES), 1)
    before = (lax.broadcasted_iota(I32, (w, w), 0) < lax.broadcasted_iota(I32, (w, w), 1)).astype(BF16)
    onehot = [(gsel == g).astype(F32) for g in range(N_GROUPS)]
    cnt = sum(jnp.where(lane == g, jnp.sum(onehot[g], axis=1, keepdims=True), 0.0) for g in range(N_GROUPS))
    run = jnp.zeros((1, LANES), F32)
    carries = []
    for b in range(nb):
        carries.append(run)
        run = run + cnt[b:b + 1]
    carry = jnp.concatenate(carries, axis=0)
    padded = jnp.floor((run + (tr - 1)) * (1.0 / tr)) * tr
    size = [jnp.sum(jnp.where(lane == g, padded, 0.0), axis=1, keepdims=True) for g in range(N_GROUPS)]
    start = [sum(size[:g], jnp.zeros((1, 1), F32)) for g in range(N_GROUPS)]
    total = sum(size, jnp.zeros((1, 1), F32))
    pos = jnp.zeros((nb, w), F32)
    for g in range(N_GROUPS):
        rank = _dot(onehot[g].astype(BF16), before)
        base = jnp.sum(jnp.where(lane == g, carry, 0.0), axis=1, keepdims=True) + start[g]
        pos = pos + onehot[g] * (rank + base)
    pos_ref[...] = pos.astype(I32)
    tile_row = (lane * tr).astype(F32)
    tile_group = jnp.zeros((1, LANES), I32)
    last_group = jnp.zeros((1, 1), I32)
    for g in range(N_GROUPS):
        tile_group = tile_group + jnp.where((tile_row >= start[g]) & (tile_row < start[g] + size[g]), g, 0)
        last_group = jnp.where(size[g] > 0, g, last_group)
    tile_group = jnp.where(tile_row < total, tile_group, last_group)
    meta_ref[...] = jnp.where(lane == META_NUSED, (total * (1.0 / tr)).astype(I32), tile_group)


def _route_call(gsel, *, tr):
    nb, w = gsel.shape
    return pl.pallas_call(
        functools.partial(_route_kernel, tr=tr),
        out_shape=(jax.ShapeDtypeStruct((nb, w), I32), jax.ShapeDtypeStruct((1, LANES), I32)),
        compiler_params=pltpu.CompilerParams(vmem_limit_bytes=VMEM_LIMIT_BYTES),
    )(gsel)


def _scatter_rows_kernel(pos_ref, src_ref, zeros_ref, dst_ref, sem):
    del zeros_ref
    tile = src_ref.shape[0]
    base = pl.program_id(0) * tile

    def body(i, carry):
        for u in range(ROW_DMA_UNROLL):
            r = i * ROW_DMA_UNROLL + u
            pltpu.make_async_copy(
                src_ref.at[pl.ds(r, 1)], dst_ref.at[pl.ds(pos_ref[base + r], 1)], sem).start(priority=u % 2)
        return carry

    lax.fori_loop(0, tile // ROW_DMA_UNROLL, body, 0)
    pltpu.make_async_copy(src_ref, dst_ref.at[pl.ds(0, tile)], sem).wait()


def _scatter_rows_call(pos, src, *, n_out, tile):
    n, width = src.shape
    any_spec = pl.BlockSpec(memory_space=pl.ANY)
    return pl.pallas_call(
        _scatter_rows_kernel,
        grid_spec=pltpu.PrefetchScalarGridSpec(
            num_scalar_prefetch=1, grid=(n // tile,),
            in_specs=[pl.BlockSpec((tile, width), lambda i, pos: (i, 0)), any_spec], out_specs=any_spec,
            scratch_shapes=[pltpu.SemaphoreType.DMA(())]),
        out_shape=jax.ShapeDtypeStruct((n_out, width), src.dtype),
        input_output_aliases={2: 0},
        compiler_params=pltpu.CompilerParams(dimension_semantics=("arbitrary",), has_side_effects=True),
    )(pos, src, jnp.zeros((n_out, width), src.dtype))


def _gather_rows_kernel(pos_ref, src_ref, out_ref, sem):
    tile = out_ref.shape[0]
    base = pl.program_id(0) * tile

    def body(i, carry):
        for u in range(ROW_DMA_UNROLL):
            r = i * ROW_DMA_UNROLL + u
            pltpu.make_async_copy(
                src_ref.at[pl.ds(pos_ref[base + r], 1)], out_ref.at[pl.ds(r, 1)], sem).start(priority=u % 2)
        return carry

    lax.fori_loop(0, tile // ROW_DMA_UNROLL, body, 0)
    pltpu.make_async_copy(src_ref.at[pl.ds(0, tile)], out_ref, sem).wait()


def _gather_rows_call(pos, src, *, tile):
    n = pos.shape[0]
    width = src.shape[1]
    return pl.pallas_call(
        _gather_rows_kernel,
        grid_spec=pltpu.PrefetchScalarGridSpec(
            num_scalar_prefetch=1, grid=(n // tile,),
            in_specs=[pl.BlockSpec(memory_space=pl.ANY)],
            out_specs=pl.BlockSpec((tile, width), lambda i, pos: (i, 0)),
            scratch_shapes=[pltpu.SemaphoreType.DMA(())]),
        out_shape=jax.ShapeDtypeStruct((n, width), src.dtype),
        compiler_params=pltpu.CompilerParams(dimension_semantics=("arbitrary",)),
    )(pos, src)


def _expert_kernel(meta_ref, rows_ref, gf_ref, wg_ref, wu_ref, wd_ref, out_ref):
    j = pl.program_id(0)
    tr, d = out_ref.shape
    n_exp, _, ff = wg_ref.shape

    @pl.when(j < meta_ref[META_NUSED])
    def _():
        x2 = rows_ref[:, :d]
        gates = rows_ref[:, d:]
        h = _rms_rows(x2, gf_ref[...]).astype(BF16)
        lane = lax.broadcasted_iota(I32, (tr, LANES), 1)
        first = meta_ref[j] * n_exp
        hid = []
        for e in range(n_exp):
            a = _dot(h, wg_ref[e].astype(BF16))
            b = _dot(h, wu_ref[e].astype(BF16))
            gate = jnp.sum(jnp.where(lane == first + e, gates, 0.0), axis=-1, keepdims=True)
            hid.append((a * jax.nn.sigmoid(a) * b * gate).astype(BF16))
        hid = jnp.concatenate(hid, axis=-1)
        out_ref[...] = x2 + _dot(hid, wd_ref[...].reshape(n_exp * ff, d).astype(BF16))

    @pl.when(j >= meta_ref[META_NUSED])
    def _():
        out_ref[...] = jnp.zeros_like(out_ref)


def _expert_call(meta, rows, gf, wg, wu, wd, *, tr):
    n_rows, width = rows.shape
    d = width - LANES
    _, n_exp, _, ff = wg.shape
    by_group = lambda shape, buffers: pl.BlockSpec(
        (None,) + shape, lambda j, meta: (meta[j], 0, 0, 0), pipeline_mode=pl.Buffered(buffers))
    return pl.pallas_call(
        _expert_kernel,
        grid_spec=pltpu.PrefetchScalarGridSpec(
            num_scalar_prefetch=1, grid=(n_rows // tr,),
            in_specs=[pl.BlockSpec((tr, width), lambda j, meta: (j, 0)),
                      pl.BlockSpec(gf.shape, lambda j, meta: (0, 0)),
                      by_group((n_exp, d, ff), 2), by_group((n_exp, d, ff), 2), by_group((n_exp, ff, d), 1)],
            out_specs=pl.BlockSpec((tr, d), lambda j, meta: (j, 0))),
        out_shape=jax.ShapeDtypeStruct((n_rows, d), F32),
        compiler_params=pltpu.CompilerParams(dimension_semantics=("arbitrary",), vmem_limit_bytes=VMEM_LIMIT_BYTES),
    )(meta, rows, gf, wg, wu, wd)


def _rope_tables_t(seq_len, dim):
    inv = ROPE_THETA ** (-jnp.arange(0, dim, 2, dtype=F32) / dim)
    ang = jnp.arange(seq_len, dtype=F32)[:, None] * inv[None, :]
    ang = jnp.concatenate([ang, ang], axis=-1)
    sign = jnp.concatenate([-jnp.ones((dim // 2,), F32), jnp.ones((dim // 2,), F32)])
    return jnp.cos(ang).T, (jnp.sin(ang) * sign[None, :]).T


def _layer(x, mem, mix_norm, w_in, pool_w, pool_scale, q_norm, k_norm, idx_k_norm, w_out,
           xattn_norm, mem_norm, xattn_wq, xattn_wkv, xattn_q_norm, xattn_k_norm, xattn_wo,
           ffn_norm, router_group_w, router_group_b, router_expert_w, router_expert_b,
           expert_w_gate, expert_w_up, expert_w_down, *, tm, tm_mid, tq, tk, tr, tp):
    B, S, D = x.shape
    topk = min(TOPK_MAX, S // 4)
    row = lambda v: v.reshape(1, -1).astype(F32)
    col = lambda v: jnp.broadcast_to(v.astype(F32)[:, None], (v.shape[0], tm))

    wu = w_in[:, :POOL_WIDTH].astype(BF16)
    pad = lambda n: jnp.zeros((n, D), F32)
    wt = jnp.concatenate([
        w_in[:, OFF_Q:OFF_KI].T, w_in[:, OFF_KI:OFF_WI].T, pad(LANES - IDX_DIM),
        w_in[:, OFF_WI:IN_COLS].T, pad(2 * SUBLANES - IDX_HEADS)], axis=0).astype(BF16)
    cos_t, sin_t = _rope_tables_t(S, HEAD_DIM)

    ypool, qt, k, vt, qit, ki, wit = _proj_call(
        x, row(mix_norm), wu, wt, pool_w.astype(BF16), row(pool_scale), col(q_norm), col(k_norm),
        col(idx_k_norm), cos_t, sin_t, tm=tm, tk=tk)
    yattn = _dsa_call(qit, wit, qt, ki, k, vt, tq=tq, topk=topk)
    kmem, vmem = _memkv_call(mem, row(mem_norm), xattn_wkv.astype(BF16), row(xattn_k_norm))

    n_logits = N_GROUPS + N_EXPERTS
    wr = jnp.concatenate([router_group_w, router_expert_w, jnp.zeros((D, LANES - n_logits), F32)], axis=1)
    br = jnp.concatenate([router_group_b, router_expert_b, jnp.zeros((LANES - n_logits,), F32)]).reshape(1, LANES)
    rows, gsel = _mid_call(
        x, ypool, yattn, w_out[:POOL_WIDTH].astype(BF16), w_out[POOL_WIDTH:].astype(BF16), row(xattn_norm),
        xattn_wq.astype(BF16), row(xattn_q_norm), kmem, vmem, xattn_wo.astype(BF16), row(ffn_norm), wr, br,
        tm=tm_mid)

    n_tok = B * S
    n_sorted = n_tok + N_GROUPS * tr
    pos, meta = _route_call(gsel.reshape(n_tok // tm_mid, tm_mid), tr=tr)
    pos = pos.reshape(n_tok)
    sorted_rows = _scatter_rows_call(pos, rows.reshape(n_tok, D + LANES), n_out=n_sorted, tile=tp)
    sorted_out = _expert_call(
        meta.reshape(LANES), sorted_rows, row(ffn_norm), expert_w_gate, expert_w_up, expert_w_down, tr=tr)
    out = _gather_rows_call(pos, sorted_out, tile=tp)
    return out.reshape(B, S, D)


def kernel(x, mem, mix_norm, w_in, pool_w, pool_scale, q_norm, k_norm, idx_k_norm, w_out, xattn_norm, mem_norm,
           xattn_wq, xattn_wkv, xattn_q_norm, xattn_k_norm, xattn_wo, ffn_norm, router_group_w, router_group_b,
           router_expert_w, router_expert_b, expert_w_gate, expert_w_up, expert_w_down):
    depth = mix_norm.shape[0]
    for l in range(depth):
        x = _layer(
            x, mem, mix_norm[l], w_in[l], pool_w[l], pool_scale[l], q_norm[l], k_norm[l], idx_k_norm[l], w_out[l],
            xattn_norm[l], mem_norm[l], xattn_wq[l], xattn_wkv[l], xattn_q_norm[l], xattn_k_norm[l], xattn_wo[l],
            ffn_norm[l], router_group_w[l], router_group_b[l], router_expert_w[l], router_expert_b[l],
            expert_w_gate[l], expert_w_up[l], expert_w_down[l], tm=1024, tm_mid=1024, tq=512, tk=512, tr=512, tp=2048)
    return x
```

```python
import functools
import math

import jax
import jax.numpy as jnp
from jax import lax
from jax.experimental import pallas as pl
from jax.experimental.pallas import tpu as pltpu

CHUNK = 64
POOL_WINDOWS = (2, 4, 8, 16)
POOL_GROUP = 128
POOL_WIDTH = POOL_GROUP * len(POOL_WINDOWS)
ATTN_HEADS = 8
HEAD_DIM = 64
ATTN_WIDTH = ATTN_HEADS * HEAD_DIM
IDX_HEADS = 8
IDX_DIM = 64
TOPK_MAX = 256
ROPE_THETA = 10000.0
MEM_HEADS = 4
MEM_HEAD_DIM = 128
MEM_WIDTH = MEM_HEADS * MEM_HEAD_DIM
N_GROUPS = 4
EXPERTS_PER_GROUP = 8
N_EXPERTS = N_GROUPS * EXPERTS_PER_GROUP
EPS = 1e-6
OFF_Q = POOL_WIDTH
OFF_K = OFF_Q + ATTN_WIDTH
OFF_V = OFF_K + ATTN_WIDTH
OFF_QI = OFF_V + ATTN_WIDTH
OFF_KI = OFF_QI + IDX_HEADS * IDX_DIM
OFF_WI = OFF_KI + IDX_DIM
IN_COLS = OFF_WI + IDX_HEADS

LANES = 128
SUBLANES = 8
VMEM_LIMIT_BYTES = 56 * 1024 * 1024

HALO = 16
PACKED_SUBLANES = 2 * SUBLANES
NEG_BIG = -1e30
Q_SCALE = HEAD_DIM ** -0.5 * math.log2(math.e)

F32 = jnp.float32
BF16 = jnp.bfloat16
I32 = jnp.int32


def _dot(a, b):
    return jnp.dot(a, b, preferred_element_type=F32)


def _dot_nt(a, b):
    return lax.dot_general(a, b, (((1,), (1,)), ((), ())), preferred_element_type=F32)


def _rms_rows(x, g):
    ms = jnp.mean(x * x, axis=-1, keepdims=True)
    return x * lax.rsqrt(ms + EPS) * g


ROW_Q = 0
ROW_K = ROW_Q + ATTN_WIDTH
ROW_V = ROW_K + ATTN_WIDTH
ROW_QI = ROW_V + ATTN_WIDTH
ROW_KI = ROW_QI + IDX_HEADS * IDX_DIM
ROW_WI = ROW_KI + LANES
ROWS_T = ROW_WI + 2 * SUBLANES


def _norm_rope_t(z, gain, cos, sin_signed):
    if gain is not None:
        ms = jnp.mean(z * z, axis=0, keepdims=True)
        z = z * lax.rsqrt(ms + EPS) * gain
    half = z.shape[0] // 2
    swapped = jnp.concatenate([z[half:], z[:half]], axis=0)
    return z * cos + swapped * sin_signed


def _proj_kernel(x_ref, g_ref, wu_ref, wt_ref, poolw_ref, pscale_ref, qg_ref, kg_ref, ig_ref,
                 cos_ref, sin_ref,
                 ypool_ref, qt_ref, k_ref, vt_ref, qit_ref, ki_ref, wit_ref,
                 halo_ref, ext_ref, kt_ref, *, idx_scale):
    tm = x_ref.shape[0]
    h = _rms_rows(x_ref[...], g_ref[...]).astype(BF16)
    cos = cos_ref[...]
    sin = sin_ref[...]

    u = _dot(h, wu_ref[...])
    @pl.when(pl.program_id(1) == 0)
    def _():
        halo_ref[...] = jnp.zeros_like(halo_ref)

    ext_ref[0:HALO, :] = halo_ref[...]
    ext_ref[HALO:HALO + tm, :] = u
    halo_ref[...] = u[tm - HALO:, :]
    pos1 = pl.program_id(1) * tm + lax.broadcasted_iota(I32, (tm, 1), 0) + 1
    for g, w in enumerate(POOL_WINDOWS):
        cols = slice(g * POOL_GROUP, (g + 1) * POOL_GROUP)
        win = u[:, cols]
        for j in range(1, w):
            win = win + ext_ref[HALO - j:HALO - j + tm, cols]
        cnt = jnp.minimum(pos1, w).astype(F32)
        mixed = (win / cnt - u[:, cols]).astype(BF16)
        y = _dot(mixed, poolw_ref[g]) * pscale_ref[:, cols]
        ypool_ref[:, cols] = y.astype(ypool_ref.dtype)

    qg = qg_ref[...]
    kg = kg_ref[...]
    zq = _dot_nt(wt_ref[ROW_Q:ROW_K, :], h)
    for hd in range(ATTN_HEADS):
        rows = slice(hd * HEAD_DIM, (hd + 1) * HEAD_DIM)
        qt_ref[rows, :] = (_norm_rope_t(zq[rows], qg, cos, sin) * Q_SCALE).astype(qt_ref.dtype)
    zk = _dot_nt(wt_ref[ROW_K:ROW_V, :], h)
    for hd in range(ATTN_HEADS):
        rows = slice(hd * HEAD_DIM, (hd + 1) * HEAD_DIM)
        kt_ref[rows, :] = _norm_rope_t(zk[rows], kg, cos, sin)
    k_ref[...] = kt_ref[...].T.astype(k_ref.dtype)
    vt = _dot_nt(wt_ref[ROW_V:ROW_QI, :], h).astype(vt_ref.dtype)
    tk = vt_ref.shape[-1]
    for c in range(vt_ref.shape[0]):
        vt_ref[c] = vt[:, c * tk:(c + 1) * tk]
    zqi = _dot_nt(wt_ref[ROW_QI:ROW_KI, :], h)
    for hd in range(IDX_HEADS):
        rows = slice(hd * IDX_DIM, (hd + 1) * IDX_DIM)
        qit_ref[rows, :] = _norm_rope_t(zqi[rows], None, cos, sin).astype(qit_ref.dtype)
    zi = _dot_nt(wt_ref[ROW_KI:ROWS_T, :], h)
    kit = _norm_rope_t(zi[0:IDX_DIM], ig_ref[...], cos, sin)
    kit = jnp.concatenate([kit, jnp.zeros((LANES - IDX_DIM, tm), F32)], axis=0)
    ki_ref[...] = kit.T.astype(ki_ref.dtype)
    wit_ref[...] = zi[LANES:LANES + IDX_HEADS] * idx_scale


def _proj_call(x, g, wu, wt, poolw, pscale, qg, kg, ig, cos_t, sin_t, *, tm, tk):
    B, S, D = x.shape
    nt = S // tm
    kernel = functools.partial(_proj_kernel, idx_scale=(IDX_DIM ** -0.5) * (IDX_HEADS ** -0.5))
    full = lambda shape: pl.BlockSpec(shape, lambda b, i: (0,) * len(shape))
    out_shape = (
        jax.ShapeDtypeStruct((B, S, POOL_WIDTH), BF16),
        jax.ShapeDtypeStruct((B, ATTN_WIDTH, S), BF16),
        jax.ShapeDtypeStruct((B, S, ATTN_WIDTH), BF16),
        jax.ShapeDtypeStruct((B, S // tk, ATTN_WIDTH, tk), BF16),
        jax.ShapeDtypeStruct((B, IDX_HEADS * IDX_DIM, S), BF16),
        jax.ShapeDtypeStruct((B, S, LANES), BF16),
        jax.ShapeDtypeStruct((B, IDX_HEADS, S), F32),
    )
    return pl.pallas_call(
        kernel,
        grid=(B, nt),
        in_specs=[
            pl.BlockSpec((None, tm, D), lambda b, i: (b, i, 0)),
            full(g.shape), full(wu.shape), full(wt.shape), full(poolw.shape), full(pscale.shape),
            full(qg.shape), full(kg.shape), full(ig.shape),
            pl.BlockSpec((HEAD_DIM, tm), lambda b, i: (0, i)),
            pl.BlockSpec((HEAD_DIM, tm), lambda b, i: (0, i)),
        ],
        out_specs=(
            pl.BlockSpec((None, tm, POOL_WIDTH), lambda b, i: (b, i, 0)),
            pl.BlockSpec((None, ATTN_WIDTH, tm), lambda b, i: (b, 0, i)),
            pl.BlockSpec((None, tm, ATTN_WIDTH), lambda b, i: (b, i, 0)),
            pl.BlockSpec((None, tm // tk, ATTN_WIDTH, tk), lambda b, i: (b, i, 0, 0)),
            pl.BlockSpec((None, IDX_HEADS * IDX_DIM, tm), lambda b, i: (b, 0, i)),
            pl.BlockSpec((None, tm, LANES), lambda b, i: (b, i, 0)),
            pl.BlockSpec((None, IDX_HEADS, tm), lambda b, i: (b, 0, i)),
        ),
        out_shape=out_shape,
        scratch_shapes=[
            pltpu.VMEM((HALO, POOL_WIDTH), F32),
            pltpu.VMEM((HALO + tm, POOL_WIDTH), F32),
            pltpu.VMEM((ATTN_WIDTH, tm), F32),
        ],
        compiler_params=pltpu.CompilerParams(
            dimension_semantics=("arbitrary", "arbitrary"), vmem_limit_bytes=VMEM_LIMIT_BYTES),
    )(x, g, wu, wt, poolw, pscale, qg, kg, ig, cos_t, sin_t)


KEY_NEG_INF = -0x7F800000
FINE_FIXED_STEPS = 9


def _key_to_float(key):
    key = jnp.maximum(key, KEY_NEG_INF)
    bits = jnp.where(key < 0, (key - 1) ^ jnp.int32(0x7FFFFFFF), key)
    return lax.bitcast_convert_type(bits, F32)


def _colsum8(x):
    rows, t = x.shape
    return jnp.sum(x.reshape(rows // SUBLANES, SUBLANES, t), axis=0)


def _dsa_kernel(qit_ref, wit_ref, qt_ref, ki_ref, k_ref, vt_ref, out_ref, sc_ref, coarse_ref, ot_ref, m_ref, l_ref,
                *, topk):
    tq = qit_ref.shape[1]
    nkt_max, tk, _ = sc_ref.shape
    q0 = pl.program_id(1) * tq
    nkt = (q0 + tq + tk - 1) // tk
    qpos = q0 + lax.broadcasted_iota(I32, (1, tq), 1)
    qend = (qpos // CHUNK + 1) * CHUNK
    krow = lax.broadcasted_iota(I32, (tk, 1), 0)
    zeros_half = jnp.zeros((LANES - IDX_DIM, tq), BF16)

    def score_tile(kt, carry):
        ki_t = ki_ref[pl.ds(pl.multiple_of(kt * tk, tk), tk), :]
        acc = jnp.zeros((tk, tq), F32)
        for hd in range(IDX_HEADS):
            qh = jnp.concatenate([qit_ref[hd * IDX_DIM:(hd + 1) * IDX_DIM, :], zeros_half], axis=0)
            rel = jnp.maximum(_dot(ki_t, qh), 0.0)
            acc = acc + rel * wit_ref[hd:hd + 1, :]
        score = jnp.where(kt * tk + krow < qend, acc, -jnp.inf)
        sc_ref[kt] = score
        coarse_ref[kt] = score.astype(BF16)
        return carry

    lax.fori_loop(0, nkt, score_tile, 0)

    def count(pred_fn):
        def body(kt, c8):
            return c8 + _colsum8(jnp.where(pred_fn(sc_ref[kt], kt), 1, 0).astype(I32))
        c8 = lax.fori_loop(0, nkt, body, jnp.zeros((SUBLANES, tq), I32))
        return jnp.sum(c8, axis=0, keepdims=True)

    def count_coarse(cand):
        def body(kt, c16):
            ones = jnp.where(coarse_ref[kt] >= cand, jnp.int16(1), jnp.int16(0))
            for j in range(tk // PACKED_SUBLANES):
                c16 = c16 + ones[j * PACKED_SUBLANES:(j + 1) * PACKED_SUBLANES]
            return c16
        c16 = lax.fori_loop(0, nkt, body, jnp.zeros((PACKED_SUBLANES, tq), jnp.int16))
        return jnp.sum(c16.astype(I32), axis=0, keepdims=True)

    def coarse_step(i, k):
        cand = k + lax.shift_left(jnp.int32(1), 15 - i)
        c = count_coarse(_key_to_float(cand * 65536).astype(BF16))
        return jnp.where(c >= topk, cand, k)

    k_coarse = lax.fori_loop(0, 16, coarse_step, jnp.full((1, tq), -(2 ** 15), I32))

    def fine_step(i, carry):
        t, n = carry
        cand = t + lax.shift_left(jnp.int32(1), 16 - i)
        cand_f = _key_to_float(cand)
        c = count(lambda s, kt: s >= cand_f)
        return jnp.where(c >= topk, cand, t), jnp.where(c >= topk, c, n)

    thr_key, n_sel = lax.fori_loop(
        0, FINE_FIXED_STEPS, fine_step, ((k_coarse - 1) * 65536, jnp.zeros((1, tq), I32)))

    def unresolved(carry):
        i, _, n = carry
        return (i < 17) & (jnp.max(jnp.where(n == topk, 0, 1)) > 0)

    def fine_more(carry):
        i, t, n = carry
        t, n = fine_step(i, (t, n))
        return i + 1, t, n

    _, thr_key, n_sel = lax.while_loop(unresolved, fine_more, (jnp.int32(FINE_FIXED_STEPS), thr_key, n_sel))
    thr = _key_to_float(thr_key)
    few = thr == -jnp.inf
    thr_adm = jnp.where(few, jnp.finfo(F32).min, thr)
    excess_ties = jnp.max(jnp.where(few, 0, n_sel)) > topk

    def store_mask(sel_fn):
        def body(kt, carry):
            sc_ref[kt] = jnp.where(sel_fn(sc_ref[kt], kt), 0.0, NEG_BIG).astype(F32)
            return carry
        lax.fori_loop(0, nkt, body, 0)

    @pl.when(jnp.logical_not(excess_ties))
    def _():
        store_mask(lambda s, kt: s >= thr_adm)

    @pl.when(excess_ties)
    def _():
        need = topk - count(lambda s, kt: s > thr)
        nbits = max(1, (nkt_max * tk - 1).bit_length())

        def tie_step(step, cut):
            cand = cut + lax.shift_left(jnp.int32(1), nbits - 1 - step)
            c = count(lambda s, kt: (s == thr) & (kt * tk + krow < cand))
            return jnp.where(c < need, cand, cut)

        cut = lax.fori_loop(0, nbits, tie_step, jnp.zeros((1, tq), I32))
        cut = jnp.where(few, -1, cut)
        store_mask(lambda s, kt: (s > thr_adm) | ((s == thr) & (kt * tk + krow <= cut)))

    zeros_head = jnp.zeros((HEAD_DIM, tq), BF16)
    ones_rows = jnp.ones((PACKED_SUBLANES, tk), BF16)

    def masked_logits(kt, hd):
        qh = qt_ref[hd * HEAD_DIM:(hd + 1) * HEAD_DIM, :]
        qh = jnp.concatenate([qh, zeros_head] if hd % 2 == 0 else [zeros_head, qh], axis=0)
        k2 = k_ref[pl.ds(pl.multiple_of(kt * tk, tk), tk), (hd // 2) * LANES:(hd // 2 + 1) * LANES]
        return _dot(k2, qh) + sc_ref[kt]

    def weighted_values(kt, hd, p):
        pv = _dot(jnp.concatenate([vt_ref[kt, hd * HEAD_DIM:(hd + 1) * HEAD_DIM, :], ones_rows], axis=0), p)
        return pv[:HEAD_DIM, :], pv[HEAD_DIM:HEAD_DIM + 1, :]

    def attn_tile_plain(kt, carry):
        for hd in range(ATTN_HEADS):
            rows = slice(hd * HEAD_DIM, (hd + 1) * HEAD_DIM)
            o, l = weighted_values(kt, hd, jnp.exp2(masked_logits(kt, hd)).astype(BF16))
            l_ref[hd:hd + 1, :] += l
            ot_ref[rows, :] += o
        return carry

    def attn_tile_online(kt, carry):
        for hd in range(ATTN_HEADS):
            rows = slice(hd * HEAD_DIM, (hd + 1) * HEAD_DIM)
            s = masked_logits(kt, hd)
            m_old = m_ref[hd:hd + 1, :]
            m_new = jnp.maximum(m_old, jnp.max(s, axis=0, keepdims=True))
            alpha = jnp.exp2(m_old - m_new)
            m_ref[hd:hd + 1, :] = m_new
            o, l = weighted_values(kt, hd, jnp.exp2(s - m_new).astype(BF16))
            l_ref[hd:hd + 1, :] = alpha * l_ref[hd:hd + 1, :] + l
            ot_ref[rows, :] = alpha * ot_ref[rows, :] + o
        return carry

    l_ref[...] = jnp.zeros(l_ref.shape, F32)
    ot_ref[...] = jnp.zeros(ot_ref.shape, F32)
    lax.fori_loop(0, nkt, attn_tile_plain, 0)
    l_all = l_ref[...]
    unusable = jnp.where((l_all > 0.0) & (l_all < jnp.finfo(F32).max), 0, 1)

    @pl.when(jnp.max(unusable) > 0)
    def _():
        m_ref[...] = jnp.full(m_ref.shape, NEG_BIG, F32)
        l_ref[...] = jnp.zeros(l_ref.shape, F32)
        ot_ref[...] = jnp.zeros(ot_ref.shape, F32)
        lax.fori_loop(0, nkt, attn_tile_online, 0)

    for hd in range(ATTN_HEADS):
        rows = slice(hd * HEAD_DIM, (hd + 1) * HEAD_DIM)
        ot_ref[rows, :] = ot_ref[rows, :] / l_ref[hd:hd + 1, :]
    out_ref[...] = ot_ref[...].T.astype(out_ref.dtype)


def _dsa_call(qit, wit, qt, ki, k, vt, *, tq, topk):
    B, S, _ = k.shape
    _, nkt, _, tk = vt.shape
    kernel = functools.partial(_dsa_kernel, topk=topk)
    return pl.pallas_call(
        kernel,
        grid=(B, S // tq),
        in_specs=[
            pl.BlockSpec((None, IDX_HEADS * IDX_DIM, tq), lambda b, i: (b, 0, i)),
            pl.BlockSpec((None, IDX_HEADS, tq), lambda b, i: (b, 0, i)),
            pl.BlockSpec((None, ATTN_WIDTH, tq), lambda b, i: (b, 0, i)),
            pl.BlockSpec((None, S, LANES), lambda b, i: (b, 0, 0)),
            pl.BlockSpec((None, S, ATTN_WIDTH), lambda b, i: (b, 0, 0)),
            pl.BlockSpec((None, nkt, ATTN_WIDTH, tk), lambda b, i: (b, 0, 0, 0)),
        ],
        out_specs=pl.BlockSpec((None, tq, ATTN_WIDTH), lambda b, i: (b, i, 0)),
        out_shape=jax.ShapeDtypeStruct((B, S, ATTN_WIDTH), BF16),
        scratch_shapes=[
            pltpu.VMEM((nkt, tk, tq), F32),
            pltpu.VMEM((nkt, tk, tq), BF16),
            pltpu.VMEM((ATTN_WIDTH, tq), F32),
            pltpu.VMEM((ATTN_HEADS, tq), F32),
            pltpu.VMEM((ATTN_HEADS, tq), F32),
        ],
        compiler_params=pltpu.CompilerParams(
            dimension_semantics=("arbitrary", "arbitrary"), vmem_limit_bytes=VMEM_LIMIT_BYTES),
    )(qit, wit, qt, ki, k, vt)


def _memkv_kernel(mem_ref, g_ref, wkv_ref, kn_ref, k_ref, v_ref):
    m = _rms_rows(mem_ref[...], g_ref[...]).astype(BF16)
    kv = _dot(m, wkv_ref[...])
    for hd in range(MEM_HEADS):
        cols = slice(hd * MEM_HEAD_DIM, (hd + 1) * MEM_HEAD_DIM)
        k_ref[:, cols] = _rms_rows(kv[:, cols], kn_ref[...]).astype(k_ref.dtype)
    v_ref[...] = kv[:, MEM_WIDTH:].astype(v_ref.dtype)


def _memkv_call(mem, g, wkv, kn):
    B, M, D = mem.shape
    full = lambda shape: pl.BlockSpec(shape, lambda b: (0,) * len(shape))
    return pl.pallas_call(
        _memkv_kernel,
        grid=(B,),
        in_specs=[pl.BlockSpec((None, M, D), lambda b: (b, 0, 0)), full(g.shape), full(wkv.shape), full(kn.shape)],
        out_specs=(pl.BlockSpec((None, M, MEM_WIDTH), lambda b: (b, 0, 0)),
                   pl.BlockSpec((None, M, MEM_WIDTH), lambda b: (b, 0, 0))),
        out_shape=(jax.ShapeDtypeStruct((B, M, MEM_WIDTH), BF16), jax.ShapeDtypeStruct((B, M, MEM_WIDTH), BF16)),
        compiler_params=pltpu.CompilerParams(dimension_semantics=("arbitrary",), vmem_limit_bytes=VMEM_LIMIT_BYTES),
    )(mem, g, wkv, kn)


def _split_bf16(a):
    hi = a.astype(BF16)
    lo = (a - hi.astype(F32)).astype(BF16)
    return hi, lo


def _mid_kernel(x_ref, yp_ref, ya_ref, wo1_ref, wo2_ref, gx_ref, wq_ref, qn_ref, km_ref, vm_ref, wo_ref,
                gf_ref, wr_ref, br_ref, rows_ref, gsel_ref):
    tm, d = x_ref.shape
    x1 = x_ref[...] + _dot(yp_ref[...], wo1_ref[...]) + _dot(ya_ref[...], wo2_ref[...])

    h = _rms_rows(x1, gx_ref[...]).astype(BF16)
    q = _dot(h, wq_ref[...])
    heads = []
    for hd in range(MEM_HEADS):
        cols = slice(hd * MEM_HEAD_DIM, (hd + 1) * MEM_HEAD_DIM)
        qh = (_rms_rows(q[:, cols], qn_ref[...]) * (MEM_HEAD_DIM ** -0.5)).astype(BF16)
        s = _dot_nt(qh, km_ref[:, cols])
        p = jnp.exp(s - jnp.max(s, axis=-1, keepdims=True))
        p = p / jnp.sum(p, axis=-1, keepdims=True)
        heads.append(_dot(p.astype(BF16), vm_ref[:, cols]))
    o = jnp.concatenate(heads, axis=-1).astype(BF16)
    x2 = x1 + _dot(o, wo_ref[...])
    rows_ref[:, :d] = x2

    h_hi, h_lo = _split_bf16(_rms_rows(x2, gf_ref[...]))
    w_hi, w_lo = _split_bf16(wr_ref[...])
    hw = _dot(h_hi, jnp.concatenate([w_hi, w_lo], axis=1))
    logits = hw[:, :LANES] + (hw[:, LANES:] + _dot(h_lo, w_hi)) + br_ref[...]
    lane = lax.broadcasted_iota(I32, (tm, LANES), 1).astype(F32)
    neg_inf = -jnp.inf
    g_logit = jnp.where(lane < N_GROUPS, logits, neg_inf)
    g_max = jnp.max(g_logit, axis=-1, keepdims=True)
    g_sel = jnp.min(jnp.where(g_logit == g_max, lane, LANES), axis=-1, keepdims=True)
    g_w = 1.0 / jnp.sum(jnp.exp(g_logit - g_max), axis=-1, keepdims=True)
    e_lo = N_GROUPS + g_sel * EXPERTS_PER_GROUP
    in_group = (lane >= e_lo) & (lane < e_lo + EXPERTS_PER_GROUP)
    e_logit = jnp.where(in_group, logits, neg_inf)
    v1 = jnp.max(e_logit, axis=-1, keepdims=True)
    i1 = jnp.min(jnp.where(e_logit == v1, lane, LANES), axis=-1, keepdims=True)
    rest = jnp.where(lane == i1, neg_inf, e_logit)
    v2 = jnp.max(rest, axis=-1, keepdims=True)
    i2 = jnp.min(jnp.where(rest == v2, lane, LANES), axis=-1, keepdims=True)
    e2 = jnp.exp(v2 - v1)
    w1 = g_w / (1.0 + e2)
    w2 = g_w * e2 / (1.0 + e2)
    gates = jnp.where(lane == i1 - N_GROUPS, w1, 0.0) + jnp.where(lane == i2 - N_GROUPS, w2, 0.0)
    rows_ref[:, d:] = gates
    gsel_ref[...] = jnp.broadcast_to(g_sel, (tm, LANES)).T[0:1, :]


def _mid_call(x, ypool, yattn, wo1, wo2, gx, wq, qn, kmem, vmem, wo, gf, wr, br, *, tm):
    B, S, D = x.shape
    M = kmem.shape[1]
    nt = S // tm
    full = lambda shape: pl.BlockSpec(shape, lambda b, i: (0,) * len(shape))
    tile = lambda width: pl.BlockSpec((None, tm, width), lambda b, i: (b, i, 0))
    return pl.pallas_call(
        _mid_kernel,
        grid=(B, nt),
        in_specs=[tile(D), tile(POOL_WIDTH), tile(ATTN_WIDTH), full(wo1.shape), full(wo2.shape), full(gx.shape),
                  full(wq.shape), full(qn.shape),
                  pl.BlockSpec((None, M, MEM_WIDTH), lambda b, i: (b, 0, 0)),
                  pl.BlockSpec((None, M, MEM_WIDTH), lambda b, i: (b, 0, 0)),
                  full(wo.shape), full(gf.shape), full(wr.shape), full(br.shape)],
        out_specs=(tile(D + LANES), pl.BlockSpec((None, 1, tm), lambda b, i: (b * nt + i, 0, 0))),
        out_shape=(jax.ShapeDtypeStruct((B, S, D + LANES), F32),
                   jax.ShapeDtypeStruct((B * nt, 1, tm), F32)),
        compiler_params=pltpu.CompilerParams(
            dimension_semantics=("arbitrary", "arbitrary"), vmem_limit_bytes=VMEM_LIMIT_BYTES),
    )(x, ypool, yattn, wo1, wo2, gx, wq, qn, kmem, vmem, wo, gf, wr, br)


META_NUSED = 64
ROW_DMA_UNROLL = 8


def _route_kernel(gsel_ref, pos_ref, meta_ref, *, tr):
    nb, w = gsel_ref.shape
    gsel = gsel_ref[...]
    lane = lax.broadcasted_iota(I32, (1, LANES), 1)
    before = (lax.broadcasted_iota(I32, (w, w), 0) < lax.broadcasted_iota(I32, (w, w), 1)).astype(BF16)
    onehot = [(gsel == g).astype(F32) for g in range(N_GROUPS)]
    cnt = sum(jnp.where(lane == g, jnp.sum(onehot[g], axis=1, keepdims=True), 0.0) for g in range(N_GROUPS))
    run = jnp.zeros((1, LANES), F32)
    carries = []
    for b in range(nb):
        carries.append(run)
        run = run + cnt[b:b + 1]
    carry = jnp.concatenate(carries, axis=0)
    padded = jnp.floor((run + (tr - 1)) * (1.0 / tr)) * tr
    size = [jnp.sum(jnp.where(lane == g, padded, 0.0), axis=1, keepdims=True) for g in range(N_GROUPS)]
    start = [sum(size[:g], jnp.zeros((1, 1), F32)) for g in range(N_GROUPS)]
    total = sum(size, jnp.zeros((1, 1), F32))
    pos = jnp.zeros((nb, w), F32)
    for g in range(N_GROUPS):
        rank = _dot(onehot[g].astype(BF16), before)
        base = jnp.sum(jnp.where(lane == g, carry, 0.0), axis=1, keepdims=True) + start[g]
        pos = pos + onehot[g] * (rank + base)
    pos_ref[...] = pos.astype(I32)
    tile_row = (lane * tr).astype(F32)
    tile_group = jnp.zeros((1, LANES), I32)
    last_group = jnp.zeros((1, 1), I32)
    for g in range(N_GROUPS):
        tile_group = tile_group + jnp.where((tile_row >= start[g]) & (tile_row < start[g] + size[g]), g, 0)
        last_group = jnp.where(size[g] > 0, g, last_group)
    tile_group = jnp.where(tile_row < total, tile_group, last_group)
    meta_ref[...] = jnp.where(lane == META_NUSED, (total * (1.0 / tr)).astype(I32), tile_group)


def _route_call(gsel, *, tr):
    nb, w = gsel.shape
    return pl.pallas_call(
        functools.partial(_route_kernel, tr=tr),
        out_shape=(jax.ShapeDtypeStruct((nb, w), I32), jax.ShapeDtypeStruct((1, LANES), I32)),
        compiler_params=pltpu.CompilerParams(vmem_limit_bytes=VMEM_LIMIT_BYTES),
    )(gsel)


def _scatter_rows_kernel(pos_ref, src_ref, zeros_ref, dst_ref, sem):
    del zeros_ref
    tile = src_ref.shape[0]
    base = pl.program_id(0) * tile

    def body(i, carry):
        for u in range(ROW_DMA_UNROLL):
            r = i * ROW_DMA_UNROLL + u
            pltpu.make_async_copy(
                src_ref.at[pl.ds(r, 1)], dst_ref.at[pl.ds(pos_ref[base + r], 1)], sem).start(priority=u % 2)
        return carry

    lax.fori_loop(0, tile // ROW_DMA_UNROLL, body, 0)
    pltpu.make_async_copy(src_ref, dst_ref.at[pl.ds(0, tile)], sem).wait()


def _scatter_rows_call(pos, src, *, n_out, tile):
    n, width = src.shape
    any_spec = pl.BlockSpec(memory_space=pl.ANY)
    return pl.pallas_call(
        _scatter_rows_kernel,
        grid_spec=pltpu.PrefetchScalarGridSpec(
            num_scalar_prefetch=1, grid=(n // tile,),
            in_specs=[pl.BlockSpec((tile, width), lambda i, pos: (i, 0)), any_spec], out_specs=any_spec,
            scratch_shapes=[pltpu.SemaphoreType.DMA(())]),
        out_shape=jax.ShapeDtypeStruct((n_out, width), src.dtype),
        input_output_aliases={2: 0},
        compiler_params=pltpu.CompilerParams(dimension_semantics=("arbitrary",), has_side_effects=True),
    )(pos, src, jnp.zeros((n_out, width), src.dtype))


def _gather_rows_kernel(pos_ref, src_ref, out_ref, sem):
    tile = out_ref.shape[0]
    base = pl.program_id(0) * tile

    def body(i, carry):
        for u in range(ROW_DMA_UNROLL):
            r = i * ROW_DMA_UNROLL + u
            pltpu.make_async_copy(
                src_ref.at[pl.ds(pos_ref[base + r], 1)], out_ref.at[pl.ds(r, 1)], sem).start(priority=u % 2)
        return carry

    lax.fori_loop(0, tile // ROW_DMA_UNROLL, body, 0)
    pltpu.make_async_copy(src_ref.at[pl.ds(0, tile)], out_ref, sem).wait()


def _gather_rows_call(pos, src, *, tile):
    n = pos.shape[0]
    width = src.shape[1]
    return pl.pallas_call(
        _gather_rows_kernel,
        grid_spec=pltpu.PrefetchScalarGridSpec(
            num_scalar_prefetch=1, grid=(n // tile,),
            in_specs=[pl.BlockSpec(memory_space=pl.ANY)],
            out_specs=pl.BlockSpec((tile, width), lambda i, pos: (i, 0)),
            scratch_shapes=[pltpu.SemaphoreType.DMA(())]),
        out_shape=jax.ShapeDtypeStruct((n, width), src.dtype),
        compiler_params=pltpu.CompilerParams(dimension_semantics=("arbitrary",)),
    )(pos, src)


def _expert_kernel(meta_ref, rows_ref, gf_ref, wg_ref, wu_ref, wd_ref, out_ref):
    j = pl.program_id(0)
    tr, d = out_ref.shape
    n_exp, _, ff = wg_ref.shape

    @pl.when(j < meta_ref[META_NUSED])
    def _():
        x2 = rows_ref[:, :d]
        gates = rows_ref[:, d:]
        h = _rms_rows(x2, gf_ref[...]).astype(BF16)
        lane = lax.broadcasted_iota(I32, (tr, LANES), 1)
        first = meta_ref[j] * n_exp
        hid = []
        for e in range(n_exp):
            a = _dot(h, wg_ref[e].astype(BF16))
            b = _dot(h, wu_ref[e].astype(BF16))
            gate = jnp.sum(jnp.where(lane == first + e, gates, 0.0), axis=-1, keepdims=True)
            hid.append((a * jax.nn.sigmoid(a) * b * gate).astype(BF16))
        hid = jnp.concatenate(hid, axis=-1)
        out_ref[...] = x2 + _dot(hid, wd_ref[...].reshape(n_exp * ff, d).astype(BF16))

    @pl.when(j >= meta_ref[META_NUSED])
    def _():
        out_ref[...] = jnp.zeros_like(out_ref)


def _expert_call(meta, rows, gf, wg, wu, wd, *, tr):
    n_rows, width = rows.shape
    d = width - LANES
    _, n_exp, _, ff = wg.shape
    by_group = lambda shape, buffers: pl.BlockSpec(
        (None,) + shape, lambda j, meta: (meta[j], 0, 0, 0), pipeline_mode=pl.Buffered(buffers))
    return pl.pallas_call(
        _expert_kernel,
        grid_spec=pltpu.PrefetchScalarGridSpec(
            num_scalar_prefetch=1, grid=(n_rows // tr,),
            in_specs=[pl.BlockSpec((tr, width), lambda j, meta: (j, 0)),
                      pl.BlockSpec(gf.shape, lambda j, meta: (0, 0)),
                      by_group((n_exp, d, ff), 2), by_group((n_exp, d, ff), 2), by_group((n_exp, ff, d), 1)],
            out_specs=pl.BlockSpec((tr, d), lambda j, meta: (j, 0))),
        out_shape=jax.ShapeDtypeStruct((n_rows, d), F32),
        compiler_params=pltpu.CompilerParams(dimension_semantics=("arbitrary",), vmem_limit_bytes=VMEM_LIMIT_BYTES),
    )(meta, rows, gf, wg, wu, wd)


def _rope_tables_t(seq_len, dim):
    inv = ROPE_THETA ** (-jnp.arange(0, dim, 2, dtype=F32) / dim)
    ang = jnp.arange(seq_len, dtype=F32)[:, None] * inv[None, :]
    ang = jnp.concatenate([ang, ang], axis=-1)
    sign = jnp.concatenate([-jnp.ones((dim // 2,), F32), jnp.ones((dim // 2,), F32)])
    return jnp.cos(ang).T, (jnp.sin(ang) * sign[None, :]).T


def _layer(x, mem, mix_norm, w_in, pool_w, pool_scale, q_norm, k_norm, idx_k_norm, w_out,
           xattn_norm, mem_norm, xattn_wq, xattn_wkv, xattn_q_norm, xattn_k_norm, xattn_wo,
           ffn_norm, router_group_w, router_group_b, router_expert_w, router_expert_b,
           expert_w_gate, expert_w_up, expert_w_down, *, tm, tm_mid, tq, tk, tr, tp):
    B, S, D = x.shape
    topk = min(TOPK_MAX, S // 4)
    row = lambda v: v.reshape(1, -1).astype(F32)
    col = lambda v: jnp.broadcast_to(v.astype(F32)[:, None], (v.shape[0], tm))

    wu = w_in[:, :POOL_WIDTH].astype(BF16)
    pad = lambda n: jnp.zeros((n, D), F32)
    wt = jnp.concatenate([
        w_in[:, OFF_Q:OFF_KI].T, w_in[:, OFF_KI:OFF_WI].T, pad(LANES - IDX_DIM),
        w_in[:, OFF_WI:IN_COLS].T, pad(2 * SUBLANES - IDX_HEADS)], axis=0).astype(BF16)
    cos_t, sin_t = _rope_tables_t(S, HEAD_DIM)

    ypool, qt, k, vt, qit, ki, wit = _proj_call(
        x, row(mix_norm), wu, wt, pool_w.astype(BF16), row(pool_scale), col(q_norm), col(k_norm),
        col(idx_k_norm), cos_t, sin_t, tm=tm, tk=tk)
    yattn = _dsa_call(qit, wit, qt, ki, k, vt, tq=tq, topk=topk)
    kmem, vmem = _memkv_call(mem, row(mem_norm), xattn_wkv.astype(BF16), row(xattn_k_norm))

    n_logits = N_GROUPS + N_EXPERTS
    wr = jnp.concatenate([router_group_w, router_expert_w, jnp.zeros((D, LANES - n_logits), F32)], axis=1)
    br = jnp.concatenate([router_group_b, router_expert_b, jnp.zeros((LANES - n_logits,), F32)]).reshape(1, LANES)
    rows, gsel = _mid_call(
        x, ypool, yattn, w_out[:POOL_WIDTH].astype(BF16), w_out[POOL_WIDTH:].astype(BF16), row(xattn_norm),
        xattn_wq.astype(BF16), row(xattn_q_norm), kmem, vmem, xattn_wo.astype(BF16), row(ffn_norm), wr, br,
        tm=tm_mid)

    n_tok = B * S
    n_sorted = n_tok + N_GROUPS * tr
    pos, meta = _route_call(gsel.reshape(n_tok // tm_mid, tm_mid), tr=tr)
    pos = pos.reshape(n_tok)
    sorted_rows = _scatter_rows_call(pos, rows.reshape(n_tok, D + LANES), n_out=n_sorted, tile=tp)
    sorted_out = _expert_call(
        meta.reshape(LANES), sorted_rows, row(ffn_norm), expert_w_gate, expert_w_up, expert_w_down, tr=tr)
    out = _gather_rows_call(pos, sorted_out, tile=tp)
    return out.reshape(B, S, D)


def kernel(x, mem, mix_norm, w_in, pool_w, pool_scale, q_norm, k_norm, idx_k_norm, w_out, xattn_norm, mem_norm,
           xattn_wq, xattn_wkv, xattn_q_norm, xattn_k_norm, xattn_wo, ffn_norm, router_group_w, router_group_b,
           router_expert_w, router_expert_b, expert_w_gate, expert_w_up, expert_w_down):
    depth = mix_norm.shape[0]
    for l in range(depth):
        x = _layer(
            x, mem, mix_norm[l], w_in[l], pool_w[l], pool_scale[l], q_norm[l], k_norm[l], idx_k_norm[l], w_out[l],
            xattn_norm[l], mem_norm[l], xattn_wq[l], xattn_wkv[l], xattn_q_norm[l], xattn_k_norm[l], xattn_wo[l],
            ffn_norm[l], router_group_w[l], router_group_b[l], router_expert_w[l], router_expert_b[l],
            expert_w_gate[l], expert_w_up[l], expert_w_down[l], tm=1024, tm_mid=1024, tq=512, tk=512, tr=512, tp=2048)
    return x
```

```python
import functools
import math

import jax
import jax.numpy as jnp
from jax import lax
from jax.experimental import pallas as pl
from jax.experimental.pallas import tpu as pltpu

CHUNK = 64
POOL_WINDOWS = (2, 4, 8, 16)
POOL_GROUP = 128
POOL_WIDTH = POOL_GROUP * len(POOL_WINDOWS)
ATTN_HEADS = 8
HEAD_DIM = 64
ATTN_WIDTH = ATTN_HEADS * HEAD_DIM
IDX_HEADS = 8
IDX_DIM = 64
TOPK_MAX = 256
ROPE_THETA = 10000.0
MEM_HEADS = 4
MEM_HEAD_DIM = 128
MEM_WIDTH = MEM_HEADS * MEM_HEAD_DIM
N_GROUPS = 4
EXPERTS_PER_GROUP = 8
N_EXPERTS = N_GROUPS * EXPERTS_PER_GROUP
EPS = 1e-6
OFF_Q = POOL_WIDTH
OFF_K = OFF_Q + ATTN_WIDTH
OFF_V = OFF_K + ATTN_WIDTH
OFF_QI = OFF_V + ATTN_WIDTH
OFF_KI = OFF_QI + IDX_HEADS * IDX_DIM
OFF_WI = OFF_KI + IDX_DIM
IN_COLS = OFF_WI + IDX_HEADS

LANES = 128
SUBLANES = 8
VMEM_LIMIT_BYTES = 56 * 1024 * 1024

HALO = 16
PACKED_SUBLANES = 2 * SUBLANES
NEG_BIG = -1e30
Q_SCALE = HEAD_DIM ** -0.5 * math.log2(math.e)

F32 = jnp.float32
BF16 = jnp.bfloat16
I32 = jnp.int32


def _dot(a, b):
    return jnp.dot(a, b, preferred_element_type=F32)


def _dot_nt(a, b):
    return lax.dot_general(a, b, (((1,), (1,)), ((), ())), preferred_element_type=F32)


def _rms_rows(x, g):
    ms = jnp.mean(x * x, axis=-1, keepdims=True)
    return x * lax.rsqrt(ms + EPS) * g


ROW_Q = 0
ROW_K = ROW_Q + ATTN_WIDTH
ROW_V = ROW_K + ATTN_WIDTH
ROW_QI = ROW_V + ATTN_WIDTH
ROW_KI = ROW_QI + IDX_HEADS * IDX_DIM
ROW_WI = ROW_KI + LANES
ROWS_T = ROW_WI + 2 * SUBLANES


def _norm_rope_t(z, gain, cos, sin_signed):
    if gain is not None:
        ms = jnp.mean(z * z, axis=0, keepdims=True)
        z = z * lax.rsqrt(ms + EPS) * gain
    half = z.shape[0] // 2
    swapped = jnp.concatenate([z[half:], z[:half]], axis=0)
    return z * cos + swapped * sin_signed


def _proj_kernel(x_ref, g_ref, wu_ref, wt_ref, poolw_ref, pscale_ref, qg_ref, kg_ref, ig_ref,
                 cos_ref, sin_ref,
                 ypool_ref, qt_ref, k_ref, vt_ref, qit_ref, ki_ref, wit_ref,
                 halo_ref, ext_ref, kt_ref, *, idx_scale):
    tm = x_ref.shape[0]
    h = _rms_rows(x_ref[...], g_ref[...]).astype(BF16)
    cos = cos_ref[...]
    sin = sin_ref[...]

    u = _dot(h, wu_ref[...])
    @pl.when(pl.program_id(1) == 0)
    def _():
        halo_ref[...] = jnp.zeros_like(halo_ref)

    ext_ref[0:HALO, :] = halo_ref[...]
    ext_ref[HALO:HALO + tm, :] = u
    halo_ref[...] = u[tm - HALO:, :]
    pos1 = pl.program_id(1) * tm + lax.broadcasted_iota(I32, (tm, 1), 0) + 1
    for g, w in enumerate(POOL_WINDOWS):
        cols = slice(g * POOL_GROUP, (g + 1) * POOL_GROUP)
        win = u[:, cols]
        for j in range(1, w):
            win = win + ext_ref[HALO - j:HALO - j + tm, cols]
        cnt = jnp.minimum(pos1, w).astype(F32)
        mixed = (win / cnt - u[:, cols]).astype(BF16)
        y = _dot(mixed, poolw_ref[g]) * pscale_ref[:, cols]
        ypool_ref[:, cols] = y.astype(ypool_ref.dtype)

    qg = qg_ref[...]
    kg = kg_ref[...]
    zq = _dot_nt(wt_ref[ROW_Q:ROW_K, :], h)
    for hd in range(ATTN_HEADS):
        rows = slice(hd * HEAD_DIM, (hd + 1) * HEAD_DIM)
        qt_ref[rows, :] = (_norm_rope_t(zq[rows], qg, cos, sin) * Q_SCALE).astype(qt_ref.dtype)
    zk = _dot_nt(wt_ref[ROW_K:ROW_V, :], h)
    for hd in range(ATTN_HEADS):
        rows = slice(hd * HEAD_DIM, (hd + 1) * HEAD_DIM)
        kt_ref[rows, :] = _norm_rope_t(zk[rows], kg, cos, sin)
    k_ref[...] = kt_ref[...].T.astype(k_ref.dtype)
    vt = _dot_nt(wt_ref[ROW_V:ROW_QI, :], h).astype(vt_ref.dtype)
    tk = vt_ref.shape[-1]
    for c in range(vt_ref.shape[0]):
        vt_ref[c] = vt[:, c * tk:(c + 1) * tk]
    zqi = _dot_nt(wt_ref[ROW_QI:ROW_KI, :], h)
    for hd in range(IDX_HEADS):
        rows = slice(hd * IDX_DIM, (hd + 1) * IDX_DIM)
        qit_ref[rows, :] = _norm_rope_t(zqi[rows], None, cos, sin).astype(qit_ref.dtype)
    zi = _dot_nt(wt_ref[ROW_KI:ROWS_T, :], h)
    kit = _norm_rope_t(zi[0:IDX_DIM], ig_ref[...], cos, sin)
    kit = jnp.concatenate([kit, jnp.zeros((LANES - IDX_DIM, tm), F32)], axis=0)
    ki_ref[...] = kit.T.astype(ki_ref.dtype)
    wit_ref[...] = zi[LANES:LANES + IDX_HEADS] * idx_scale


def _proj_call(x, g, wu, wt, poolw, pscale, qg, kg, ig, cos_t, sin_t, *, tm, tk):
    B, S, D = x.shape
    nt = S // tm
    kernel = functools.partial(_proj_kernel, idx_scale=(IDX_DIM ** -0.5) * (IDX_HEADS ** -0.5))
    full = lambda shape: pl.BlockSpec(shape, lambda b, i: (0,) * len(shape))
    out_shape = (
        jax.ShapeDtypeStruct((B, S, POOL_WIDTH), BF16),
        jax.ShapeDtypeStruct((B, ATTN_WIDTH, S), BF16),
        jax.ShapeDtypeStruct((B, S, ATTN_WIDTH), BF16),
        jax.ShapeDtypeStruct((B, S // tk, ATTN_WIDTH, tk), BF16),
        jax.ShapeDtypeStruct((B, IDX_HEADS * IDX_DIM, S), BF16),
        jax.ShapeDtypeStruct((B, S, LANES), BF16),
        jax.ShapeDtypeStruct((B, IDX_HEADS, S), F32),
    )
    return pl.pallas_call(
        kernel,
        grid=(B, nt),
        in_specs=[
            pl.BlockSpec((None, tm, D), lambda b, i: (b, i, 0)),
            full(g.shape), full(wu.shape), full(wt.shape), full(poolw.shape), full(pscale.shape),
            full(qg.shape), full(kg.shape), full(ig.shape),
            pl.BlockSpec((HEAD_DIM, tm), lambda b, i: (0, i)),
            pl.BlockSpec((HEAD_DIM, tm), lambda b, i: (0, i)),
        ],
        out_specs=(
            pl.BlockSpec((None, tm, POOL_WIDTH), lambda b, i: (b, i, 0)),
            pl.BlockSpec((None, ATTN_WIDTH, tm), lambda b, i: (b, 0, i)),
            pl.BlockSpec((None, tm, ATTN_WIDTH), lambda b, i: (b, i, 0)),
            pl.BlockSpec((None, tm // tk, ATTN_WIDTH, tk), lambda b, i: (b, i, 0, 0)),
            pl.BlockSpec((None, IDX_HEADS * IDX_DIM, tm), lambda b, i: (b, 0, i)),
            pl.BlockSpec((None, tm, LANES), lambda b, i: (b, i, 0)),
            pl.BlockSpec((None, IDX_HEADS, tm), lambda b, i: (b, 0, i)),
        ),
        out_shape=out_shape,
        scratch_shapes=[
            pltpu.VMEM((HALO, POOL_WIDTH), F32),
            pltpu.VMEM((HALO + tm, POOL_WIDTH), F32),
            pltpu.VMEM((ATTN_WIDTH, tm), F32),
        ],
        compiler_params=pltpu.CompilerParams(
            dimension_semantics=("arbitrary", "arbitrary"), vmem_limit_bytes=VMEM_LIMIT_BYTES),
    )(x, g, wu, wt, poolw, pscale, qg, kg, ig, cos_t, sin_t)


KEY_NEG_INF = -0x7F800000


def _key_to_float(key):
    key = jnp.maximum(key, KEY_NEG_INF)
    bits = jnp.where(key < 0, (key - 1) ^ jnp.int32(0x7FFFFFFF), key)
    return lax.bitcast_convert_type(bits, F32)


def _colsum8(x):
    rows, t = x.shape
    return jnp.sum(x.reshape(rows // SUBLANES, SUBLANES, t), axis=0)


def _dsa_kernel(qit_ref, wit_ref, qt_ref, ki_ref, k_ref, vt_ref, out_ref, sc_ref, coarse_ref, ot_ref, m_ref, l_ref,
                *, topk):
    tq = qit_ref.shape[1]
    nkt_max, tk, _ = sc_ref.shape
    q0 = pl.program_id(1) * tq
    nkt = (q0 + tq + tk - 1) // tk
    qpos = q0 + lax.broadcasted_iota(I32, (1, tq), 1)
    qend = (qpos // CHUNK + 1) * CHUNK
    krow = lax.broadcasted_iota(I32, (tk, 1), 0)
    zeros_half = jnp.zeros((LANES - IDX_DIM, tq), BF16)

    def score_tile(kt, carry):
        ki_t = ki_ref[pl.ds(pl.multiple_of(kt * tk, tk), tk), :]
        acc = jnp.zeros((tk, tq), F32)
        for hd in range(IDX_HEADS):
            qh = jnp.concatenate([qit_ref[hd * IDX_DIM:(hd + 1) * IDX_DIM, :], zeros_half], axis=0)
            rel = jnp.maximum(_dot(ki_t, qh), 0.0)
            acc = acc + rel * wit_ref[hd:hd + 1, :]
        score = jnp.where(kt * tk + krow < qend, acc, -jnp.inf)
        sc_ref[kt] = score
        coarse_ref[kt] = score.astype(BF16)
        return carry

    lax.fori_loop(0, nkt, score_tile, 0)

    def count(pred_fn):
        def body(kt, c8):
            return c8 + _colsum8(jnp.where(pred_fn(sc_ref[kt], kt), 1, 0).astype(I32))
        c8 = lax.fori_loop(0, nkt, body, jnp.zeros((SUBLANES, tq), I32))
        return jnp.sum(c8, axis=0, keepdims=True)

    def count_coarse(cand):
        def body(kt, c16):
            ones = jnp.where(coarse_ref[kt] >= cand, jnp.int16(1), jnp.int16(0))
            for j in range(tk // PACKED_SUBLANES):
                c16 = c16 + ones[j * PACKED_SUBLANES:(j + 1) * PACKED_SUBLANES]
            return c16
        c16 = lax.fori_loop(0, nkt, body, jnp.zeros((PACKED_SUBLANES, tq), jnp.int16))
        return jnp.sum(c16.astype(I32), axis=0, keepdims=True)

    def coarse_step(i, k):
        cand = k + lax.shift_left(jnp.int32(1), 15 - i)
        c = count_coarse(_key_to_float(cand * 65536).astype(BF16))
        return jnp.where(c >= topk, cand, k)

    k_coarse = lax.fori_loop(0, 16, coarse_step, jnp.full((1, tq), -(2 ** 15), I32))

    def fine_step(i, carry):
        t, n = carry
        cand = t + lax.shift_left(jnp.int32(1), 16 - i)
        cand_f = _key_to_float(cand)
        c = count(lambda s, kt: s >= cand_f)
        return jnp.where(c >= topk, cand, t), jnp.where(c >= topk, c, n)

    thr_key, n_sel = lax.fori_loop(0, 17, fine_step, ((k_coarse - 1) * 65536, jnp.zeros((1, tq), I32)))
    thr = _key_to_float(thr_key)
    few = thr == -jnp.inf
    thr_adm = jnp.where(few, jnp.finfo(F32).min, thr)
    excess_ties = jnp.max(jnp.where(few, 0, n_sel)) > topk

    def store_mask(sel_fn):
        def body(kt, carry):
            sc_ref[kt] = jnp.where(sel_fn(sc_ref[kt], kt), 0.0, NEG_BIG).astype(F32)
            return carry
        lax.fori_loop(0, nkt, body, 0)

    @pl.when(jnp.logical_not(excess_ties))
    def _():
        store_mask(lambda s, kt: s >= thr_adm)

    @pl.when(excess_ties)
    def _():
        need = topk - count(lambda s, kt: s > thr)
        nbits = max(1, (nkt_max * tk - 1).bit_length())

        def tie_step(step, cut):
            cand = cut + lax.shift_left(jnp.int32(1), nbits - 1 - step)
            c = count(lambda s, kt: (s == thr) & (kt * tk + krow < cand))
            return jnp.where(c < need, cand, cut)

        cut = lax.fori_loop(0, nbits, tie_step, jnp.zeros((1, tq), I32))
        cut = jnp.where(few, -1, cut)
        store_mask(lambda s, kt: (s > thr_adm) | ((s == thr) & (kt * tk + krow <= cut)))

    zeros_head = jnp.zeros((HEAD_DIM, tq), BF16)
    ones_rows = jnp.ones((PACKED_SUBLANES, tk), BF16)

    def masked_logits(kt, hd):
        qh = qt_ref[hd * HEAD_DIM:(hd + 1) * HEAD_DIM, :]
        qh = jnp.concatenate([qh, zeros_head] if hd % 2 == 0 else [zeros_head, qh], axis=0)
        k2 = k_ref[pl.ds(pl.multiple_of(kt * tk, tk), tk), (hd // 2) * LANES:(hd // 2 + 1) * LANES]
        return _dot(k2, qh) + sc_ref[kt]

    def weighted_values(kt, hd, p):
        pv = _dot(jnp.concatenate([vt_ref[kt, hd * HEAD_DIM:(hd + 1) * HEAD_DIM, :], ones_rows], axis=0), p)
        return pv[:HEAD_DIM, :], pv[HEAD_DIM:HEAD_DIM + 1, :]

    def attn_tile_plain(kt, carry):
        for hd in range(ATTN_HEADS):
            rows = slice(hd * HEAD_DIM, (hd + 1) * HEAD_DIM)
            o, l = weighted_values(kt, hd, jnp.exp2(masked_logits(kt, hd)).astype(BF16))
            l_ref[hd:hd + 1, :] += l
            ot_ref[rows, :] += o
        return carry

    def attn_tile_online(kt, carry):
        for hd in range(ATTN_HEADS):
            rows = slice(hd * HEAD_DIM, (hd + 1) * HEAD_DIM)
            s = masked_logits(kt, hd)
            m_old = m_ref[hd:hd + 1, :]
            m_new = jnp.maximum(m_old, jnp.max(s, axis=0, keepdims=True))
            alpha = jnp.exp2(m_old - m_new)
            m_ref[hd:hd + 1, :] = m_new
            o, l = weighted_values(kt, hd, jnp.exp2(s - m_new).astype(BF16))
            l_ref[hd:hd + 1, :] = alpha * l_ref[hd:hd + 1, :] + l
            ot_ref[rows, :] = alpha * ot_ref[rows, :] + o
        return carry

    l_ref[...] = jnp.zeros(l_ref.shape, F32)
    ot_ref[...] = jnp.zeros(ot_ref.shape, F32)
    lax.fori_loop(0, nkt, attn_tile_plain, 0)
    l_all = l_ref[...]
    unusable = jnp.where((l_all > 0.0) & (l_all < jnp.finfo(F32).max), 0, 1)

    @pl.when(jnp.max(unusable) > 0)
    def _():
        m_ref[...] = jnp.full(m_ref.shape, NEG_BIG, F32)
        l_ref[...] = jnp.zeros(l_ref.shape, F32)
        ot_ref[...] = jnp.zeros(ot_ref.shape, F32)
        lax.fori_loop(0, nkt, attn_tile_online, 0)

    for hd in range(ATTN_HEADS):
        rows = slice(hd * HEAD_DIM, (hd + 1) * HEAD_DIM)
        ot_ref[rows, :] = ot_ref[rows, :] / l_ref[hd:hd + 1, :]
    out_ref[...] = ot_ref[...].T.astype(out_ref.dtype)


def _dsa_call(qit, wit, qt, ki, k, vt, *, tq, topk):
    B, S, _ = k.shape
    _, nkt, _, tk = vt.shape
    kernel = functools.partial(_dsa_kernel, topk=topk)
    return pl.pallas_call(
        kernel,
        grid=(B, S // tq),
        in_specs=[
            pl.BlockSpec((None, IDX_HEADS * IDX_DIM, tq), lambda b, i: (b, 0, i)),
            pl.BlockSpec((None, IDX_HEADS, tq), lambda b, i: (b, 0, i)),
            pl.BlockSpec((None, ATTN_WIDTH, tq), lambda b, i: (b, 0, i)),
            pl.BlockSpec((None, S, LANES), lambda b, i: (b, 0, 0)),
            pl.BlockSpec((None, S, ATTN_WIDTH), lambda b, i: (b, 0, 0)),
            pl.BlockSpec((None, nkt, ATTN_WIDTH, tk), lambda b, i: (b, 0, 0, 0)),
        ],
        out_specs=pl.BlockSpec((None, tq, ATTN_WIDTH), lambda b, i: (b, i, 0)),
        out_shape=jax.ShapeDtypeStruct((B, S, ATTN_WIDTH), BF16),
        scratch_shapes=[
            pltpu.VMEM((nkt, tk, tq), F32),
            pltpu.VMEM((nkt, tk, tq), BF16),
            pltpu.VMEM((ATTN_WIDTH, tq), F32),
            pltpu.VMEM((ATTN_HEADS, tq), F32),
            pltpu.VMEM((ATTN_HEADS, tq), F32),
        ],
        compiler_params=pltpu.CompilerParams(
            dimension_semantics=("arbitrary", "arbitrary"), vmem_limit_bytes=VMEM_LIMIT_BYTES),
    )(qit, wit, qt, ki, k, vt)


def _memkv_kernel(mem_ref, g_ref, wkv_ref, kn_ref, k_ref, v_ref):
    m = _rms_rows(mem_ref[...], g_ref[...]).astype(BF16)
    kv = _dot(m, wkv_ref[...])
    for hd in range(MEM_HEADS):
        cols = slice(hd * MEM_HEAD_DIM, (hd + 1) * MEM_HEAD_DIM)
        k_ref[:, cols] = _rms_rows(kv[:, cols], kn_ref[...]).astype(k_ref.dtype)
    v_ref[...] = kv[:, MEM_WIDTH:].astype(v_ref.dtype)


def _memkv_call(mem, g, wkv, kn):
    B, M, D = mem.shape
    full = lambda shape: pl.BlockSpec(shape, lambda b: (0,) * len(shape))
    return pl.pallas_call(
        _memkv_kernel,
        grid=(B,),
        in_specs=[pl.BlockSpec((None, M, D), lambda b: (b, 0, 0)), full(g.shape), full(wkv.shape), full(kn.shape)],
        out_specs=(pl.BlockSpec((None, M, MEM_WIDTH), lambda b: (b, 0, 0)),
                   pl.BlockSpec((None, M, MEM_WIDTH), lambda b: (b, 0, 0))),
        out_shape=(jax.ShapeDtypeStruct((B, M, MEM_WIDTH), BF16), jax.ShapeDtypeStruct((B, M, MEM_WIDTH), BF16)),
        compiler_params=pltpu.CompilerParams(dimension_semantics=("arbitrary",), vmem_limit_bytes=VMEM_LIMIT_BYTES),
    )(mem, g, wkv, kn)


def _split_bf16(a):
    hi = a.astype(BF16)
    lo = (a - hi.astype(F32)).astype(BF16)
    return hi, lo


def _mid_kernel(x_ref, yp_ref, ya_ref, wo1_ref, wo2_ref, gx_ref, wq_ref, qn_ref, km_ref, vm_ref, wo_ref,
                gf_ref, wr_ref, br_ref, rows_ref, gsel_ref):
    tm, d = x_ref.shape
    x1 = x_ref[...] + _dot(yp_ref[...], wo1_ref[...]) + _dot(ya_ref[...], wo2_ref[...])

    h = _rms_rows(x1, gx_ref[...]).astype(BF16)
    q = _dot(h, wq_ref[...])
    heads = []
    for hd in range(MEM_HEADS):
        cols = slice(hd * MEM_HEAD_DIM, (hd + 1) * MEM_HEAD_DIM)
        qh = (_rms_rows(q[:, cols], qn_ref[...]) * (MEM_HEAD_DIM ** -0.5)).astype(BF16)
        s = _dot_nt(qh, km_ref[:, cols])
        p = jnp.exp(s - jnp.max(s, axis=-1, keepdims=True))
        p = p / jnp.sum(p, axis=-1, keepdims=True)
        heads.append(_dot(p.astype(BF16), vm_ref[:, cols]))
    o = jnp.concatenate(heads, axis=-1).astype(BF16)
    x2 = x1 + _dot(o, wo_ref[...])
    rows_ref[:, :d] = x2

    h_hi, h_lo = _split_bf16(_rms_rows(x2, gf_ref[...]))
    w_hi, w_lo = _split_bf16(wr_ref[...])
    hw = _dot(h_hi, jnp.concatenate([w_hi, w_lo], axis=1))
    logits = hw[:, :LANES] + (hw[:, LANES:] + _dot(h_lo, w_hi)) + br_ref[...]
    lane = lax.broadcasted_iota(I32, (tm, LANES), 1).astype(F32)
    neg_inf = -jnp.inf
    g_logit = jnp.where(lane < N_GROUPS, logits, neg_inf)
    g_max = jnp.max(g_logit, axis=-1, keepdims=True)
    g_sel = jnp.min(jnp.where(g_logit == g_max, lane, LANES), axis=-1, keepdims=True)
    g_w = 1.0 / jnp.sum(jnp.exp(g_logit - g_max), axis=-1, keepdims=True)
    e_lo = N_GROUPS + g_sel * EXPERTS_PER_GROUP
    in_group = (lane >= e_lo) & (lane < e_lo + EXPERTS_PER_GROUP)
    e_logit = jnp.where(in_group, logits, neg_inf)
    v1 = jnp.max(e_logit, axis=-1, keepdims=True)
    i1 = jnp.min(jnp.where(e_logit == v1, lane, LANES), axis=-1, keepdims=True)
    rest = jnp.where(lane == i1, neg_inf, e_logit)
    v2 = jnp.max(rest, axis=-1, keepdims=True)
    i2 = jnp.min(jnp.where(rest == v2, lane, LANES), axis=-1, keepdims=True)
    e2 = jnp.exp(v2 - v1)
    w1 = g_w / (1.0 + e2)
    w2 = g_w * e2 / (1.0 + e2)
    gates = jnp.where(lane == i1 - N_GROUPS, w1, 0.0) + jnp.where(lane == i2 - N_GROUPS, w2, 0.0)
    rows_ref[:, d:] = gates
    gsel_ref[...] = jnp.broadcast_to(g_sel, (tm, LANES)).T[0:1, :]


def _mid_call(x, ypool, yattn, wo1, wo2, gx, wq, qn, kmem, vmem, wo, gf, wr, br, *, tm):
    B, S, D = x.shape
    M = kmem.shape[1]
    nt = S // tm
    full = lambda shape: pl.BlockSpec(shape, lambda b, i: (0,) * len(shape))
    tile = lambda width: pl.BlockSpec((None, tm, width), lambda b, i: (b, i, 0))
    return pl.pallas_call(
        _mid_kernel,
        grid=(B, nt),
        in_specs=[tile(D), tile(POOL_WIDTH), tile(ATTN_WIDTH), full(wo1.shape), full(wo2.shape), full(gx.shape),
                  full(wq.shape), full(qn.shape),
                  pl.BlockSpec((None, M, MEM_WIDTH), lambda b, i: (b, 0, 0)),
                  pl.BlockSpec((None, M, MEM_WIDTH), lambda b, i: (b, 0, 0)),
                  full(wo.shape), full(gf.shape), full(wr.shape), full(br.shape)],
        out_specs=(tile(D + LANES), pl.BlockSpec((None, 1, tm), lambda b, i: (b * nt + i, 0, 0))),
        out_shape=(jax.ShapeDtypeStruct((B, S, D + LANES), F32),
                   jax.ShapeDtypeStruct((B * nt, 1, tm), F32)),
        compiler_params=pltpu.CompilerParams(
            dimension_semantics=("arbitrary", "arbitrary"), vmem_limit_bytes=VMEM_LIMIT_BYTES),
    )(x, ypool, yattn, wo1, wo2, gx, wq, qn, kmem, vmem, wo, gf, wr, br)


META_NUSED = 64


def _route_kernel(gsel_ref, pos_ref, meta_ref, *, tr):
    nb, w = gsel_ref.shape
    gsel = gsel_ref[...]
    lane = lax.broadcasted_iota(I32, (1, LANES), 1)
    before = (lax.broadcasted_iota(I32, (w, w), 0) < lax.broadcasted_iota(I32, (w, w), 1)).astype(BF16)
    onehot = [(gsel == g).astype(F32) for g in range(N_GROUPS)]
    cnt = sum(jnp.where(lane == g, jnp.sum(onehot[g], axis=1, keepdims=True), 0.0) for g in range(N_GROUPS))
    run = jnp.zeros((1, LANES), F32)
    carries = []
    for b in range(nb):
        carries.append(run)
        run = run + cnt[b:b + 1]
    carry = jnp.concatenate(carries, axis=0)
    padded = jnp.floor((run + (tr - 1)) * (1.0 / tr)) * tr
    size = [jnp.sum(jnp.where(lane == g, padded, 0.0), axis=1, keepdims=True) for g in range(N_GROUPS)]
    start = [sum(size[:g], jnp.zeros((1, 1), F32)) for g in range(N_GROUPS)]
    total = sum(size, jnp.zeros((1, 1), F32))
    pos = jnp.zeros((nb, w), F32)
    for g in range(N_GROUPS):
        rank = _dot(onehot[g].astype(BF16), before)
        base = jnp.sum(jnp.where(lane == g, carry, 0.0), axis=1, keepdims=True) + start[g]
        pos = pos + onehot[g] * (rank + base)
    pos_ref[...] = pos.astype(I32)
    tile_row = (lane * tr).astype(F32)
    tile_group = jnp.zeros((1, LANES), I32)
    last_group = jnp.zeros((1, 1), I32)
    for g in range(N_GROUPS):
        tile_group = tile_group + jnp.where((tile_row >= start[g]) & (tile_row < start[g] + size[g]), g, 0)
        last_group = jnp.where(size[g] > 0, g, last_group)
    tile_group = jnp.where(tile_row < total, tile_group, last_group)
    meta_ref[...] = jnp.where(lane == META_NUSED, (total * (1.0 / tr)).astype(I32), tile_group)


def _route_call(gsel, *, tr):
    nb, w = gsel.shape
    return pl.pallas_call(
        functools.partial(_route_kernel, tr=tr),
        out_shape=(jax.ShapeDtypeStruct((nb, w), I32), jax.ShapeDtypeStruct((1, LANES), I32)),
        compiler_params=pltpu.CompilerParams(vmem_limit_bytes=VMEM_LIMIT_BYTES),
    )(gsel)


def _scatter_rows_kernel(pos_ref, src_ref, zeros_ref, dst_ref, sem):
    del zeros_ref
    groups = src_ref.shape[0]
    base = pl.program_id(0) * groups * SUBLANES

    def body(i, carry):
        for u in range(SUBLANES):
            p = pos_ref[base + i * SUBLANES + u]
            pltpu.make_async_copy(
                src_ref.at[i, pl.ds(u, 1)],
                dst_ref.at[lax.shift_right_logical(p, 3), pl.ds(p & (SUBLANES - 1), 1)], sem).start(priority=u % 2)
        return carry

    lax.fori_loop(0, groups, body, 0)
    pltpu.make_async_copy(src_ref, dst_ref.at[pl.ds(0, groups)], sem).wait()


def _scatter_rows_call(pos, src, *, n_out, tile):
    n, width = src.shape
    any_spec = pl.BlockSpec(memory_space=pl.ANY)
    out = pl.pallas_call(
        _scatter_rows_kernel,
        grid_spec=pltpu.PrefetchScalarGridSpec(
            num_scalar_prefetch=1, grid=(n // tile,),
            in_specs=[pl.BlockSpec((tile // SUBLANES, SUBLANES, width), lambda i, pos: (i, 0, 0)), any_spec],
            out_specs=any_spec,
            scratch_shapes=[pltpu.SemaphoreType.DMA(())]),
        out_shape=jax.ShapeDtypeStruct((n_out // SUBLANES, SUBLANES, width), src.dtype),
        input_output_aliases={2: 0},
        compiler_params=pltpu.CompilerParams(dimension_semantics=("arbitrary",), has_side_effects=True),
    )(pos, src.reshape(n // SUBLANES, SUBLANES, width), jnp.zeros((n_out // SUBLANES, SUBLANES, width), src.dtype))
    return out.reshape(n_out, width)


def _gather_rows_kernel(pos_ref, src_ref, out_ref, sem):
    groups = out_ref.shape[0]
    base = pl.program_id(0) * groups * SUBLANES

    def body(i, carry):
        for u in range(SUBLANES):
            p = pos_ref[base + i * SUBLANES + u]
            pltpu.make_async_copy(
                src_ref.at[lax.shift_right_logical(p, 3), pl.ds(p & (SUBLANES - 1), 1)],
                out_ref.at[i, pl.ds(u, 1)], sem).start(priority=u % 2)
        return carry

    lax.fori_loop(0, groups, body, 0)
    pltpu.make_async_copy(src_ref.at[pl.ds(0, groups)], out_ref, sem).wait()


def _gather_rows_call(pos, src, *, tile):
    n = pos.shape[0]
    m, width = src.shape
    out = pl.pallas_call(
        _gather_rows_kernel,
        grid_spec=pltpu.PrefetchScalarGridSpec(
            num_scalar_prefetch=1, grid=(n // tile,),
            in_specs=[pl.BlockSpec(memory_space=pl.ANY)],
            out_specs=pl.BlockSpec((tile // SUBLANES, SUBLANES, width), lambda i, pos: (i, 0, 0)),
            scratch_shapes=[pltpu.SemaphoreType.DMA(())]),
        out_shape=jax.ShapeDtypeStruct((n // SUBLANES, SUBLANES, width), src.dtype),
        compiler_params=pltpu.CompilerParams(dimension_semantics=("arbitrary",)),
    )(pos, src.reshape(m // SUBLANES, SUBLANES, width))
    return out.reshape(n, width)


def _expert_kernel(meta_ref, rows_ref, gf_ref, wg_ref, wu_ref, wd_ref, out_ref):
    j = pl.program_id(0)
    tr, d = out_ref.shape
    n_exp, _, ff = wg_ref.shape

    @pl.when(j < meta_ref[META_NUSED])
    def _():
        x2 = rows_ref[:, :d]
        gates = rows_ref[:, d:]
        h = _rms_rows(x2, gf_ref[...]).astype(BF16)
        lane = lax.broadcasted_iota(I32, (tr, LANES), 1)
        first = meta_ref[j] * n_exp
        hid = []
        for e in range(n_exp):
            a = _dot(h, wg_ref[e].astype(BF16))
            b = _dot(h, wu_ref[e].astype(BF16))
            gate = jnp.sum(jnp.where(lane == first + e, gates, 0.0), axis=-1, keepdims=True)
            hid.append((a * jax.nn.sigmoid(a) * b * gate).astype(BF16))
        hid = jnp.concatenate(hid, axis=-1)
        out_ref[...] = x2 + _dot(hid, wd_ref[...].reshape(n_exp * ff, d).astype(BF16))

    @pl.when(j >= meta_ref[META_NUSED])
    def _():
        out_ref[...] = jnp.zeros_like(out_ref)


def _expert_call(meta, rows, gf, wg, wu, wd, *, tr):
    n_rows, width = rows.shape
    d = width - LANES
    _, n_exp, _, ff = wg.shape
    by_group = lambda shape, buffers: pl.BlockSpec(
        (None,) + shape, lambda j, meta: (meta[j], 0, 0, 0), pipeline_mode=pl.Buffered(buffers))
    return pl.pallas_call(
        _expert_kernel,
        grid_spec=pltpu.PrefetchScalarGridSpec(
            num_scalar_prefetch=1, grid=(n_rows // tr,),
            in_specs=[pl.BlockSpec((tr, width), lambda j, meta: (j, 0)),
                      pl.BlockSpec(gf.shape, lambda j, meta: (0, 0)),
                      by_group((n_exp, d, ff), 2), by_group((n_exp, d, ff), 2), by_group((n_exp, ff, d), 1)],
            out_specs=pl.BlockSpec((tr, d), lambda j, meta: (j, 0))),
        out_shape=jax.ShapeDtypeStruct((n_rows, d), F32),
        compiler_params=pltpu.CompilerParams(dimension_semantics=("arbitrary",), vmem_limit_bytes=VMEM_LIMIT_BYTES),
    )(meta, rows, gf, wg, wu, wd)


def _rope_tables_t(seq_len, dim):
    inv = ROPE_THETA ** (-jnp.arange(0, dim, 2, dtype=F32) / dim)
    ang = jnp.arange(seq_len, dtype=F32)[:, None] * inv[None, :]
    ang = jnp.concatenate([ang, ang], axis=-1)
    sign = jnp.concatenate([-jnp.ones((dim // 2,), F32), jnp.ones((dim // 2,), F32)])
    return jnp.cos(ang).T, (jnp.sin(ang) * sign[None, :]).T


def _layer(x, mem, mix_norm, w_in, pool_w, pool_scale, q_norm, k_norm, idx_k_norm, w_out,
           xattn_norm, mem_norm, xattn_wq, xattn_wkv, xattn_q_norm, xattn_k_norm, xattn_wo,
           ffn_norm, router_group_w, router_group_b, router_expert_w, router_expert_b,
           expert_w_gate, expert_w_up, expert_w_down, *, tm, tm_mid, tq, tk, tr, tp):
    B, S, D = x.shape
    topk = min(TOPK_MAX, S // 4)
    row = lambda v: v.reshape(1, -1).astype(F32)
    col = lambda v: jnp.broadcast_to(v.astype(F32)[:, None], (v.shape[0], tm))

    wu = w_in[:, :POOL_WIDTH].astype(BF16)
    pad = lambda n: jnp.zeros((n, D), F32)
    wt = jnp.concatenate([
        w_in[:, OFF_Q:OFF_KI].T, w_in[:, OFF_KI:OFF_WI].T, pad(LANES - IDX_DIM),
        w_in[:, OFF_WI:IN_COLS].T, pad(2 * SUBLANES - IDX_HEADS)], axis=0).astype(BF16)
    cos_t, sin_t = _rope_tables_t(S, HEAD_DIM)

    ypool, qt, k, vt, qit, ki, wit = _proj_call(
        x, row(mix_norm), wu, wt, pool_w.astype(BF16), row(pool_scale), col(q_norm), col(k_norm),
        col(idx_k_norm), cos_t, sin_t, tm=tm, tk=tk)
    yattn = _dsa_call(qit, wit, qt, ki, k, vt, tq=tq, topk=topk)
    kmem, vmem = _memkv_call(mem, row(mem_norm), xattn_wkv.astype(BF16), row(xattn_k_norm))

    n_logits = N_GROUPS + N_EXPERTS
    wr = jnp.concatenate([router_group_w, router_expert_w, jnp.zeros((D, LANES - n_logits), F32)], axis=1)
    br = jnp.concatenate([router_group_b, router_expert_b, jnp.zeros((LANES - n_logits,), F32)]).reshape(1, LANES)
    rows, gsel = _mid_call(
        x, ypool, yattn, w_out[:POOL_WIDTH].astype(BF16), w_out[POOL_WIDTH:].astype(BF16), row(xattn_norm),
        xattn_wq.astype(BF16), row(xattn_q_norm), kmem, vmem, xattn_wo.astype(BF16), row(ffn_norm), wr, br,
        tm=tm_mid)

    n_tok = B * S
    n_sorted = n_tok + N_GROUPS * tr
    pos, meta = _route_call(gsel.reshape(n_tok // tm_mid, tm_mid), tr=tr)
    pos = pos.reshape(n_tok)
    sorted_rows = _scatter_rows_call(pos, rows.reshape(n_tok, D + LANES), n_out=n_sorted, tile=tp)
    sorted_out = _expert_call(
        meta.reshape(LANES), sorted_rows, row(ffn_norm), expert_w_gate, expert_w_up, expert_w_down, tr=tr)
    out = _gather_rows_call(pos, sorted_out, tile=tp)
    return out.reshape(B, S, D)


def kernel(x, mem, mix_norm, w_in, pool_w, pool_scale, q_norm, k_norm, idx_k_norm, w_out, xattn_norm, mem_norm,
           xattn_wq, xattn_wkv, xattn_q_norm, xattn_k_norm, xattn_wo, ffn_norm, router_group_w, router_group_b,
           router_expert_w, router_expert_b, expert_w_gate, expert_w_up, expert_w_down):
    depth = mix_norm.shape[0]
    for l in range(depth):
        x = _layer(
            x, mem, mix_norm[l], w_in[l], pool_w[l], pool_scale[l], q_norm[l], k_norm[l], idx_k_norm[l], w_out[l],
            xattn_norm[l], mem_norm[l], xattn_wq[l], xattn_wkv[l], xattn_q_norm[l], xattn_k_norm[l], xattn_wo[l],
            ffn_norm[l], router_group_w[l], router_group_b[l], router_expert_w[l], router_expert_b[l],
            expert_w_gate[l], expert_w_up[l], expert_w_down[l], tm=1024, tm_mid=1024, tq=512, tk=512, tr=512, tp=2048)
    return x
```
